```python
import math
import jax, jax.numpy as jnp
from jax import lax
import numpy as np

D_MODEL = 1024
BATCH = 16
SEQ = 2048
DEPTH = 4

HEAD_DIM = 64
GDN_HEADS = 6
GDN_DIM = GDN_HEADS * HEAD_DIM
GDN_CONV = 4
GDN_CHUNK = 64
NSA_HEADS = 6
NSA_KV_HEADS = 2
NSA_GROUP = NSA_HEADS // NSA_KV_HEADS
NSA_DIM = NSA_HEADS * HEAD_DIM
NSA_KV_DIM = NSA_KV_HEADS * HEAD_DIM
CMP_STRIDE = 16
CMP_LEN = 2 * CMP_STRIDE
SEL_BLOCK = 64
SEL_TOPK = 8
WINDOW = 512
Q_BLOCK = 128
CONV_CH = 256
CONV_WIDTH = 31
MIX_DIM = GDN_DIM + NSA_DIM + CONV_CH
D_FF = 2816
FFN_CONV = 3
ROPE_THETA = 10000.0
EPS = 1e-6
NEG = -1e30
FORCE = 1e4
SPLIT_SIZES = (GDN_DIM, GDN_DIM, GDN_DIM, GDN_DIM, GDN_HEADS, GDN_HEADS,
               NSA_DIM, NSA_KV_DIM, NSA_KV_DIM, NSA_KV_DIM, NSA_KV_DIM, NSA_KV_DIM, NSA_KV_DIM,
               3 * NSA_HEADS, 2 * CONV_CH)
IN_DIM = sum(SPLIT_SIZES)

kernel_name = "hybrid_gdn_nsa_conformer_block"


def rms_norm(x, g):
    xf = x.astype(jnp.float32)
    y = xf * lax.rsqrt(jnp.mean(xf * xf, axis=-1, keepdims=True) + EPS)
    return (y * g.astype(jnp.float32)).astype(x.dtype)


def causal_dwconv(x, w):
    k = w.shape[0]
    xp = jnp.pad(x, ((0, 0), (k - 1, 0), (0, 0)))
    return lax.conv_general_dilated(xp, w[:, None, :].astype(x.dtype), window_strides=(1,), padding='VALID',
                                    dimension_numbers=('NWC', 'WIO', 'NWC'), feature_group_count=x.shape[-1])


def rope_tables(positions):
    inv = 1.0 / (ROPE_THETA ** (jnp.arange(0, HEAD_DIM, 2, dtype=jnp.float32) / HEAD_DIM))
    ang = positions.astype(jnp.float32)[..., None] * inv
    return jnp.cos(ang), jnp.sin(ang)


def apply_rope(x, cos, sin):
    x1, x2 = jnp.split(x.astype(jnp.float32), 2, axis=-1)
    c = cos[:, :, None]
    s = sin[:, :, None]
    return jnp.concatenate([x1 * c - x2 * s, x1 * s + x2 * c], axis=-1).astype(x.dtype)


def masked_softmax(s, valid):
    s = jnp.where(valid, s.astype(jnp.float32), NEG)
    return jnp.where(valid, jax.nn.softmax(s, axis=-1), 0.0)


def gated_deltanet(q, k, v, z, a, b, conv_w, a_log, dt_bias, norm_g):
    bsz, seq, _ = q.shape
    f32 = jnp.float32
    qkv = jax.nn.silu(causal_dwconv(jnp.concatenate([q, k, v], axis=-1), conv_w))
    q, k, v = jnp.split(qkv, 3, axis=-1)
    heads = lambda t: t.reshape(bsz, seq, GDN_HEADS, HEAD_DIM).transpose(0, 2, 1, 3).astype(f32)
    q, k, v = heads(q), heads(k), heads(v)
    q = q * lax.rsqrt(jnp.sum(q * q, -1, keepdims=True) + EPS) * (HEAD_DIM ** -0.5)
    k = k * lax.rsqrt(jnp.sum(k * k, -1, keepdims=True) + EPS)
    beta = jax.nn.sigmoid(b.astype(f32)).transpose(0, 2, 1)
    g = (-jnp.exp(a_log.astype(f32))[:, None]
         * jax.nn.softplus(a.astype(f32) + dt_bias.astype(f32)).transpose(0, 2, 1))
    n_chunks = seq // GDN_CHUNK
    ch = lambda t: t.reshape(bsz, GDN_HEADS, n_chunks, GDN_CHUNK, *t.shape[3:])
    q, k, v, beta, g = ch(q), ch(k), ch(v), ch(beta), ch(g)
    g = jnp.cumsum(g, axis=-1)
    tri = jnp.tril(jnp.ones((GDN_CHUNK, GDN_CHUNK), bool))
    tri_strict = jnp.tril(jnp.ones((GDN_CHUNK, GDN_CHUNK), bool), -1)
    decay = jnp.exp(jnp.where(tri, g[..., :, None] - g[..., None, :], -jnp.inf))
    kk = jnp.einsum('bhnid,bhnjd->bhnij', k, k)
    lower = jnp.where(tri_strict, beta[..., :, None] * kk * decay, 0.0)
    rhs = jnp.concatenate([v * beta[..., None], k * (beta * jnp.exp(g))[..., None]], axis=-1)
    sol = lax.linalg.triangular_solve(jnp.eye(GDN_CHUNK, dtype=f32) + lower, rhs,
                                      left_side=True, lower=True, unit_diagonal=True)
    u, w = jnp.split(sol, 2, axis=-1)
    qk = jnp.einsum('bhnid,bhnjd->bhnij', q, k) * decay

    def step(state, inp):
        q_c, k_c, u_c, w_c, g_c, qk_c = inp
        v_new = u_c - jnp.einsum('bhcd,bhde->bhce', w_c, state)
        o = (jnp.einsum('bhcd,bhde->bhce', q_c * jnp.exp(g_c)[..., None], state)
             + jnp.einsum('bhij,bhje->bhie', qk_c, v_new))
        g_last = g_c[..., -1:]
        state = (state * jnp.exp(g_last)[..., None]
                 + jnp.einsum('bhcd,bhce->bhde', k_c * jnp.exp(g_last - g_c)[..., None], v_new))
        return state, o

    xs = tuple(jnp.moveaxis(t, 2, 0) for t in (q, k, u, w, g, qk))
    state0 = jnp.zeros((bsz, GDN_HEADS, HEAD_DIM, HEAD_DIM), f32)
    _, o = lax.scan(step, state0, xs)
    o = o.transpose(1, 0, 3, 2, 4).reshape(bsz, seq, GDN_HEADS, HEAD_DIM)
    o = rms_norm(o, norm_g) * jax.nn.silu(z.astype(f32).reshape(bsz, seq, GDN_HEADS, HEAD_DIM))
    return o.reshape(bsz, seq, GDN_DIM).astype(z.dtype)


def compress(t, w, pe):
    bsz, seq, h, d = t.shape
    c = t.reshape(bsz, seq // CMP_STRIDE, CMP_STRIDE, h, d)
    blk = jnp.concatenate([c[:, :-1], c[:, 1:]], axis=2) + pe[:, None, :].astype(t.dtype)
    return jnp.einsum('bnlhd,lde->bnhe', blk, w.astype(t.dtype))


def native_sparse_attention(q, k_cmp, v_cmp, k_sel, v_sel, k_win, v_win, gate_logits, cos, sin,
                            w_ck, w_cv, pe_ck, pe_cv):
    bsz, seq, _ = q.shape
    dt = q.dtype
    nqb = seq // Q_BLOCK
    n_cmp = seq // CMP_STRIDE - 1
    n_slc = seq // SEL_BLOCK
    topk = min(SEL_TOPK, n_slc)
    scale = HEAD_DIM ** -0.5
    perm = (1, 0, 3, 4, 2, 5)
    q = apply_rope(q.reshape(bsz, seq, NSA_HEADS, HEAD_DIM), cos, sin)
    q = q.reshape(bsz, nqb, Q_BLOCK, NSA_KV_HEADS, NSA_GROUP, HEAD_DIM).transpose(perm)
    gates = jax.nn.sigmoid(gate_logits).reshape(bsz, nqb, Q_BLOCK, NSA_KV_HEADS, NSA_GROUP, 3).transpose(perm)
    kvh = lambda t: t.reshape(bsz, seq, NSA_KV_HEADS, HEAD_DIM)
    cmp_end = np.arange(n_cmp) * CMP_STRIDE + CMP_LEN - 1
    kc = apply_rope(compress(kvh(k_cmp), w_ck, pe_ck), cos[:, cmp_end], sin[:, cmp_end]).transpose(0, 2, 1, 3)
    vc = compress(kvh(v_cmp), w_cv, pe_cv).transpose(0, 2, 1, 3)
    ks = apply_rope(kvh(k_sel), cos, sin).transpose(0, 2, 1, 3).reshape(bsz, NSA_KV_HEADS, n_slc, SEL_BLOCK, HEAD_DIM)
    vs = kvh(v_sel).transpose(0, 2, 1, 3).reshape(bsz, NSA_KV_HEADS, n_slc, SEL_BLOCK, HEAD_DIM)
    pad = ((0, 0), (0, 0), (WINDOW, 0), (0, 0))
    kw = jnp.pad(apply_rope(kvh(k_win), cos, sin).transpose(0, 2, 1, 3), pad)
    vw = jnp.pad(kvh(v_win).transpose(0, 2, 1, 3), pad)
    cmp_end_j = jnp.asarray(cmp_end)
    ci = np.arange(n_cmp)[:, None] * CMP_STRIDE
    sj = np.arange(n_slc)[None, :] * SEL_BLOCK
    overlap = jnp.asarray(((ci < sj + SEL_BLOCK) & (ci + CMP_LEN > sj)).astype(np.float32))
    slc = jnp.arange(n_slc)
    bi = jnp.arange(bsz)[:, None, None, None]
    hi = jnp.arange(NSA_KV_HEADS)[None, :, None, None]
    span = WINDOW + Q_BLOCK

    def block(inp):
        qi, q_b, g_b = inp
        t = qi * Q_BLOCK + jnp.arange(Q_BLOCK)
        s = jnp.einsum('bhgqd,bhnd->bhgqn', q_b, kc) * scale
        p_c = masked_softmax(s, cmp_end_j[None, :] <= t[:, None])
        o_c = jnp.einsum('bhgqn,bhnd->bhgqd', p_c.astype(dt), vc)
        imp = jnp.einsum('bhgqn,nj->bhqj', p_c, overlap)
        cur = t // SEL_BLOCK
        forced = (slc[None, :] == 0) | (slc[None, :] == cur[:, None]) | (slc[None, :] == cur[:, None] - 1)
        imp = jnp.where(forced, FORCE, jnp.where(slc[None, :] * SEL_BLOCK <= t[:, None], imp, -1.0))
        _, idx = lax.top_k(imp, topk)
        k_g = ks[bi, hi, idx].reshape(bsz, NSA_KV_HEADS, Q_BLOCK, topk * SEL_BLOCK, HEAD_DIM)
        v_g = vs[bi, hi, idx].reshape(bsz, NSA_KV_HEADS, Q_BLOCK, topk * SEL_BLOCK, HEAD_DIM)
        kpos = (idx[..., None] * SEL_BLOCK + jnp.arange(SEL_BLOCK)).reshape(bsz, NSA_KV_HEADS, Q_BLOCK, -1)
        s = jnp.einsum('bhgqd,bhqkd->bhgqk', q_b, k_g) * scale
        p_s = masked_softmax(s, kpos[:, :, None] <= t[:, None])
        o_s = jnp.einsum('bhgqk,bhqkd->bhgqd', p_s.astype(dt), v_g)
        k_b = lax.dynamic_slice_in_dim(kw, qi * Q_BLOCK, span, axis=2)
        v_b = lax.dynamic_slice_in_dim(vw, qi * Q_BLOCK, span, axis=2)
        wpos = qi * Q_BLOCK - WINDOW + jnp.arange(span)
        diff = t[:, None] - wpos[None, :]
        s = jnp.einsum('bhgqd,bhkd->bhgqk', q_b, k_b) * scale
        p_w = masked_softmax(s, (diff >= 0) & (diff < WINDOW) & (wpos[None, :] >= 0))
        o_w = jnp.einsum('bhgqk,bhkd->bhgqd', p_w.astype(dt), v_b)
        return g_b[..., 0:1] * o_c + g_b[..., 1:2] * o_s + g_b[..., 2:3] * o_w

    out = lax.map(block, (jnp.arange(nqb), q, gates))
    return out.transpose(1, 0, 4, 2, 3, 5).reshape(bsz, seq, NSA_DIM)


def conformer_conv(u, dw_w, dw_b, ln_g, ln_b):
    a, gt = jnp.split(u, 2, axis=-1)
    h = causal_dwconv(a * jax.nn.sigmoid(gt), dw_w) + dw_b.astype(u.dtype)
    hf = h.astype(jnp.float32)
    mu = jnp.mean(hf, -1, keepdims=True)
    var = jnp.mean(jnp.square(hf - mu), -1, keepdims=True)
    hf = (hf - mu) * lax.rsqrt(var + EPS) * ln_g.astype(jnp.float32) + ln_b.astype(jnp.float32)
    return jax.nn.silu(hf).astype(u.dtype)


def conv_gated_mlp(x, w_up, conv_w, w_down):
    h = causal_dwconv(x @ w_up, conv_w)
    gt, val = jnp.split(h, 2, axis=-1)
    return (jax.nn.silu(gt) * val) @ w_down


def setup_inputs(seed: int = 0) -> dict:
    key = jax.random.key(seed)
    ks = jax.random.split(key, 26)
    f32 = jnp.float32
    nrm = lambda k, shape, sc: jax.random.normal(k, shape, f32) * sc
    gain = lambda k, n: 1.0 + 0.05 * jax.random.normal(k, (DEPTH, n), f32)
    dt = jnp.exp(jax.random.uniform(ks[10], (DEPTH, GDN_HEADS), f32, math.log(1e-3), math.log(1e-1)))
    return {
        "x": jax.random.normal(ks[0], (BATCH, SEQ, D_MODEL), f32),
        "positions": (jnp.arange(SEQ, dtype=jnp.int32)[None, :]
                      + jax.random.randint(ks[1], (BATCH, 1), 0, SEQ, dtype=jnp.int32)),
        "norm_mix_pre": gain(ks[2], D_MODEL),
        "norm_mix_post": gain(ks[3], D_MODEL),
        "norm_ffn_pre": gain(ks[4], D_MODEL),
        "norm_ffn_post": gain(ks[5], D_MODEL),
        "w_in": nrm(ks[6], (DEPTH, D_MODEL, IN_DIM), D_MODEL ** -0.5),
        "w_out": nrm(ks[7], (DEPTH, MIX_DIM, D_MODEL), MIX_DIM ** -0.5),
        "gdn_conv_w": nrm(ks[8], (DEPTH, GDN_CONV, 3 * GDN_DIM), GDN_CONV ** -0.5),
        "gdn_a_log": jnp.log(jax.random.uniform(ks[9], (DEPTH, GDN_HEADS), f32, 1.0, 16.0)),
        "gdn_dt_bias": dt + jnp.log(-jnp.expm1(-dt)),
        "gdn_norm_g": gain(ks[11], HEAD_DIM),
        "nsa_cmp_wk": nrm(ks[12], (DEPTH, CMP_LEN, HEAD_DIM, HEAD_DIM), (CMP_LEN * HEAD_DIM) ** -0.5),
        "nsa_cmp_wv": nrm(ks[13], (DEPTH, CMP_LEN, HEAD_DIM, HEAD_DIM), (CMP_LEN * HEAD_DIM) ** -0.5),
        "nsa_cmp_pe_k": nrm(ks[14], (DEPTH, CMP_LEN, HEAD_DIM), 0.02),
        "nsa_cmp_pe_v": nrm(ks[15], (DEPTH, CMP_LEN, HEAD_DIM), 0.02),
        "cc_dw_w": nrm(ks[16], (DEPTH, CONV_WIDTH, CONV_CH), CONV_WIDTH ** -0.5),
        "cc_dw_b": nrm(ks[17], (DEPTH, CONV_CH), 0.02),
        "cc_ln_g": gain(ks[18], CONV_CH),
        "cc_ln_b": nrm(ks[19], (DEPTH, CONV_CH), 0.02),
        "ffn_w_up": nrm(ks[20], (DEPTH, D_MODEL, 2 * D_FF), D_MODEL ** -0.5),
        "ffn_conv_w": nrm(ks[21], (DEPTH, FFN_CONV, 2 * D_FF), FFN_CONV ** -0.5),
        "ffn_w_down": nrm(ks[22], (DEPTH, D_FF, D_MODEL), D_FF ** -0.5),
    }


def reference(x, positions, norm_mix_pre, norm_mix_post, norm_ffn_pre, norm_ffn_post, w_in, w_out,
              gdn_conv_w, gdn_a_log, gdn_dt_bias, gdn_norm_g, nsa_cmp_wk, nsa_cmp_wv, nsa_cmp_pe_k,
              nsa_cmp_pe_v, cc_dw_w, cc_dw_b, cc_ln_g, cc_ln_b, ffn_w_up, ffn_conv_w, ffn_w_down):
    cos, sin = rope_tables(positions)
    split_idx = [int(s) for s in np.cumsum(SPLIT_SIZES)[:-1]]
    for l in range(DEPTH):
        h = rms_norm(x, norm_mix_pre[l])
        (gq, gk, gv, gz, ga, gb, nq, nkc, nvc, nks, nvs, nkw, nvw, ngate, cu) = jnp.split(h @ w_in[l], split_idx, axis=-1)
        o_a = gated_deltanet(gq, gk, gv, gz, ga, gb, gdn_conv_w[l], gdn_a_log[l], gdn_dt_bias[l], gdn_norm_g[l])
        o_b = native_sparse_attention(nq, nkc, nvc, nks, nvs, nkw, nvw, ngate, cos, sin,
                                      nsa_cmp_wk[l], nsa_cmp_wv[l], nsa_cmp_pe_k[l], nsa_cmp_pe_v[l])
        o_c = conformer_conv(cu, cc_dw_w[l], cc_dw_b[l], cc_ln_g[l], cc_ln_b[l])
        mix = jnp.concatenate([o_a, o_b, o_c], axis=-1) @ w_out[l]
        x = x + rms_norm(mix, norm_mix_post[l])
        h = rms_norm(x, norm_ffn_pre[l])
        x = x + rms_norm(conv_gated_mlp(h, ffn_w_up[l], ffn_conv_w[l], ffn_w_down[l]), norm_ffn_post[l])
    return x
```

```python
import functools

import jax
import jax.numpy as jnp
import numpy as np
from jax import lax
from jax.experimental import pallas as pl
from jax.experimental.pallas import tpu as pltpu

F32 = jnp.float32
BF16 = jnp.bfloat16
HIGHEST = lax.Precision.HIGHEST

HEAD_DIM = 64
GDN_HEADS = 6
GDN_DIM = GDN_HEADS * HEAD_DIM
GDN_CONV = 4
GDN_CHUNK = 64
NSA_HEADS = 6
NSA_KV_HEADS = 2
NSA_GROUP = NSA_HEADS // NSA_KV_HEADS
NSA_DIM = NSA_HEADS * HEAD_DIM
CMP_STRIDE = 16
CMP_LEN = 32
SEL_BLOCK = 64
SEL_TOPK = 8
WINDOW = 512
Q_BLOCK = 128
CONV_CH = 256
CONV_WIDTH = 31
ROPE_THETA = 10000.0
EPS = 1e-6
NEG = -1e30
FORCE = 1e4

LANE = 128
SUBLANE = 8
VMEM_LIMIT = 56 * 1024 * 1024

COL_QKV = 0
COL_Z = 1152
COL_NQ = 1536
COL_GAB = 1920
COL_GATE = 2048
COL_CMP = 2304
COL_SEL = 2560
COL_WIN = 2816
COL_CU = 3072
PROJ_DIM = 3584


def _cparams(sem):
    return pltpu.CompilerParams(dimension_semantics=sem, vmem_limit_bytes=VMEM_LIMIT)


def _sigmoid(x):
    return 1.0 / (1.0 + jnp.exp(-x))


def _silu(x):
    return x * _sigmoid(x)


def _dot(a, b):
    return jnp.dot(a.astype(BF16), b.astype(BF16), preferred_element_type=F32)


def _dot_nt(a, b):
    return lax.dot_general(a.astype(BF16), b.astype(BF16), (((1,), (1,)), ((), ())),
                           preferred_element_type=F32)


def _dot_hi(a, b):
    return jnp.dot(a, b, preferred_element_type=F32, precision=HIGHEST)


def _dot_nt_hi(a, b):
    return lax.dot_general(a, b, (((1,), (1,)), ((), ())), preferred_element_type=F32,
                           precision=HIGHEST)


def _dot_tn_hi(a, b):
    return lax.dot_general(a, b, (((0,), (0,)), ((), ())), preferred_element_type=F32,
                           precision=HIGHEST)


def _rms(x, g):
    return x * lax.rsqrt(jnp.mean(x * x, axis=-1, keepdims=True) + EPS) * g


def _rope_table_kernel(pos_ref, inv_ref, sign_ref, cos_ref, sin_ref):
    ang = pos_ref[0].astype(F32) * inv_ref[...]
    cos_ref[0] = jnp.cos(ang)
    sin_ref[0] = jnp.sin(ang) * sign_ref[...]


def _rope_tables(positions):
    bsz, seq = positions.shape
    t = min(seq, 512)
    inv = 1.0 / (ROPE_THETA ** (jnp.arange(0, HEAD_DIM, 2, dtype=F32) / HEAD_DIM))
    inv = jnp.tile(inv, LANE // (HEAD_DIM // 2))[None, :]
    sign = jnp.tile(jnp.concatenate([-jnp.ones(HEAD_DIM // 2, F32), jnp.ones(HEAD_DIM // 2, F32)]),
                    LANE // HEAD_DIM)[None, :]
    out = jax.ShapeDtypeStruct((bsz, seq, LANE), F32)
    return pl.pallas_call(
        _rope_table_kernel,
        grid=(bsz, seq // t),
        in_specs=[pl.BlockSpec((1, t, 1), lambda b, s: (b, s, 0)),
                  pl.BlockSpec((1, LANE), lambda b, s: (0, 0)),
                  pl.BlockSpec((1, LANE), lambda b, s: (0, 0))],
        out_specs=[pl.BlockSpec((1, t, LANE), lambda b, s: (b, s, 0))] * 2,
        out_shape=[out, out],
        compiler_params=_cparams(("parallel", "parallel")),
    )(positions[:, :, None], inv, sign)


def _rope(x, cos, sin):
    lane = lax.broadcasted_iota(jnp.int32, x.shape, 1)
    first_half = (lane % HEAD_DIM) < (HEAD_DIM // 2)
    partner = jnp.where(first_half, pltpu.roll(x, LANE - HEAD_DIM // 2, 1), pltpu.roll(x, HEAD_DIM // 2, 1))
    return x * cos + partner * sin


def _inproj_kernel(x_ref, g_ref, w_ref, o_ref, xn_ref):
    @pl.when(pl.program_id(1) == 0)
    def _():
        xn_ref[...] = _rms(x_ref[...], g_ref[...]).astype(BF16)

    o_ref[...] = jnp.dot(xn_ref[...], w_ref[...], preferred_element_type=F32)


def _inproj(x2d, g, w):
    m, d = x2d.shape
    n = w.shape[1]
    tm = min(m, 512)
    tn = n // 2
    return pl.pallas_call(
        _inproj_kernel,
        grid=(m // tm, n // tn),
        in_specs=[pl.BlockSpec((tm, d), lambda i, j: (i, 0)),
                  pl.BlockSpec((1, d), lambda i, j: (0, 0)),
                  pl.BlockSpec((d, tn), lambda i, j: (0, j))],
        out_specs=pl.BlockSpec((tm, tn), lambda i, j: (i, j)),
        out_shape=jax.ShapeDtypeStruct((m, n), F32),
        scratch_shapes=[pltpu.VMEM((tm, d), BF16)],
        compiler_params=_cparams(("parallel", "arbitrary")),
    )(x2d, g, w)


def _gdn_kernel(qkv_ref, z_ref, gab_ref, cw_ref, hp_ref, ng_ref, o_ref, xbuf, state):
    c = GDN_CHUNK
    d = HEAD_DIM
    s = pl.program_id(1)

    @pl.when(s == 0)
    def _():
        xbuf[0:SUBLANE, :] = jnp.zeros((SUBLANE, 3 * GDN_DIM), F32)
        state[...] = jnp.zeros_like(state)

    @pl.when(s != 0)
    def _():
        xbuf[0:SUBLANE, :] = xbuf[c:c + SUBLANE, :]

    xbuf[SUBLANE:SUBLANE + c, :] = qkv_ref[0]
    cw = cw_ref[...]
    y = cw[0:1] * xbuf[pl.ds(SUBLANE - 3, c), :]
    for j in range(1, GDN_CONV):
        y = y + cw[j:j + 1] * xbuf[pl.ds(SUBLANE - 3 + j, c), :]
    y = _silu(y)

    gab = gab_ref[0]
    hp = hp_ref[...]
    sp_in = gab + hp[1:2]
    softplus = jnp.maximum(sp_in, 0.0) + jnp.log(1.0 + jnp.exp(-jnp.abs(sp_in)))
    g = -jnp.exp(hp[0:1]) * softplus
    beta = _sigmoid(gab)
    row = lax.broadcasted_iota(jnp.int32, (c, c), 0)
    col = lax.broadcasted_iota(jnp.int32, (c, c), 1)
    tri = row >= col
    tri_strict = row > col
    gcum = _dot_hi(tri.astype(F32), g)
    gcum_t = gcum.T
    eg = jnp.exp(gcum)
    g_last = gcum[c - 1:c, :]
    e_last = jnp.exp(g_last)
    k_dec = jnp.exp(g_last - gcum)
    eye = (row == col).astype(F32)
    z = z_ref[0]
    ng = ng_ref[...]

    outs = []
    for h in range(GDN_HEADS):
        qh = y[:, h * d:(h + 1) * d]
        kh = y[:, GDN_DIM + h * d:GDN_DIM + (h + 1) * d]
        vh = y[:, 2 * GDN_DIM + h * d:2 * GDN_DIM + (h + 1) * d]
        qh = qh * lax.rsqrt(jnp.sum(qh * qh, -1, keepdims=True) + EPS) * (d ** -0.5)
        kh = kh * lax.rsqrt(jnp.sum(kh * kh, -1, keepdims=True) + EPS)
        gcol = gcum[:, h:h + 1]
        grow = gcum_t[h:h + 1, :]
        decay = jnp.exp(jnp.where(tri, gcol - grow, NEG))
        bcol = beta[:, GDN_HEADS + h:GDN_HEADS + h + 1]
        kk = _dot_nt_hi(kh, kh)
        a = jnp.where(tri_strict, -(bcol * kk * decay), 0.0)
        tinv = eye + a
        p = a
        for _ in range(5):
            p = _dot_hi(p, p)
            tinv = tinv + _dot_hi(tinv, p)
        rhs = jnp.concatenate([vh * bcol, kh * (bcol * eg[:, h:h + 1])], axis=-1)
        sol = _dot_hi(tinv, rhs)
        u = sol[:, :d]
        w = sol[:, d:]
        qk = _dot_nt_hi(qh, kh) * decay
        st = state[h]
        v_new = u - _dot_hi(w, st)
        o = _dot_hi(qh * eg[:, h:h + 1], st) + _dot_hi(qk, v_new)
        state[h] = st * e_last[:, h:h + 1] + _dot_tn_hi(kh * k_dec[:, h:h + 1], v_new)
        o = _rms(o, ng) * _silu(z[:, h * d:(h + 1) * d])
        outs.append(o)
    o_ref[0] = jnp.concatenate(outs, axis=-1)


def _gdn(proj, conv_w, head_params, norm_g):
    bsz, seq, _ = proj.shape
    c = GDN_CHUNK
    w_qkv = 3 * GDN_DIM
    return pl.pallas_call(
        _gdn_kernel,
        grid=(bsz, seq // c),
        in_specs=[pl.BlockSpec((1, c, w_qkv), lambda b, s: (b, s, COL_QKV // w_qkv)),
                  pl.BlockSpec((1, c, GDN_DIM), lambda b, s: (b, s, COL_Z // GDN_DIM)),
                  pl.BlockSpec((1, c, LANE), lambda b, s: (b, s, COL_GAB // LANE)),
                  pl.BlockSpec((GDN_CONV, w_qkv), lambda b, s: (0, 0)),
                  pl.BlockSpec((SUBLANE, LANE), lambda b, s: (0, 0)),
                  pl.BlockSpec((1, HEAD_DIM), lambda b, s: (0, 0))],
        out_specs=pl.BlockSpec((1, c, GDN_DIM), lambda b, s: (b, s, 0)),
        out_shape=jax.ShapeDtypeStruct((bsz, seq, GDN_DIM), F32),
        scratch_shapes=[pltpu.VMEM((c + SUBLANE, w_qkv), F32),
                        pltpu.VMEM((GDN_HEADS, HEAD_DIM, HEAD_DIM), F32)],
        compiler_params=_cparams(("parallel", "arbitrary")),
    )(proj, proj, proj, conv_w, head_params, norm_g)


def _cmp_kernel(c_ref, wa_ref, wb_ref, pea_ref, peb_ref, cos_ref, sin_ref, o_ref):
    cb = c_ref[0]
    ya = _dot(cb + pea_ref[...], wa_ref[...])
    yb = _dot(cb + peb_ref[...], wb_ref[...])
    n = ya.shape[0]
    y = ya + pltpu.roll(yb, n - 1, 0)
    kc = _rope(y[:, :LANE], cos_ref[0], sin_ref[0])
    o_ref[0] = jnp.concatenate([kc, y[:, LANE:]], axis=-1)


def _compress(cmp_rows, wa, wb, pea, peb, cos_c, sin_c):
    bsz, n, width = cmp_rows.shape
    return pl.pallas_call(
        _cmp_kernel,
        grid=(bsz,),
        in_specs=[pl.BlockSpec((1, n, width), lambda b: (b, 0, 0)),
                  pl.BlockSpec(wa.shape, lambda b: (0, 0)),
                  pl.BlockSpec(wb.shape, lambda b: (0, 0)),
                  pl.BlockSpec((1, width), lambda b: (0, 0)),
                  pl.BlockSpec((1, width), lambda b: (0, 0)),
                  pl.BlockSpec((1, n, LANE), lambda b: (b, 0, 0)),
                  pl.BlockSpec((1, n, LANE), lambda b: (b, 0, 0))],
        out_specs=pl.BlockSpec((1, n, 2 * LANE), lambda b: (b, 0, 0)),
        out_shape=jax.ShapeDtypeStruct((bsz, n, 2 * LANE), F32),
        compiler_params=_cparams(("parallel",)),
    )(cmp_rows, wa, wb, pea, peb, cos_c, sin_c)


def _rope_prep_kernel(q_ref, sel_ref, win_ref, cos_ref, sin_ref, qo_ref, so_ref, wo_ref):
    cos = cos_ref[0]
    sin = sin_ref[0]
    q = q_ref[0]
    scale = HEAD_DIM ** -0.5
    qo_ref[0] = jnp.concatenate(
        [_rope(q[:, i * LANE:(i + 1) * LANE], cos, sin) * scale for i in range(NSA_DIM // LANE)], axis=-1)
    sel = sel_ref[0]
    so_ref[0] = jnp.concatenate([_rope(sel[:, :LANE], cos, sin), sel[:, LANE:]], axis=-1)
    win = win_ref[0]
    wo_ref[0] = jnp.concatenate([_rope(win[:, :LANE], cos, sin), win[:, LANE:]], axis=-1)


def _rope_prep(proj, cos, sin):
    bsz, seq, _ = proj.shape
    t = min(seq, 512)
    kv = 2 * LANE
    return pl.pallas_call(
        _rope_prep_kernel,
        grid=(bsz, seq // t),
        in_specs=[pl.BlockSpec((1, t, NSA_DIM), lambda b, s: (b, s, COL_NQ // NSA_DIM)),
                  pl.BlockSpec((1, t, kv), lambda b, s: (b, s, COL_SEL // kv)),
                  pl.BlockSpec((1, t, kv), lambda b, s: (b, s, COL_WIN // kv)),
                  pl.BlockSpec((1, t, LANE), lambda b, s: (b, s, 0)),
                  pl.BlockSpec((1, t, LANE), lambda b, s: (b, s, 0))],
        out_specs=[pl.BlockSpec((1, t, NSA_DIM), lambda b, s: (b, s, 0)),
                   pl.BlockSpec((1, t, kv), lambda b, s: (b, s, 0)),
                   pl.BlockSpec((1, t, kv), lambda b, s: (b, s, 0))],
        out_shape=[jax.ShapeDtypeStruct((bsz, seq, NSA_DIM), F32),
                   jax.ShapeDtypeStruct((bsz, seq, kv), F32),
                   jax.ShapeDtypeStruct((bsz, seq, kv), F32)],
        compiler_params=_cparams(("parallel", "parallel")),
    )(proj, proj, proj, cos, sin)


def _softmax_step(s, ok, v, carry):
    m, l, acc = carry
    s = jnp.where(ok, s, NEG)
    m_new = jnp.maximum(m, jnp.max(s, axis=-1, keepdims=True))
    alpha = jnp.exp(m - m_new)
    p = jnp.where(ok, jnp.exp(s - m_new), 0.0)
    l = alpha * l + jnp.sum(p, axis=-1, keepdims=True)
    acc = alpha * acc + _dot(p, v)
    return m_new, l, acc


def _nsa_kernel(q_ref, gate_ref, kvc_ref, ksel_ref, kwin_ref, o_ref, mask_ref, *, seq):
    qb = Q_BLOCK
    d = HEAD_DIM
    grp = NSA_GROUP
    rows = grp * qb
    n_slc = seq // SEL_BLOCK
    topk = min(SEL_TOPK, n_slc)
    ncp = seq // CMP_STRIDE
    qi = pl.program_id(1)
    q = q_ref[0]
    gates = _sigmoid(gate_ref[0])
    t_col = qi * qb + lax.broadcasted_iota(jnp.int32, (qb, 1), 0)
    t3 = jnp.concatenate([t_col] * grp, axis=0)

    n_idx = lax.broadcasted_iota(jnp.int32, (1, ncp), 1)
    cmp_valid = (n_idx * CMP_STRIDE + (CMP_LEN - 1) <= t3) & (n_idx < ncp - 1)
    ci = lax.broadcasted_iota(jnp.int32, (ncp, LANE), 0) * CMP_STRIDE
    sj = lax.broadcasted_iota(jnp.int32, (ncp, LANE), 1) * SEL_BLOCK
    overlap = ((ci < sj + SEL_BLOCK) & (ci + CMP_LEN > sj) & (ci < (ncp - 1) * CMP_STRIDE)
               & (sj < seq)).astype(F32)
    blk = lax.broadcasted_iota(jnp.int32, (qb, LANE), 1)
    blk_f = blk.astype(F32)
    cur = t_col // SEL_BLOCK
    forced = (blk == 0) | (blk == cur) | (blk == cur - 1)
    expand = (lax.broadcasted_iota(jnp.int32, (LANE, seq), 1) // SEL_BLOCK
              == lax.broadcasted_iota(jnp.int32, (LANE, seq), 0)).astype(BF16)
    lane_k = lax.broadcasted_iota(jnp.int32, (1, qb), 1)

    outs = []
    for h in range(NSA_KV_HEADS):
        q3 = jnp.concatenate([q[:, (h * grp + g) * d:(h * grp + g + 1) * d] for g in range(grp)], axis=0)
        q3 = q3.astype(BF16)
        kc = kvc_ref[0, :, h * d:(h + 1) * d]
        vc = kvc_ref[0, :, LANE + h * d:LANE + (h + 1) * d]
        s = jnp.where(cmp_valid, _dot_nt(q3, kc), NEG)
        m = jnp.max(s, axis=-1, keepdims=True)
        p = jnp.where(cmp_valid, jnp.exp(s - m), 0.0)
        l = jnp.sum(p, axis=-1, keepdims=True)
        p = p / jnp.where(l > 0.0, l, 1.0)
        o_c = _dot(p, vc)
        p_sum = p[0:qb]
        for g in range(1, grp):
            p_sum = p_sum + p[g * qb:(g + 1) * qb]
        imp = _dot_hi(p_sum, overlap)
        imp = jnp.where(forced, FORCE, jnp.where(blk * SEL_BLOCK <= t_col, imp, -1.0))
        work = jnp.where(blk < n_slc, imp, -2.0)
        sel = jnp.zeros((qb, LANE), F32)
        for _ in range(topk):
            best = jnp.max(work, axis=-1, keepdims=True)
            idx = jnp.min(jnp.where(work == best, blk_f, 1e9), axis=-1, keepdims=True)
            pick = blk_f == idx
            sel = jnp.where(pick, 1.0, sel)
            work = jnp.where(pick, -3.0, work)
        mask_ref[...] = jnp.dot(sel.astype(BF16), expand, preferred_element_type=F32)

        init = (jnp.full((rows, 1), NEG, F32), jnp.zeros((rows, 1), F32), jnp.zeros((rows, d), F32))

        def sel_body(j, carry):
            off = pl.multiple_of(j * qb, qb)
            kb = ksel_ref[0, pl.ds(off, qb), h * d:(h + 1) * d]
            vb = ksel_ref[0, pl.ds(off, qb), LANE + h * d:LANE + (h + 1) * d]
            chosen = mask_ref[:, pl.ds(off, qb)]
            chosen = jnp.concatenate([chosen] * grp, axis=0)
            ok = (chosen > 0.5) & (off + lane_k <= t3)
            return _softmax_step(_dot_nt(q3, kb), ok, vb, carry)

        _, l_s, acc_s = lax.fori_loop(0, qi + 1, sel_body, init)
        o_s = acc_s / l_s

        def win_body(j, carry):
            off = pl.multiple_of(j * qb, qb)
            kb = kwin_ref[0, pl.ds(off, qb), h * d:(h + 1) * d]
            vb = kwin_ref[0, pl.ds(off, qb), LANE + h * d:LANE + (h + 1) * d]
            diff = t3 - (off + lane_k)
            ok = (diff >= 0) & (diff < WINDOW)
            return _softmax_step(_dot_nt(q3, kb), ok, vb, carry)

        _, l_w, acc_w = lax.fori_loop(jnp.maximum(qi - WINDOW // qb, 0), qi + 1, win_body, init)
        o_w = acc_w / l_w

        for g in range(grp):
            c0 = (h * grp + g) * 3
            r = slice(g * qb, (g + 1) * qb)
            outs.append(gates[:, c0:c0 + 1] * o_c[r] + gates[:, c0 + 1:c0 + 2] * o_s[r]
                        + gates[:, c0 + 2:c0 + 3] * o_w[r])
    o_ref[0] = jnp.concatenate(outs, axis=-1)


def _nsa_attention(q_r, proj, kvc, ksel, kwin):
    bsz, seq, _ = q_r.shape
    qb = Q_BLOCK
    ncp = kvc.shape[1]
    kv = 2 * LANE
    return pl.pallas_call(
        functools.partial(_nsa_kernel, seq=seq),
        grid=(bsz, seq // qb),
        in_specs=[pl.BlockSpec((1, qb, NSA_DIM), lambda b, i: (b, i, 0)),
                  pl.BlockSpec((1, qb, LANE), lambda b, i: (b, i, COL_GATE // LANE)),
                  pl.BlockSpec((1, ncp, kv), lambda b, i: (b, 0, 0)),
                  pl.BlockSpec((1, seq, kv), lambda b, i: (b, 0, 0)),
                  pl.BlockSpec((1, seq, kv), lambda b, i: (b, 0, 0))],
        out_specs=pl.BlockSpec((1, qb, NSA_DIM), lambda b, i: (b, i, 0)),
        out_shape=jax.ShapeDtypeStruct((bsz, seq, NSA_DIM), F32),
        scratch_shapes=[pltpu.VMEM((qb, seq), F32)],
        compiler_params=_cparams(("parallel", "arbitrary")),
    )(q_r, proj, kvc, ksel, kwin)


_CC_HALO = 32
_CC_ROWS = 64


def _cconv_kernel(u_ref, w_ref, b_ref, lg_ref, lb_ref, o_ref, xbuf):
    t = u_ref.shape[1]
    s = pl.program_id(1)

    @pl.when(s == 0)
    def _():
        xbuf[0:_CC_HALO, :] = jnp.zeros((_CC_HALO, CONV_CH), F32)

    @pl.when(s != 0)
    def _():
        xbuf[0:_CC_HALO, :] = xbuf[t:t + _CC_HALO, :]

    u = u_ref[0]
    xbuf[_CC_HALO:_CC_HALO + t, :] = u[:, :CONV_CH] * _sigmoid(u[:, CONV_CH:])
    w = w_ref[...]
    first = _CC_HALO - (CONV_WIDTH - 1)
    for r in range(t // _CC_ROWS):
        acc = jnp.broadcast_to(b_ref[...], (_CC_ROWS, CONV_CH))
        for j in range(CONV_WIDTH):
            acc = acc + w[j:j + 1] * xbuf[pl.ds(r * _CC_ROWS + first + j, _CC_ROWS), :]
        mu = jnp.mean(acc, axis=-1, keepdims=True)
        var = jnp.mean(jnp.square(acc - mu), axis=-1, keepdims=True)
        hn = (acc - mu) * lax.rsqrt(var + EPS) * lg_ref[...] + lb_ref[...]
        o_ref[0, r * _CC_ROWS:(r + 1) * _CC_ROWS, :] = _silu(hn)


def _cconv(proj, dw_w, dw_b, ln_g, ln_b):
    bsz, seq, _ = proj.shape
    t = min(seq, 256)
    wu = 2 * CONV_CH
    vec = pl.BlockSpec((1, CONV_CH), lambda b, s: (0, 0))
    return pl.pallas_call(
        _cconv_kernel,
        grid=(bsz, seq // t),
        in_specs=[pl.BlockSpec((1, t, wu), lambda b, s: (b, s, COL_CU // wu)),
                  pl.BlockSpec((CONV_WIDTH, CONV_CH), lambda b, s: (0, 0)),
                  vec, vec, vec],
        out_specs=pl.BlockSpec((1, t, CONV_CH), lambda b, s: (b, s, 0)),
        out_shape=jax.ShapeDtypeStruct((bsz, seq, CONV_CH), F32),
        scratch_shapes=[pltpu.VMEM((t + _CC_HALO, CONV_CH), F32)],
        compiler_params=_cparams(("parallel", "arbitrary")),
    )(proj, dw_w, dw_b, ln_g, ln_b)


def _outproj_kernel(x_ref, oa_ref, ob_ref, oc_ref, wa_ref, wb_ref, wc_ref, g_ref, o_ref):
    mix = (jnp.dot(oa_ref[...].astype(BF16), wa_ref[...], preferred_element_type=F32)
           + jnp.dot(ob_ref[...].astype(BF16), wb_ref[...], preferred_element_type=F32)
           + jnp.dot(oc_ref[...].astype(BF16), wc_ref[...], preferred_element_type=F32))
    o_ref[...] = x_ref[...] + _rms(mix, g_ref[...])


def _outproj(x2d, o_a, o_b, o_c, w_out, g):
    m, d = x2d.shape
    tm = min(m, 512)
    wa = w_out[:GDN_DIM]
    wb = w_out[GDN_DIM:GDN_DIM + NSA_DIM]
    wc = w_out[GDN_DIM + NSA_DIM:]
    row = lambda width: pl.BlockSpec((tm, width), lambda i: (i, 0))
    full = lambda arr: pl.BlockSpec(arr.shape, lambda i: (0, 0))
    return pl.pallas_call(
        _outproj_kernel,
        grid=(m // tm,),
        in_specs=[row(d), row(GDN_DIM), row(NSA_DIM), row(CONV_CH), full(wa), full(wb), full(wc), full(g)],
        out_specs=row(d),
        out_shape=jax.ShapeDtypeStruct((m, d), F32),
        compiler_params=_cparams(("parallel",)),
    )(x2d, o_a, o_b, o_c, wa, wb, wc, g)


def _ffn_kernel(x_ref, gpre_ref, wg_ref, wv_ref, cg_ref, cv_ref, wd_ref, gpost_ref, o_ref,
                xn_ref, acc_ref, hg_ref, hv_ref, carry_ref):
    i = pl.program_id(1)
    f = pl.program_id(2)
    nf = pl.num_programs(2)
    tm = x_ref.shape[1]
    hal = SUBLANE

    @pl.when(f == 0)
    def _():
        xn_ref[...] = _rms(x_ref[0], gpre_ref[...]).astype(BF16)
        acc_ref[...] = jnp.zeros_like(acc_ref)

    def conv(h_ref, slot, h, cw):
        @pl.when(i == 0)
        def _():
            h_ref[0:hal, :] = jnp.zeros((hal, h.shape[1]), F32)

        @pl.when(i != 0)
        def _():
            h_ref[0:hal, :] = carry_ref[slot, f]

        h_ref[hal:hal + tm, :] = h
        carry_ref[slot, f] = h[tm - hal:tm, :]
        return (cw[0:1] * h_ref[pl.ds(hal - 2, tm), :] + cw[1:2] * h_ref[pl.ds(hal - 1, tm), :]
                + cw[2:3] * h)

    xn = xn_ref[...]
    yg = conv(hg_ref, 0, jnp.dot(xn, wg_ref[...], preferred_element_type=F32), cg_ref[...])
    yv = conv(hv_ref, 1, jnp.dot(xn, wv_ref[...], preferred_element_type=F32), cv_ref[...])
    act = (_silu(yg) * yv).astype(BF16)
    acc_ref[...] += jnp.dot(act, wd_ref[...], preferred_element_type=F32)

    @pl.when(f == nf - 1)
    def _():
        o_ref[0] = x_ref[0] + _rms(acc_ref[...], gpost_ref[...])


def _ffn(x, g_pre, w_up, conv_w, w_down, g_post):
    bsz, seq, d = x.shape
    d_ff = w_down.shape[0]
    tm = min(seq, 512)
    tf = 256
    nf = d_ff // tf
    kw = conv_w.shape[0]
    return pl.pallas_call(
        _ffn_kernel,
        grid=(bsz, seq // tm, nf),
        in_specs=[pl.BlockSpec((1, tm, d), lambda b, i, f: (b, i, 0)),
                  pl.BlockSpec((1, d), lambda b, i, f: (0, 0)),
                  pl.BlockSpec((d, tf), lambda b, i, f: (0, f)),
                  pl.BlockSpec((d, tf), lambda b, i, f: (0, f + nf)),
                  pl.BlockSpec((kw, tf), lambda b, i, f: (0, f)),
                  pl.BlockSpec((kw, tf), lambda b, i, f: (0, f + nf)),
                  pl.BlockSpec((tf, d), lambda b, i, f: (f, 0)),
                  pl.BlockSpec((1, d), lambda b, i, f: (0, 0))],
        out_specs=pl.BlockSpec((1, tm, d), lambda b, i, f: (b, i, 0)),
        out_shape=jax.ShapeDtypeStruct((bsz, seq, d), F32),
        scratch_shapes=[pltpu.VMEM((tm, d), BF16),
                        pltpu.VMEM((tm, d), F32),
                        pltpu.VMEM((tm + SUBLANE, tf), F32),
                        pltpu.VMEM((tm + SUBLANE, tf), F32),
                        pltpu.VMEM((2, nf, SUBLANE, tf), F32)],
        compiler_params=_cparams(("parallel", "arbitrary", "arbitrary")),
    )(x, g_pre, w_up, w_up, conv_w, conv_w, w_down, g_post)


def _pack_w_in(w_in):
    depth, d, _ = w_in.shape
    sizes = (GDN_DIM,) * 4 + (GDN_HEADS,) * 2 + (NSA_DIM,) + (LANE,) * 6 + (3 * NSA_HEADS, 2 * CONV_CH)
    offs = np.concatenate([[0], np.cumsum(sizes)])
    piece = lambda k: w_in[:, :, offs[k]:offs[k + 1]]
    zeros = lambda n: jnp.zeros((depth, d, n), w_in.dtype)
    gq, gk, gv, gz, ga, gb, nq, nkc, nvc, nks, nvs, nkw, nvw, ngate, cu = [piece(k) for k in range(15)]
    cols = [gq, gk, gv, gz, nq,
            ga, gb, zeros(LANE - 2 * GDN_HEADS),
            ngate, zeros(LANE - 3 * NSA_HEADS),
            zeros(LANE),
            nkc, nvc, nks, nvs, nkw, nvw, cu]
    packed = jnp.concatenate(cols, axis=-1)
    assert packed.shape[-1] == PROJ_DIM
    return packed.astype(BF16)


def _pack_compress(wk, wv, pe_k, pe_v):
    half = CMP_LEN // 2
    d = HEAD_DIM
    big = jnp.zeros((CMP_LEN, 4, d, 4, d), F32)
    for slot, w in enumerate((wk, wk, wv, wv)):
        big = big.at[:, slot, :, slot, :].set(w)
    big = big.reshape(CMP_LEN, 4 * d, 4 * d)
    wa = big[:half].reshape(half * 4 * d, 4 * d).astype(BF16)
    wb = big[half:].reshape(half * 4 * d, 4 * d).astype(BF16)
    pe = jnp.concatenate([pe_k, pe_k, pe_v, pe_v], axis=-1)
    pea = pe[:half].reshape(1, half * 4 * d)
    peb = pe[half:].reshape(1, half * 4 * d)
    return wa, wb, pea, peb


def kernel(x, positions, norm_mix_pre, norm_mix_post, norm_ffn_pre, norm_ffn_post, w_in, w_out, gdn_conv_w, gdn_a_log, gdn_dt_bias, gdn_norm_g, nsa_cmp_wk, nsa_cmp_wv, nsa_cmp_pe_k, nsa_cmp_pe_v, cc_dw_w, cc_dw_b, cc_ln_g, cc_ln_b, ffn_w_up, ffn_conv_w, ffn_w_down):
    bsz, seq, d = x.shape
    depth = w_in.shape[0]
    assert seq % Q_BLOCK == 0 and d == GDN_DIM + NSA_DIM + CONV_CH
    m = bsz * seq

    cos, sin = _rope_tables(positions)
    ncp = seq // CMP_STRIDE
    pad_rows = lambda t: jnp.pad(t[:, CMP_LEN - 1::CMP_STRIDE], ((0, 0), (0, 1), (0, 0)))
    cos_c, sin_c = pad_rows(cos), pad_rows(sin)

    w_in_p = _pack_w_in(w_in)
    w_out_b = w_out.astype(BF16)
    w_up_b = ffn_w_up.astype(BF16)
    w_down_b = ffn_w_down.astype(BF16)
    head_params = jnp.zeros((depth, SUBLANE, LANE), F32)
    head_params = head_params.at[:, 0, :GDN_HEADS].set(gdn_a_log).at[:, 1, :GDN_HEADS].set(gdn_dt_bias)

    for l in range(depth):
        proj = _inproj(x.reshape(m, d), norm_mix_pre[l][None], w_in_p[l]).reshape(bsz, seq, PROJ_DIM)
        o_a = _gdn(proj, gdn_conv_w[l], head_params[l], gdn_norm_g[l][None])
        wa, wb, pea, peb = _pack_compress(nsa_cmp_wk[l], nsa_cmp_wv[l], nsa_cmp_pe_k[l], nsa_cmp_pe_v[l])
        cmp_rows = proj[:, :, COL_CMP:COL_CMP + 2 * LANE].reshape(bsz, ncp, CMP_STRIDE * 2 * LANE)
        kvc = _compress(cmp_rows, wa, wb, pea, peb, cos_c, sin_c)
        q_r, ksel, kwin = _rope_prep(proj, cos, sin)
        o_b = _nsa_attention(q_r, proj, kvc, ksel, kwin)
        o_c = _cconv(proj, cc_dw_w[l], cc_dw_b[l][None], cc_ln_g[l][None], cc_ln_b[l][None])
        x = _outproj(x.reshape(m, d), o_a.reshape(m, -1), o_b.reshape(m, -1), o_c.reshape(m, -1),
                     w_out_b[l], norm_mix_post[l][None]).reshape(bsz, seq, d)
        x = _ffn(x, norm_ffn_pre[l][None], w_up_b[l], ffn_conv_w[l], w_down_b[l], norm_ffn_post[l][None])
    return x
```

```python
import functools

import jax
import jax.numpy as jnp
import numpy as np
from jax import lax
from jax.experimental import pallas as pl
from jax.experimental.pallas import tpu as pltpu

F32 = jnp.float32
BF16 = jnp.bfloat16

HEAD_DIM = 64
GDN_HEADS = 6
GDN_DIM = GDN_HEADS * HEAD_DIM
GDN_CONV = 4
GDN_CHUNK = 64
NSA_HEADS = 6
NSA_KV_HEADS = 2
NSA_GROUP = NSA_HEADS // NSA_KV_HEADS
NSA_DIM = NSA_HEADS * HEAD_DIM
CMP_STRIDE = 16
CMP_LEN = 32
SEL_BLOCK = 64
SEL_TOPK = 8
WINDOW = 512
Q_BLOCK = 128
CONV_CH = 256
CONV_WIDTH = 31
ROPE_THETA = 10000.0
EPS = 1e-6
NEG = -1e30
FORCE = 1e4

LANE = 128
SUBLANE = 8
VMEM_LIMIT = 56 * 1024 * 1024

COL_QKV = 0
COL_Z = 1152
COL_NQ = 1536
COL_GAB = 1920
COL_GATE = 2048
COL_CMP = 2304
COL_SEL = 2560
COL_WIN = 2816
COL_CU = 3072
PROJ_DIM = 3584


def _cparams(sem):
    return pltpu.CompilerParams(dimension_semantics=sem, vmem_limit_bytes=VMEM_LIMIT)


def _sigmoid(x):
    return 1.0 / (1.0 + jnp.exp(-x))


def _silu(x):
    return x * _sigmoid(x)


def _dot(a, b):
    return jnp.dot(a.astype(BF16), b.astype(BF16), preferred_element_type=F32)


def _dot_nt(a, b):
    return lax.dot_general(a.astype(BF16), b.astype(BF16), (((1,), (1,)), ((), ())),
                           preferred_element_type=F32)


def _dot_split3_rhs(sel, x):
    hi = x.astype(BF16)
    r1 = x - hi.astype(F32)
    mid = r1.astype(BF16)
    lo = (r1 - mid.astype(F32)).astype(BF16)
    return (jnp.dot(sel, hi, preferred_element_type=F32) + jnp.dot(sel, mid, preferred_element_type=F32)
            + jnp.dot(sel, lo, preferred_element_type=F32))


def _dot_split(x, sel):
    hi = x.astype(BF16)
    lo = (x - hi.astype(F32)).astype(BF16)
    return (jnp.dot(hi, sel, preferred_element_type=F32) + jnp.dot(lo, sel, preferred_element_type=F32))


def _rms(x, g):
    return x * lax.rsqrt(jnp.mean(x * x, axis=-1, keepdims=True) + EPS) * g


def _rope_table_kernel(pos_ref, inv_ref, sign_ref, cos_ref, sin_ref):
    ang = pos_ref[0].astype(F32) * inv_ref[...]
    cos_ref[0] = jnp.cos(ang)
    sin_ref[0] = jnp.sin(ang) * sign_ref[...]


def _rope_tables(positions):
    bsz, seq = positions.shape
    t = min(seq, 512)
    inv = 1.0 / (ROPE_THETA ** (jnp.arange(0, HEAD_DIM, 2, dtype=F32) / HEAD_DIM))
    inv = jnp.tile(inv, LANE // (HEAD_DIM // 2))[None, :]
    sign = jnp.tile(jnp.concatenate([-jnp.ones(HEAD_DIM // 2, F32), jnp.ones(HEAD_DIM // 2, F32)]),
                    LANE // HEAD_DIM)[None, :]
    out = jax.ShapeDtypeStruct((bsz, seq, LANE), F32)
    return pl.pallas_call(
        _rope_table_kernel,
        grid=(bsz, seq // t),
        in_specs=[pl.BlockSpec((1, t, 1), lambda b, s: (b, s, 0)),
                  pl.BlockSpec((1, LANE), lambda b, s: (0, 0)),
                  pl.BlockSpec((1, LANE), lambda b, s: (0, 0))],
        out_specs=[pl.BlockSpec((1, t, LANE), lambda b, s: (b, s, 0))] * 2,
        out_shape=[out, out],
        compiler_params=_cparams(("parallel", "parallel")),
    )(positions[:, :, None], inv, sign)


def _rope(x, cos, sin):
    lane = lax.broadcasted_iota(jnp.int32, x.shape, 1)
    first_half = (lane % HEAD_DIM) < (HEAD_DIM // 2)
    partner = jnp.where(first_half, pltpu.roll(x, LANE - HEAD_DIM // 2, 1), pltpu.roll(x, HEAD_DIM // 2, 1))
    return x * cos + partner * sin


def _inproj_kernel(x_ref, g_ref, w_ref, o_ref, xn_ref):
    @pl.when(pl.program_id(1) == 0)
    def _():
        xn_ref[...] = _rms(x_ref[...], g_ref[...]).astype(BF16)

    o_ref[...] = jnp.dot(xn_ref[...], w_ref[...], preferred_element_type=F32)


def _inproj(x2d, g, w):
    m, d = x2d.shape
    n = w.shape[1]
    tm = min(m, 512)
    tn = n // 2
    return pl.pallas_call(
        _inproj_kernel,
        grid=(m // tm, n // tn),
        in_specs=[pl.BlockSpec((tm, d), lambda i, j: (i, 0)),
                  pl.BlockSpec((1, d), lambda i, j: (0, 0)),
                  pl.BlockSpec((d, tn), lambda i, j: (0, j))],
        out_specs=pl.BlockSpec((tm, tn), lambda i, j: (i, j)),
        out_shape=jax.ShapeDtypeStruct((m, n), F32),
        scratch_shapes=[pltpu.VMEM((tm, d), BF16)],
        compiler_params=_cparams(("parallel", "arbitrary")),
    )(x2d, g, w)


_GDN_PAIR = 2 * GDN_CHUNK


def _gdn_kernel(qkv_ref, z_ref, gab_ref, cw_ref, hp_ref, ng_ref, o_ref, xbuf, state):
    c = GDN_CHUNK
    d = HEAD_DIM
    pr = _GDN_PAIR
    t = qkv_ref.shape[1]
    s = pl.program_id(1)

    @pl.when(s == 0)
    def _():
        xbuf[0:SUBLANE, :] = jnp.zeros((SUBLANE, 3 * GDN_DIM), F32)
        state[...] = jnp.zeros_like(state)

    @pl.when(s != 0)
    def _():
        xbuf[0:SUBLANE, :] = xbuf[t:t + SUBLANE, :]

    xbuf[SUBLANE:SUBLANE + t, :] = qkv_ref[0]
    cw = cw_ref[...]
    y = cw[0:1] * xbuf[pl.ds(SUBLANE - 3, t), :]
    for j in range(1, GDN_CONV):
        y = y + cw[j:j + 1] * xbuf[pl.ds(SUBLANE - 3 + j, t), :]
    y = _silu(y)

    gab = gab_ref[0]
    hp = hp_ref[...]
    sp_in = gab + hp[1:2]
    softplus = jnp.maximum(sp_in, 0.0) + jnp.log(1.0 + jnp.exp(-jnp.abs(sp_in)))
    gcum = -jnp.exp(hp[0:1]) * softplus
    beta = _sigmoid(gab)
    in_chunk = lax.broadcasted_iota(jnp.int32, (t, LANE), 0) % c
    shift = 1
    while shift < c:
        gcum = gcum + jnp.where(in_chunk >= shift, pltpu.roll(gcum, shift, 0), 0.0)
        shift *= 2
    g_last = jnp.concatenate(
        [jnp.broadcast_to(gcum[(i + 1) * c - 1:(i + 1) * c, :], (c, LANE)) for i in range(t // c)], axis=0)
    eg = jnp.exp(gcum)
    k_dec = jnp.exp(g_last - gcum)

    lane_h = lax.broadcasted_iota(jnp.int32, (LANE, GDN_DIM), 0)
    col_h = lax.broadcasted_iota(jnp.int32, (LANE, GDN_DIM), 1) // d
    expand_a = (lane_h == col_h).astype(BF16)
    expand_b = (lane_h == col_h + GDN_HEADS).astype(BF16)
    same_head = (lax.broadcasted_iota(jnp.int32, (GDN_DIM, GDN_DIM), 0) // d
                 == lax.broadcasted_iota(jnp.int32, (GDN_DIM, GDN_DIM), 1) // d).astype(BF16)
    beta_e = _dot_split(beta, expand_b)
    eg_e = _dot_split(eg, expand_a)
    kdec_e = _dot_split(k_dec, expand_a)

    q = y[:, :GDN_DIM]
    k = y[:, GDN_DIM:2 * GDN_DIM]
    v = y[:, 2 * GDN_DIM:]
    q = q * (lax.rsqrt(_dot_split(q * q, same_head) + EPS) * (d ** -0.5))
    k = k * lax.rsqrt(_dot_split(k * k, same_head) + EPS)
    k16 = k.astype(BF16)
    kbeta = k * beta_e
    first = (lax.broadcasted_iota(jnp.int32, (t, GDN_DIM), 1) % LANE) < d
    q16 = [jnp.where(first, q, 0.0).astype(BF16), jnp.where(first, 0.0, q).astype(BF16)]
    kb16 = [jnp.where(first, kbeta, 0.0).astype(BF16), jnp.where(first, 0.0, kbeta).astype(BF16)]
    vb = v * beta_e
    kbe = kbeta * eg_e
    qe16 = (q * eg_e).astype(BF16)
    kd16 = (k * kdec_e).astype(BF16)

    row = lax.broadcasted_iota(jnp.int32, (pr, pr), 0)
    col = lax.broadcasted_iota(jnp.int32, (pr, pr), 1)
    same_chunk = (row // c) == (col // c)
    tri = same_chunk & (row >= col)
    tri_strict = same_chunk & (row > col)
    low = lax.broadcasted_iota(jnp.int32, (pr, LANE), 1) < d
    low_c = lax.broadcasted_iota(jnp.int32, (c, LANE), 1) < d
    n_pair = t // pr
    gcum_t = [gcum[p * pr:(p + 1) * pr].T for p in range(n_pair)]
    chains = [(h, p) for p in range(n_pair) for h in range(GDN_HEADS)]

    sol, pw, qk = {}, {}, {}
    for h, p in chains:
        r = slice(p * pr, (p + 1) * pr)
        g = slice((h // 2) * LANE, (h // 2 + 1) * LANE)
        decay = jnp.exp(jnp.where(tri, gcum[r, h:h + 1] - gcum_t[p][h:h + 1, :], NEG))
        k2 = k16[r, g]
        kk = _dot_nt(kb16[h % 2][r, g], k2)
        pw[h, p] = jnp.where(tri_strict, -(kk * decay), 0.0).astype(BF16)
        qk[h, p] = (_dot_nt(q16[h % 2][r, g], k2) * decay).astype(BF16)
        ke_sw = pltpu.roll(kbe[r, g], d, 1)
        sol[h, p] = jnp.where(low, vb[r, g], ke_sw) if h % 2 == 0 else jnp.where(low, ke_sw, vb[r, g])
    for hp_ in chains:
        sol[hp_] = sol[hp_] + _dot(pw[hp_], sol[hp_])
    for _ in range(5):
        for hp_ in chains:
            pw[hp_] = _dot(pw[hp_], pw[hp_]).astype(BF16)
        for hp_ in chains:
            sol[hp_] = sol[hp_] + _dot(pw[hp_], sol[hp_])

    st = [state[h] for h in range(GDN_HEADS)]
    zeros = jnp.zeros((c, LANE), BF16)
    v_new, q_st = {}, {}
    for i in range(t // c):
        p, ic = divmod(i, pr // c)
        rc = slice(i * c, (i + 1) * c)
        sc = slice(ic * c, (ic + 1) * c)
        for h in range(GDN_HEADS):
            g = slice((h // 2) * LANE, (h // 2 + 1) * LANE)
            s16 = st[h].astype(BF16)
            even = h % 2 == 0
            s_w = jnp.concatenate([zeros, s16] if even else [s16, zeros], axis=0)
            s_q = jnp.concatenate([s16, zeros] if even else [zeros, s16], axis=0)
            sol_c = sol[h, p][sc]
            vn = sol_c - _dot(sol_c, s_w)
            q_st[h, i] = _dot(qe16[rc, g], s_q)
            upd = lax.dot_general(kd16[rc, g], vn.astype(BF16), (((0,), (0,)), ((), ())),
                                  preferred_element_type=F32)
            upd = upd[:c] if even else upd[c:]
            el = eg[(i + 1) * c - 1:(i + 1) * c, h:h + 1]
            st[h] = jnp.where(low_c if even else ~low_c, st[h] * el + upd, 0.0)
            v_new[h, i] = vn
    for h in range(GDN_HEADS):
        state[h] = st[h]

    per = pr // c
    groups = []
    for j in range(GDN_HEADS // 2):
        halves = []
        for h in (2 * j, 2 * j + 1):
            rows_out = []
            for p in range(n_pair):
                vn_pair = jnp.concatenate([v_new[h, p * per + ic] for ic in range(per)], axis=0)
                qs_pair = jnp.concatenate([q_st[h, p * per + ic] for ic in range(per)], axis=0)
                rows_out.append(qs_pair + _dot(qk[h, p], vn_pair))
            halves.append(jnp.concatenate(rows_out, axis=0))
        low_t = lax.broadcasted_iota(jnp.int32, (t, LANE), 1) < d
        groups.append(jnp.where(low_t, halves[0], halves[1]))
    o = jnp.concatenate(groups, axis=-1)
    ms = _dot_split(o * o, same_head) * (1.0 / d)
    o_ref[0] = o * lax.rsqrt(ms + EPS) * ng_ref[...] * _silu(z_ref[0])


def _gdn(proj, conv_w, head_params, norm_g):
    bsz, seq, _ = proj.shape
    t = min(seq, 256)
    w_qkv = 3 * GDN_DIM
    return pl.pallas_call(
        _gdn_kernel,
        grid=(bsz, seq // t),
        in_specs=[pl.BlockSpec((1, t, w_qkv), lambda b, s: (b, s, COL_QKV // w_qkv)),
                  pl.BlockSpec((1, t, GDN_DIM), lambda b, s: (b, s, COL_Z // GDN_DIM)),
                  pl.BlockSpec((1, t, LANE), lambda b, s: (b, s, COL_GAB // LANE)),
                  pl.BlockSpec((GDN_CONV, w_qkv), lambda b, s: (0, 0)),
                  pl.BlockSpec((SUBLANE, LANE), lambda b, s: (0, 0)),
                  pl.BlockSpec((1, GDN_DIM), lambda b, s: (0, 0))],
        out_specs=pl.BlockSpec((1, t, GDN_DIM), lambda b, s: (b, s, 0)),
        out_shape=jax.ShapeDtypeStruct((bsz, seq, GDN_DIM), F32),
        scratch_shapes=[pltpu.VMEM((t + SUBLANE, w_qkv), F32),
                        pltpu.VMEM((GDN_HEADS, HEAD_DIM, LANE), F32)],
        compiler_params=_cparams(("parallel", "arbitrary")),
    )(proj, proj, proj, conv_w, head_params, norm_g)


def _cmp_kernel(c_ref, wa_ref, wb_ref, pea_ref, peb_ref, cos_ref, sin_ref, o_ref):
    cb = c_ref[0]
    ya = _dot(cb + pea_ref[...], wa_ref[...])
    yb = _dot(cb + peb_ref[...], wb_ref[...])
    n = ya.shape[0]
    y = ya + pltpu.roll(yb, n - 1, 0)
    kc = _rope(y[:, :LANE], cos_ref[0], sin_ref[0])
    o_ref[0] = jnp.concatenate([kc, y[:, LANE:]], axis=-1)


def _compress(cmp_rows, wa, wb, pea, peb, cos_c, sin_c):
    bsz, n, width = cmp_rows.shape
    return pl.pallas_call(
        _cmp_kernel,
        grid=(bsz,),
        in_specs=[pl.BlockSpec((1, n, width), lambda b: (b, 0, 0)),
                  pl.BlockSpec(wa.shape, lambda b: (0, 0)),
                  pl.BlockSpec(wb.shape, lambda b: (0, 0)),
                  pl.BlockSpec((1, width), lambda b: (0, 0)),
                  pl.BlockSpec((1, width), lambda b: (0, 0)),
                  pl.BlockSpec((1, n, LANE), lambda b: (b, 0, 0)),
                  pl.BlockSpec((1, n, LANE), lambda b: (b, 0, 0))],
        out_specs=pl.BlockSpec((1, n, 2 * LANE), lambda b: (b, 0, 0)),
        out_shape=jax.ShapeDtypeStruct((bsz, n, 2 * LANE), F32),
        compiler_params=_cparams(("parallel",)),
    )(cmp_rows, wa, wb, pea, peb, cos_c, sin_c)


def _rope_prep_kernel(q_ref, sel_ref, win_ref, cos_ref, sin_ref, qo_ref, so_ref, wo_ref):
    cos = cos_ref[0]
    sin = sin_ref[0]
    q = q_ref[0]
    scale = HEAD_DIM ** -0.5
    qo_ref[0] = jnp.concatenate(
        [_rope(q[:, i * LANE:(i + 1) * LANE], cos, sin) * scale for i in range(NSA_DIM // LANE)], axis=-1)
    sel = sel_ref[0]
    so_ref[0] = jnp.concatenate([_rope(sel[:, :LANE], cos, sin), sel[:, LANE:]], axis=-1)
    win = win_ref[0]
    wo_ref[0] = jnp.concatenate([_rope(win[:, :LANE], cos, sin), win[:, LANE:]], axis=-1)


def _rope_prep(proj, cos, sin):
    bsz, seq, _ = proj.shape
    t = min(seq, 512)
    kv = 2 * LANE
    return pl.pallas_call(
        _rope_prep_kernel,
        grid=(bsz, seq // t),
        in_specs=[pl.BlockSpec((1, t, NSA_DIM), lambda b, s: (b, s, COL_NQ // NSA_DIM)),
                  pl.BlockSpec((1, t, kv), lambda b, s: (b, s, COL_SEL // kv)),
                  pl.BlockSpec((1, t, kv), lambda b, s: (b, s, COL_WIN // kv)),
                  pl.BlockSpec((1, t, LANE), lambda b, s: (b, s, 0)),
                  pl.BlockSpec((1, t, LANE), lambda b, s: (b, s, 0))],
        out_specs=[pl.BlockSpec((1, t, NSA_DIM), lambda b, s: (b, s, 0)),
                   pl.BlockSpec((1, t, kv), lambda b, s: (b, s, 0)),
                   pl.BlockSpec((1, t, kv), lambda b, s: (b, s, 0))],
        out_shape=[jax.ShapeDtypeStruct((bsz, seq, NSA_DIM), F32),
                   jax.ShapeDtypeStruct((bsz, seq, kv), F32),
                   jax.ShapeDtypeStruct((bsz, seq, kv), F32)],
        compiler_params=_cparams(("parallel", "parallel")),
    )(proj, proj, proj, cos, sin)


_NSA_KEY_BLOCK = 512


def _attend(k16, v16, q2, ok, head_rows):
    s = jnp.where(ok, jnp.dot(k16, q2, preferred_element_type=F32), NEG)
    m = jnp.max(s, axis=0, keepdims=True)
    p = jnp.where(ok, jnp.exp(s - m), 0.0)
    l = jnp.sum(p, axis=0, keepdims=True)
    pv = lax.dot_general(v16, p.astype(BF16), (((0,), (0,)), ((), ())), preferred_element_type=F32)
    return m, l, pv[head_rows], p


def _nsa_kernel(q_ref, gate_ref, kvc_ref, ksel_ref, kwin_ref, o_ref, mask_ref, *, seq):
    qb = Q_BLOCK
    d = HEAD_DIM
    grp = NSA_GROUP
    cols = grp * qb
    n_slc = seq // SEL_BLOCK
    topk = min(SEL_TOPK, n_slc)
    ncp = seq // CMP_STRIDE
    qi = pl.program_id(1)
    q_t = q_ref[0].T
    gate_t = _sigmoid(gate_ref[0]).T
    t_q = qi * qb + lax.broadcasted_iota(jnp.int32, (1, qb), 1)
    t_row = jnp.concatenate([t_q] * grp, axis=1)

    n_idx = lax.broadcasted_iota(jnp.int32, (ncp, cols), 0)
    cmp_valid = (n_idx * CMP_STRIDE + (CMP_LEN - 1) <= t_row) & (n_idx < ncp - 1)
    sj = lax.broadcasted_iota(jnp.int32, (n_slc, ncp), 0) * SEL_BLOCK
    ci = lax.broadcasted_iota(jnp.int32, (n_slc, ncp), 1) * CMP_STRIDE
    overlap_t = ((ci < sj + SEL_BLOCK) & (ci + CMP_LEN > sj) & (ci < (ncp - 1) * CMP_STRIDE)).astype(BF16)
    blk = lax.broadcasted_iota(jnp.int32, (n_slc, qb), 0)
    blk_f = blk.astype(F32)
    cur = t_q // SEL_BLOCK
    forced = (blk == 0) | (blk == cur) | (blk == cur - 1)
    kb = min(seq, _NSA_KEY_BLOCK)
    span = min(seq, WINDOW + qb)
    key_row = lax.broadcasted_iota(jnp.int32, (kb, cols), 0)
    win_row = lax.broadcasted_iota(jnp.int32, (span, cols), 0)
    zeros_q = jnp.zeros((d, cols), BF16)
    kvc = kvc_ref[0]
    kc16 = kvc[:, :LANE].astype(BF16)
    vc16 = kvc[:, LANE:].astype(BF16)

    out_rows = []
    for h in range(NSA_KV_HEADS):
        head_rows = slice(h * d, (h + 1) * d)
        q_h = jnp.concatenate([q_t[(h * grp + g) * d:(h * grp + g + 1) * d, :] for g in range(grp)],
                              axis=1).astype(BF16)
        q2 = jnp.concatenate([q_h, zeros_q] if h == 0 else [zeros_q, q_h], axis=0)
        _, l, o_c, p = _attend(kc16, vc16, q2, cmp_valid, head_rows)
        inv_l = 1.0 / jnp.where(l > 0.0, l, 1.0)
        o_c = o_c * inv_l
        p = p * inv_l
        p_sum = p[:, 0:qb]
        for g in range(1, grp):
            p_sum = p_sum + p[:, g * qb:(g + 1) * qb]
        imp = _dot_split3_rhs(overlap_t, p_sum)
        work = jnp.where(forced, FORCE, jnp.where(blk * SEL_BLOCK <= t_q, imp, -1.0))
        sel = jnp.zeros((n_slc, qb), F32)
        for _ in range(topk):
            best = jnp.max(work, axis=0, keepdims=True)
            idx = jnp.min(jnp.where(work == best, blk_f, 1e9), axis=0, keepdims=True)
            pick = blk_f == idx
            sel = jnp.where(pick, 1.0, sel)
            work = jnp.where(pick, -3.0, work)
        for b in range(n_slc):
            mask_ref[b * SEL_BLOCK:(b + 1) * SEL_BLOCK, :] = jnp.broadcast_to(sel[b:b + 1, :], (SEL_BLOCK, qb))

        init = (jnp.full((1, cols), NEG, F32), jnp.zeros((1, cols), F32), jnp.zeros((d, cols), F32))

        def sel_body(j, carry):
            m, l, acc = carry
            off = pl.multiple_of(j * kb, kb)
            kv = ksel_ref[0, pl.ds(off, kb), :]
            chosen = mask_ref[pl.ds(off, kb), :]
            chosen = jnp.concatenate([chosen] * grp, axis=1)
            ok = (chosen > 0.5) & (off + key_row <= t_row)
            m_b, l_b, pv_b, _ = _attend(kv[:, :LANE].astype(BF16), kv[:, LANE:].astype(BF16), q2, ok, head_rows)
            m_new = jnp.maximum(m, m_b)
            w_old = jnp.exp(m - m_new)
            w_blk = jnp.exp(m_b - m_new)
            return m_new, w_old * l + w_blk * l_b, w_old * acc + w_blk * pv_b

        _, l_s, acc_s = lax.fori_loop(0, (qi * qb + qb + kb - 1) // kb, sel_body, init)
        o_s = acc_s * (1.0 / l_s)

        w0 = pl.multiple_of(jnp.maximum(qi * qb + qb - span, 0), qb)
        kv = kwin_ref[0, pl.ds(w0, span), :]
        diff = t_row - (w0 + win_row)
        _, l_w, o_w, _ = _attend(kv[:, :LANE].astype(BF16), kv[:, LANE:].astype(BF16), q2,
                                 (diff >= 0) & (diff < WINDOW), head_rows)
        o_w = o_w * (1.0 / l_w)

        for g in range(grp):
            r0 = (h * grp + g) * 3
            cs = slice(g * qb, (g + 1) * qb)
            out_rows.append(gate_t[r0:r0 + 1] * o_c[:, cs] + gate_t[r0 + 1:r0 + 2] * o_s[:, cs]
                            + gate_t[r0 + 2:r0 + 3] * o_w[:, cs])
    o_ref[0] = jnp.concatenate(out_rows, axis=0).T


def _nsa_attention(q_r, proj, kvc, ksel, kwin):
    bsz, seq, _ = q_r.shape
    qb = Q_BLOCK
    ncp = kvc.shape[1]
    kv = 2 * LANE
    return pl.pallas_call(
        functools.partial(_nsa_kernel, seq=seq),
        grid=(bsz, seq // qb),
        in_specs=[pl.BlockSpec((1, qb, NSA_DIM), lambda b, i: (b, i, 0)),
                  pl.BlockSpec((1, qb, LANE), lambda b, i: (b, i, COL_GATE // LANE)),
                  pl.BlockSpec((1, ncp, kv), lambda b, i: (b, 0, 0)),
                  pl.BlockSpec((1, seq, kv), lambda b, i: (b, 0, 0)),
                  pl.BlockSpec((1, seq, kv), lambda b, i: (b, 0, 0))],
        out_specs=pl.BlockSpec((1, qb, NSA_DIM), lambda b, i: (b, i, 0)),
        out_shape=jax.ShapeDtypeStruct((bsz, seq, NSA_DIM), F32),
        scratch_shapes=[pltpu.VMEM((seq, qb), F32)],
        compiler_params=_cparams(("parallel", "arbitrary")),
    )(q_r, proj, kvc, ksel, kwin)


_CC_HALO = 32
_CC_ROWS = 64


def _cconv_kernel(u_ref, w_ref, b_ref, lg_ref, lb_ref, o_ref, xbuf):
    t = u_ref.shape[1]
    s = pl.program_id(1)

    @pl.when(s == 0)
    def _():
        xbuf[0:_CC_HALO, :] = jnp.zeros((_CC_HALO, CONV_CH), F32)

    @pl.when(s != 0)
    def _():
        xbuf[0:_CC_HALO, :] = xbuf[t:t + _CC_HALO, :]

    u = u_ref[0]
    xbuf[_CC_HALO:_CC_HALO + t, :] = u[:, :CONV_CH] * _sigmoid(u[:, CONV_CH:])
    w = w_ref[...]
    first = _CC_HALO - (CONV_WIDTH - 1)
    for r in range(t // _CC_ROWS):
        acc = jnp.broadcast_to(b_ref[...], (_CC_ROWS, CONV_CH))
        for j in range(CONV_WIDTH):
            acc = acc + w[j:j + 1] * xbuf[pl.ds(r * _CC_ROWS + first + j, _CC_ROWS), :]
        mu = jnp.mean(acc, axis=-1, keepdims=True)
        var = jnp.mean(jnp.square(acc - mu), axis=-1, keepdims=True)
        hn = (acc - mu) * lax.rsqrt(var + EPS) * lg_ref[...] + lb_ref[...]
        o_ref[0, r * _CC_ROWS:(r + 1) * _CC_ROWS, :] = _silu(hn)


def _cconv(proj, dw_w, dw_b, ln_g, ln_b):
    bsz, seq, _ = proj.shape
    t = min(seq, 256)
    wu = 2 * CONV_CH
    vec = pl.BlockSpec((1, CONV_CH), lambda b, s: (0, 0))
    return pl.pallas_call(
        _cconv_kernel,
        grid=(bsz, seq // t),
        in_specs=[pl.BlockSpec((1, t, wu), lambda b, s: (b, s, COL_CU // wu)),
                  pl.BlockSpec((CONV_WIDTH, CONV_CH), lambda b, s: (0, 0)),
                  vec, vec, vec],
        out_specs=pl.BlockSpec((1, t, CONV_CH), lambda b, s: (b, s, 0)),
        out_shape=jax.ShapeDtypeStruct((bsz, seq, CONV_CH), F32),
        scratch_shapes=[pltpu.VMEM((t + _CC_HALO, CONV_CH), F32)],
        compiler_params=_cparams(("parallel", "arbitrary")),
    )(proj, dw_w, dw_b, ln_g, ln_b)


def _outproj_kernel(x_ref, oa_ref, ob_ref, oc_ref, wa_ref, wb_ref, wc_ref, g_ref, o_ref):
    mix = (jnp.dot(oa_ref[...].astype(BF16), wa_ref[...], preferred_element_type=F32)
           + jnp.dot(ob_ref[...].astype(BF16), wb_ref[...], preferred_element_type=F32)
           + jnp.dot(oc_ref[...].astype(BF16), wc_ref[...], preferred_element_type=F32))
    o_ref[...] = x_ref[...] + _rms(mix, g_ref[...])


def _outproj(x2d, o_a, o_b, o_c, w_out, g):
    m, d = x2d.shape
    tm = min(m, 512)
    wa = w_out[:GDN_DIM]
    wb = w_out[GDN_DIM:GDN_DIM + NSA_DIM]
    wc = w_out[GDN_DIM + NSA_DIM:]
    row = lambda width: pl.BlockSpec((tm, width), lambda i: (i, 0))
    full = lambda arr: pl.BlockSpec(arr.shape, lambda i: (0, 0))
    return pl.pallas_call(
        _outproj_kernel,
        grid=(m // tm,),
        in_specs=[row(d), row(GDN_DIM), row(NSA_DIM), row(CONV_CH), full(wa), full(wb), full(wc), full(g)],
        out_specs=row(d),
        out_shape=jax.ShapeDtypeStruct((m, d), F32),
        compiler_params=_cparams(("parallel",)),
    )(x2d, o_a, o_b, o_c, wa, wb, wc, g)


def _ffn_kernel(x_ref, gpre_ref, wg_ref, wv_ref, cg_ref, cv_ref, wd_ref, gpost_ref, o_ref,
                xn_ref, acc_ref, hg_ref, hv_ref, carry_ref):
    i = pl.program_id(1)
    f = pl.program_id(2)
    nf = pl.num_programs(2)
    tm = x_ref.shape[1]
    hal = SUBLANE

    @pl.when(f == 0)
    def _():
        xn_ref[...] = _rms(x_ref[0], gpre_ref[...]).astype(BF16)
        acc_ref[...] = jnp.zeros_like(acc_ref)

    def conv(h_ref, slot, h, cw):
        @pl.when(i == 0)
        def _():
            h_ref[0:hal, :] = jnp.zeros((hal, h.shape[1]), F32)

        @pl.when(i != 0)
        def _():
            h_ref[0:hal, :] = carry_ref[slot, f]

        h_ref[hal:hal + tm, :] = h
        carry_ref[slot, f] = h[tm - hal:tm, :]
        return (cw[0:1] * h_ref[pl.ds(hal - 2, tm), :] + cw[1:2] * h_ref[pl.ds(hal - 1, tm), :]
                + cw[2:3] * h)

    xn = xn_ref[...]
    yg = conv(hg_ref, 0, jnp.dot(xn, wg_ref[...], preferred_element_type=F32), cg_ref[...])
    yv = conv(hv_ref, 1, jnp.dot(xn, wv_ref[...], preferred_element_type=F32), cv_ref[...])
    act = (_silu(yg) * yv).astype(BF16)
    acc_ref[...] += jnp.dot(act, wd_ref[...], preferred_element_type=F32)

    @pl.when(f == nf - 1)
    def _():
        o_ref[0] = x_ref[0] + _rms(acc_ref[...], gpost_ref[...])


def _ffn(x, g_pre, w_up, conv_w, w_down, g_post):
    bsz, seq, d = x.shape
    d_ff = w_down.shape[0]
    tm = min(seq, 512)
    tf = 256
    nf = d_ff // tf
    kw = conv_w.shape[0]
    return pl.pallas_call(
        _ffn_kernel,
        grid=(bsz, seq // tm, nf),
        in_specs=[pl.BlockSpec((1, tm, d), lambda b, i, f: (b, i, 0)),
                  pl.BlockSpec((1, d), lambda b, i, f: (0, 0)),
                  pl.BlockSpec((d, tf), lambda b, i, f: (0, f)),
                  pl.BlockSpec((d, tf), lambda b, i, f: (0, f + nf)),
                  pl.BlockSpec((kw, tf), lambda b, i, f: (0, f)),
                  pl.BlockSpec((kw, tf), lambda b, i, f: (0, f + nf)),
                  pl.BlockSpec((tf, d), lambda b, i, f: (f, 0)),
                  pl.BlockSpec((1, d), lambda b, i, f: (0, 0))],
        out_specs=pl.BlockSpec((1, tm, d), lambda b, i, f: (b, i, 0)),
        out_shape=jax.ShapeDtypeStruct((bsz, seq, d), F32),
        scratch_shapes=[pltpu.VMEM((tm, d), BF16),
                        pltpu.VMEM((tm, d), F32),
                        pltpu.VMEM((tm + SUBLANE, tf), F32),
                        pltpu.VMEM((tm + SUBLANE, tf), F32),
                        pltpu.VMEM((2, nf, SUBLANE, tf), F32)],
        compiler_params=_cparams(("parallel", "arbitrary", "arbitrary")),
    )(x, g_pre, w_up, w_up, conv_w, conv_w, w_down, g_post)


def _pack_w_in(w_in):
    depth, d, _ = w_in.shape
    sizes = (GDN_DIM,) * 4 + (GDN_HEADS,) * 2 + (NSA_DIM,) + (LANE,) * 6 + (3 * NSA_HEADS, 2 * CONV_CH)
    offs = np.concatenate([[0], np.cumsum(sizes)])
    piece = lambda k: w_in[:, :, offs[k]:offs[k + 1]]
    zeros = lambda n: jnp.zeros((depth, d, n), w_in.dtype)
    gq, gk, gv, gz, ga, gb, nq, nkc, nvc, nks, nvs, nkw, nvw, ngate, cu = [piece(k) for k in range(15)]
    cols = [gq, gk, gv, gz, nq,
            ga, gb, zeros(LANE - 2 * GDN_HEADS),
            ngate, zeros(LANE - 3 * NSA_HEADS),
            zeros(LANE),
            nkc, nvc, nks, nvs, nkw, nvw, cu]
    packed = jnp.concatenate(cols, axis=-1)
    assert packed.shape[-1] == PROJ_DIM
    return packed.astype(BF16)


def _pack_compress(wk, wv, pe_k, pe_v):
    half = CMP_LEN // 2
    d = HEAD_DIM
    big = jnp.zeros((CMP_LEN, 4, d, 4, d), F32)
    for slot, w in enumerate((wk, wk, wv, wv)):
        big = big.at[:, slot, :, slot, :].set(w)
    big = big.reshape(CMP_LEN, 4 * d, 4 * d)
    wa = big[:half].reshape(half * 4 * d, 4 * d).astype(BF16)
    wb = big[half:].reshape(half * 4 * d, 4 * d).astype(BF16)
    pe = jnp.concatenate([pe_k, pe_k, pe_v, pe_v], axis=-1)
    pea = pe[:half].reshape(1, half * 4 * d)
    peb = pe[half:].reshape(1, half * 4 * d)
    return wa, wb, pea, peb


def kernel(x, positions, norm_mix_pre, norm_mix_post, norm_ffn_pre, norm_ffn_post, w_in, w_out, gdn_conv_w, gdn_a_log, gdn_dt_bias, gdn_norm_g, nsa_cmp_wk, nsa_cmp_wv, nsa_cmp_pe_k, nsa_cmp_pe_v, cc_dw_w, cc_dw_b, cc_ln_g, cc_ln_b, ffn_w_up, ffn_conv_w, ffn_w_down):
    bsz, seq, d = x.shape
    depth = w_in.shape[0]
    assert seq % Q_BLOCK == 0 and d == GDN_DIM + NSA_DIM + CONV_CH
    m = bsz * seq

    cos, sin = _rope_tables(positions)
    ncp = seq // CMP_STRIDE
    pad_rows = lambda t: jnp.pad(t[:, CMP_LEN - 1::CMP_STRIDE], ((0, 0), (0, 1), (0, 0)))
    cos_c, sin_c = pad_rows(cos), pad_rows(sin)

    w_in_p = _pack_w_in(w_in)
    w_out_b = w_out.astype(BF16)
    w_up_b = ffn_w_up.astype(BF16)
    w_down_b = ffn_w_down.astype(BF16)
    head_params = jnp.zeros((depth, SUBLANE, LANE), F32)
    head_params = head_params.at[:, 0, :GDN_HEADS].set(gdn_a_log).at[:, 1, :GDN_HEADS].set(gdn_dt_bias)

    for l in range(depth):
        proj = _inproj(x.reshape(m, d), norm_mix_pre[l][None], w_in_p[l]).reshape(bsz, seq, PROJ_DIM)
        o_a = _gdn(proj, gdn_conv_w[l], head_params[l], jnp.tile(gdn_norm_g[l], GDN_HEADS)[None])
        wa, wb, pea, peb = _pack_compress(nsa_cmp_wk[l], nsa_cmp_wv[l], nsa_cmp_pe_k[l], nsa_cmp_pe_v[l])
        cmp_rows = proj[:, :, COL_CMP:COL_CMP + 2 * LANE].reshape(bsz, ncp, CMP_STRIDE * 2 * LANE)
        kvc = _compress(cmp_rows, wa, wb, pea, peb, cos_c, sin_c)
        q_r, ksel, kwin = _rope_prep(proj, cos, sin)
        o_b = _nsa_attention(q_r, proj, kvc, ksel, kwin)
        o_c = _cconv(proj, cc_dw_w[l], cc_dw_b[l][None], cc_ln_g[l][None], cc_ln_b[l][None])
        x = _outproj(x.reshape(m, d), o_a.reshape(m, -1), o_b.reshape(m, -1), o_c.reshape(m, -1),
                     w_out_b[l], norm_mix_post[l][None]).reshape(bsz, seq, d)
        x = _ffn(x, norm_ffn_pre[l][None], w_up_b[l], ffn_conv_w[l], w_down_b[l], norm_ffn_post[l][None])
    return x
```

```python
import functools

import jax
import jax.numpy as jnp
import numpy as np
from jax import lax
from jax.experimental import pallas as pl
from jax.experimental.pallas import tpu as pltpu

F32 = jnp.float32
BF16 = jnp.bfloat16

HEAD_DIM = 64
GDN_HEADS = 6
GDN_DIM = GDN_HEADS * HEAD_DIM
GDN_CONV = 4
GDN_CHUNK = 64
NSA_HEADS = 6
NSA_KV_HEADS = 2
NSA_GROUP = NSA_HEADS // NSA_KV_HEADS
NSA_DIM = NSA_HEADS * HEAD_DIM
CMP_STRIDE = 16
CMP_LEN = 32
SEL_BLOCK = 64
SEL_TOPK = 8
WINDOW = 512
Q_BLOCK = 128
CONV_CH = 256
CONV_WIDTH = 31
ROPE_THETA = 10000.0
EPS = 1e-6
NEG = -1e30
FORCE = 1e4

LANE = 128
SUBLANE = 8
VMEM_LIMIT = 56 * 1024 * 1024

COL_QKV = 0
COL_Z = 1152
COL_NQ = 1536
COL_GAB = 1920
COL_GATE = 2048
COL_CMP = 2304
COL_SEL = 2560
COL_WIN = 2816
COL_CU = 3072
PROJ_DIM = 3584


def _cparams(sem):
    return pltpu.CompilerParams(dimension_semantics=sem, vmem_limit_bytes=VMEM_LIMIT)


def _sigmoid(x):
    return 1.0 / (1.0 + jnp.exp(-x))


def _silu(x):
    return x * _sigmoid(x)


def _dot(a, b):
    return jnp.dot(a.astype(BF16), b.astype(BF16), preferred_element_type=F32)


def _dot_nt(a, b):
    return lax.dot_general(a.astype(BF16), b.astype(BF16), (((1,), (1,)), ((), ())),
                           preferred_element_type=F32)


def _dot_split3_rhs(sel, x):
    hi = x.astype(BF16)
    r1 = x - hi.astype(F32)
    mid = r1.astype(BF16)
    lo = (r1 - mid.astype(F32)).astype(BF16)
    return (jnp.dot(sel, hi, preferred_element_type=F32) + jnp.dot(sel, mid, preferred_element_type=F32)
            + jnp.dot(sel, lo, preferred_element_type=F32))


def _dot_split(x, sel):
    hi = x.astype(BF16)
    lo = (x - hi.astype(F32)).astype(BF16)
    return (jnp.dot(hi, sel, preferred_element_type=F32) + jnp.dot(lo, sel, preferred_element_type=F32))


def _rms(x, g):
    return x * lax.rsqrt(jnp.mean(x * x, axis=-1, keepdims=True) + EPS) * g


def _rope_table_kernel(pos_ref, inv_ref, sign_ref, cos_ref, sin_ref):
    ang = pos_ref[0].astype(F32) * inv_ref[...]
    cos_ref[0] = jnp.cos(ang)
    sin_ref[0] = jnp.sin(ang) * sign_ref[...]


def _rope_tables(positions):
    bsz, seq = positions.shape
    t = min(seq, 512)
    inv = 1.0 / (ROPE_THETA ** (jnp.arange(0, HEAD_DIM, 2, dtype=F32) / HEAD_DIM))
    inv = jnp.tile(inv, LANE // (HEAD_DIM // 2))[None, :]
    sign = jnp.tile(jnp.concatenate([-jnp.ones(HEAD_DIM // 2, F32), jnp.ones(HEAD_DIM // 2, F32)]),
                    LANE // HEAD_DIM)[None, :]
    out = jax.ShapeDtypeStruct((bsz, seq, LANE), F32)
    return pl.pallas_call(
        _rope_table_kernel,
        grid=(bsz, seq // t),
        in_specs=[pl.BlockSpec((1, t, 1), lambda b, s: (b, s, 0)),
                  pl.BlockSpec((1, LANE), lambda b, s: (0, 0)),
                  pl.BlockSpec((1, LANE), lambda b, s: (0, 0))],
        out_specs=[pl.BlockSpec((1, t, LANE), lambda b, s: (b, s, 0))] * 2,
        out_shape=[out, out],
        compiler_params=_cparams(("parallel", "parallel")),
    )(positions[:, :, None], inv, sign)


def _rope(x, cos, sin):
    lane = lax.broadcasted_iota(jnp.int32, x.shape, 1)
    first_half = (lane % HEAD_DIM) < (HEAD_DIM // 2)
    partner = jnp.where(first_half, pltpu.roll(x, LANE - HEAD_DIM // 2, 1), pltpu.roll(x, HEAD_DIM // 2, 1))
    return x * cos + partner * sin


def _inproj_kernel(x_ref, g_ref, w_ref, o_ref):
    xn = _rms(x_ref[...], g_ref[...]).astype(BF16)
    o_ref[...] = jnp.dot(xn, w_ref[...], preferred_element_type=F32)


def _inproj(x2d, g, w):
    m, d = x2d.shape
    n = w.shape[1]
    tm = min(m, 512)
    return pl.pallas_call(
        _inproj_kernel,
        grid=(m // tm,),
        in_specs=[pl.BlockSpec((tm, d), lambda i: (i, 0)),
                  pl.BlockSpec((1, d), lambda i: (0, 0)),
                  pl.BlockSpec((d, n), lambda i: (0, 0), pipeline_mode=pl.Buffered(1))],
        out_specs=pl.BlockSpec((tm, n), lambda i: (i, 0)),
        out_shape=jax.ShapeDtypeStruct((m, n), F32),
        compiler_params=_cparams(("parallel",)),
    )(x2d, g, w)


_GDN_PAIR = 2 * GDN_CHUNK


def _gdn_kernel(qkv_ref, z_ref, gab_ref, cw_ref, hp_ref, ng_ref, o_ref, xbuf, state):
    c = GDN_CHUNK
    d = HEAD_DIM
    pr = _GDN_PAIR
    t = qkv_ref.shape[1]
    s = pl.program_id(1)

    @pl.when(s == 0)
    def _():
        xbuf[0:SUBLANE, :] = jnp.zeros((SUBLANE, 3 * GDN_DIM), F32)
        state[...] = jnp.zeros_like(state)

    @pl.when(s != 0)
    def _():
        xbuf[0:SUBLANE, :] = xbuf[t:t + SUBLANE, :]

    xbuf[SUBLANE:SUBLANE + t, :] = qkv_ref[0]
    cw = cw_ref[...]
    y = cw[0:1] * xbuf[pl.ds(SUBLANE - 3, t), :]
    for j in range(1, GDN_CONV):
        y = y + cw[j:j + 1] * xbuf[pl.ds(SUBLANE - 3 + j, t), :]
    y = _silu(y)

    gab = gab_ref[0]
    hp = hp_ref[...]
    sp_in = gab + hp[1:2]
    softplus = jnp.maximum(sp_in, 0.0) + jnp.log(1.0 + jnp.exp(-jnp.abs(sp_in)))
    gcum = -jnp.exp(hp[0:1]) * softplus
    beta = _sigmoid(gab)
    in_chunk = lax.broadcasted_iota(jnp.int32, (t, LANE), 0) % c
    shift = 1
    while shift < c:
        gcum = gcum + jnp.where(in_chunk >= shift, pltpu.roll(gcum, shift, 0), 0.0)
        shift *= 2
    g_last = jnp.concatenate(
        [jnp.broadcast_to(gcum[(i + 1) * c - 1:(i + 1) * c, :], (c, LANE)) for i in range(t // c)], axis=0)
    eg = jnp.exp(gcum)
    k_dec = jnp.exp(g_last - gcum)

    lane_h = lax.broadcasted_iota(jnp.int32, (LANE, GDN_DIM), 0)
    col_h = lax.broadcasted_iota(jnp.int32, (LANE, GDN_DIM), 1) // d
    expand_a = (lane_h == col_h).astype(BF16)
    expand_b = (lane_h == col_h + GDN_HEADS).astype(BF16)
    same_head = (lax.broadcasted_iota(jnp.int32, (GDN_DIM, GDN_DIM), 0) // d
                 == lax.broadcasted_iota(jnp.int32, (GDN_DIM, GDN_DIM), 1) // d).astype(BF16)
    beta_e = _dot_split(beta, expand_b)
    eg_e = _dot_split(eg, expand_a)
    kdec_e = _dot_split(k_dec, expand_a)

    q = y[:, :GDN_DIM]
    k = y[:, GDN_DIM:2 * GDN_DIM]
    v = y[:, 2 * GDN_DIM:]
    q = q * (lax.rsqrt(_dot_split(q * q, same_head) + EPS) * (d ** -0.5))
    k = k * lax.rsqrt(_dot_split(k * k, same_head) + EPS)
    k16 = k.astype(BF16)
    kbeta = k * beta_e
    first = (lax.broadcasted_iota(jnp.int32, (t, GDN_DIM), 1) % LANE) < d
    q16 = [jnp.where(first, q, 0.0).astype(BF16), jnp.where(first, 0.0, q).astype(BF16)]
    kb16 = [jnp.where(first, kbeta, 0.0).astype(BF16), jnp.where(first, 0.0, kbeta).astype(BF16)]
    vb = v * beta_e
    kbe = kbeta * eg_e
    qe16 = (q * eg_e).astype(BF16)
    kd16 = (k * kdec_e).astype(BF16)

    row = lax.broadcasted_iota(jnp.int32, (pr, pr), 0)
    col = lax.broadcasted_iota(jnp.int32, (pr, pr), 1)
    same_chunk = (row // c) == (col // c)
    tri = same_chunk & (row >= col)
    tri_strict = same_chunk & (row > col)
    low = lax.broadcasted_iota(jnp.int32, (pr, LANE), 1) < d
    low_c = lax.broadcasted_iota(jnp.int32, (c, LANE), 1) < d
    n_pair = t // pr
    gcum_t = [gcum[p * pr:(p + 1) * pr].T for p in range(n_pair)]
    chains = [(h, p) for p in range(n_pair) for h in range(GDN_HEADS)]

    sol, pw, qk = {}, {}, {}
    for h, p in chains:
        r = slice(p * pr, (p + 1) * pr)
        g = slice((h // 2) * LANE, (h // 2 + 1) * LANE)
        decay = jnp.exp(jnp.where(tri, gcum[r, h:h + 1] - gcum_t[p][h:h + 1, :], NEG))
        k2 = k16[r, g]
        kk = _dot_nt(kb16[h % 2][r, g], k2)
        pw[h, p] = jnp.where(tri_strict, -(kk * decay), 0.0).astype(BF16)
        qk[h, p] = (_dot_nt(q16[h % 2][r, g], k2) * decay).astype(BF16)
        ke_sw = pltpu.roll(kbe[r, g], d, 1)
        sol[h, p] = jnp.where(low, vb[r, g], ke_sw) if h % 2 == 0 else jnp.where(low, ke_sw, vb[r, g])
    for hp_ in chains:
        sol[hp_] = sol[hp_] + _dot(pw[hp_], sol[hp_])
    for _ in range(5):
        for hp_ in chains:
            pw[hp_] = _dot(pw[hp_], pw[hp_]).astype(BF16)
        for hp_ in chains:
            sol[hp_] = sol[hp_] + _dot(pw[hp_], sol[hp_])

    st = [state[h] for h in range(GDN_HEADS)]
    zeros = jnp.zeros((c, LANE), BF16)
    v_new, q_st = {}, {}
    for i in range(t // c):
        p, ic = divmod(i, pr // c)
        rc = slice(i * c, (i + 1) * c)
        sc = slice(ic * c, (ic + 1) * c)
        for h in range(GDN_HEADS):
            g = slice((h // 2) * LANE, (h // 2 + 1) * LANE)
            s16 = st[h].astype(BF16)
            even = h % 2 == 0
            s_w = jnp.concatenate([zeros, s16] if even else [s16, zeros], axis=0)
            s_q = jnp.concatenate([s16, zeros] if even else [zeros, s16], axis=0)
            sol_c = sol[h, p][sc]
            vn = sol_c - _dot(sol_c, s_w)
            q_st[h, i] = _dot(qe16[rc, g], s_q)
            upd = lax.dot_general(kd16[rc, g], vn.astype(BF16), (((0,), (0,)), ((), ())),
                                  preferred_element_type=F32)
            upd = upd[:c] if even else upd[c:]
            el = eg[(i + 1) * c - 1:(i + 1) * c, h:h + 1]
            st[h] = jnp.where(low_c if even else ~low_c, st[h] * el + upd, 0.0)
            v_new[h, i] = vn
    for h in range(GDN_HEADS):
        state[h] = st[h]

    per = pr // c
    groups = []
    for j in range(GDN_HEADS // 2):
        halves = []
        for h in (2 * j, 2 * j + 1):
            rows_out = []
            for p in range(n_pair):
                vn_pair = jnp.concatenate([v_new[h, p * per + ic] for ic in range(per)], axis=0)
                qs_pair = jnp.concatenate([q_st[h, p * per + ic] for ic in range(per)], axis=0)
                rows_out.append(qs_pair + _dot(qk[h, p], vn_pair))
            halves.append(jnp.concatenate(rows_out, axis=0))
        low_t = lax.broadcasted_iota(jnp.int32, (t, LANE), 1) < d
        groups.append(jnp.where(low_t, halves[0], halves[1]))
    o = jnp.concatenate(groups, axis=-1)
    ms = _dot_split(o * o, same_head) * (1.0 / d)
    o_ref[0] = o * lax.rsqrt(ms + EPS) * ng_ref[...] * _silu(z_ref[0])


def _gdn(proj, conv_w, head_params, norm_g):
    bsz, seq, _ = proj.shape
    t = min(seq, 256)
    w_qkv = 3 * GDN_DIM
    return pl.pallas_call(
        _gdn_kernel,
        grid=(bsz, seq // t),
        in_specs=[pl.BlockSpec((1, t, w_qkv), lambda b, s: (b, s, COL_QKV // w_qkv)),
                  pl.BlockSpec((1, t, GDN_DIM), lambda b, s: (b, s, COL_Z // GDN_DIM)),
                  pl.BlockSpec((1, t, LANE), lambda b, s: (b, s, COL_GAB // LANE)),
                  pl.BlockSpec((GDN_CONV, w_qkv), lambda b, s: (0, 0)),
                  pl.BlockSpec((SUBLANE, LANE), lambda b, s: (0, 0)),
                  pl.BlockSpec((1, GDN_DIM), lambda b, s: (0, 0))],
        out_specs=pl.BlockSpec((1, t, GDN_DIM), lambda b, s: (b, s, 0)),
        out_shape=jax.ShapeDtypeStruct((bsz, seq, GDN_DIM), F32),
        scratch_shapes=[pltpu.VMEM((t + SUBLANE, w_qkv), F32),
                        pltpu.VMEM((GDN_HEADS, HEAD_DIM, LANE), F32)],
        compiler_params=_cparams(("parallel", "arbitrary")),
    )(proj, proj, proj, conv_w, head_params, norm_g)


def _cmp_kernel(c_ref, wa_ref, wb_ref, pea_ref, peb_ref, cos_ref, sin_ref, o_ref):
    cb = c_ref[0]
    ya = _dot(cb + pea_ref[...], wa_ref[...])
    yb = _dot(cb + peb_ref[...], wb_ref[...])
    n = ya.shape[0]
    y = ya + pltpu.roll(yb, n - 1, 0)
    kc = _rope(y[:, :LANE], cos_ref[0], sin_ref[0])
    o_ref[0] = jnp.concatenate([kc, y[:, LANE:]], axis=-1)


def _compress(cmp_rows, wa, wb, pea, peb, cos_c, sin_c):
    bsz, n, width = cmp_rows.shape
    return pl.pallas_call(
        _cmp_kernel,
        grid=(bsz,),
        in_specs=[pl.BlockSpec((1, n, width), lambda b: (b, 0, 0)),
                  pl.BlockSpec(wa.shape, lambda b: (0, 0)),
                  pl.BlockSpec(wb.shape, lambda b: (0, 0)),
                  pl.BlockSpec((1, width), lambda b: (0, 0)),
                  pl.BlockSpec((1, width), lambda b: (0, 0)),
                  pl.BlockSpec((1, n, LANE), lambda b: (b, 0, 0)),
                  pl.BlockSpec((1, n, LANE), lambda b: (b, 0, 0))],
        out_specs=pl.BlockSpec((1, n, 2 * LANE), lambda b: (b, 0, 0)),
        out_shape=jax.ShapeDtypeStruct((bsz, n, 2 * LANE), F32),
        compiler_params=_cparams(("parallel",)),
    )(cmp_rows, wa, wb, pea, peb, cos_c, sin_c)


def _rope_prep_kernel(q_ref, sel_ref, win_ref, cos_ref, sin_ref, qo_ref, so_ref, wo_ref):
    cos = cos_ref[0]
    sin = sin_ref[0]
    q = q_ref[0]
    scale = HEAD_DIM ** -0.5
    qo_ref[0] = jnp.concatenate(
        [_rope(q[:, i * LANE:(i + 1) * LANE], cos, sin) * scale for i in range(NSA_DIM // LANE)], axis=-1)
    sel = sel_ref[0]
    so_ref[0] = jnp.concatenate([_rope(sel[:, :LANE], cos, sin), sel[:, LANE:]], axis=-1)
    win = win_ref[0]
    wo_ref[0] = jnp.concatenate([_rope(win[:, :LANE], cos, sin), win[:, LANE:]], axis=-1)


def _rope_prep(proj, cos, sin):
    bsz, seq, _ = proj.shape
    t = min(seq, 512)
    kv = 2 * LANE
    return pl.pallas_call(
        _rope_prep_kernel,
        grid=(bsz, seq // t),
        in_specs=[pl.BlockSpec((1, t, NSA_DIM), lambda b, s: (b, s, COL_NQ // NSA_DIM)),
                  pl.BlockSpec((1, t, kv), lambda b, s: (b, s, COL_SEL // kv)),
                  pl.BlockSpec((1, t, kv), lambda b, s: (b, s, COL_WIN // kv)),
                  pl.BlockSpec((1, t, LANE), lambda b, s: (b, s, 0)),
                  pl.BlockSpec((1, t, LANE), lambda b, s: (b, s, 0))],
        out_specs=[pl.BlockSpec((1, t, NSA_DIM), lambda b, s: (b, s, 0)),
                   pl.BlockSpec((1, t, kv), lambda b, s: (b, s, 0)),
                   pl.BlockSpec((1, t, kv), lambda b, s: (b, s, 0))],
        out_shape=[jax.ShapeDtypeStruct((bsz, seq, NSA_DIM), F32),
                   jax.ShapeDtypeStruct((bsz, seq, kv), F32),
                   jax.ShapeDtypeStruct((bsz, seq, kv), F32)],
        compiler_params=_cparams(("parallel", "parallel")),
    )(proj, proj, proj, cos, sin)


_NSA_KEY_BLOCK = 512


def _attend(k16, v16, q2, bias):
    d = HEAD_DIM
    half = q2.shape[1] // NSA_KV_HEADS
    s = jnp.dot(k16, q2, preferred_element_type=F32) + bias
    m = jnp.max(s, axis=0, keepdims=True)
    p = jnp.exp(s - m)
    l = jnp.sum(p, axis=0, keepdims=True)
    pv = lax.dot_general(v16, p.astype(BF16), (((0,), (0,)), ((), ())), preferred_element_type=F32)
    return m, l, jnp.concatenate([pv[:d, :half], pv[d:, half:]], axis=1), p


def _nsa_kernel(q_ref, gate_ref, kvc_ref, ksel_ref, kwin_ref, o_ref, mask_ref, *, seq):
    qb = Q_BLOCK
    d = HEAD_DIM
    grp = NSA_GROUP
    cols = grp * qb
    n_slc = seq // SEL_BLOCK
    topk = min(SEL_TOPK, n_slc)
    ncp = seq // CMP_STRIDE
    qi = pl.program_id(1)
    q_t = q_ref[0].T
    gate_t = _sigmoid(gate_ref[0]).T
    t_q = qi * qb + lax.broadcasted_iota(jnp.int32, (1, qb), 1)

    heads = NSA_KV_HEADS * grp
    tile_all = lambda a: jnp.concatenate([a] * heads, axis=1)
    n_idx = lax.broadcasted_iota(jnp.int32, (ncp, qb), 0)
    cmp_bias = tile_all(jnp.where((n_idx * CMP_STRIDE + (CMP_LEN - 1) <= t_q) & (n_idx < ncp - 1), 0.0, NEG))
    sj = lax.broadcasted_iota(jnp.int32, (n_slc, ncp), 0) * SEL_BLOCK
    ci = lax.broadcasted_iota(jnp.int32, (n_slc, ncp), 1) * CMP_STRIDE
    overlap_t = ((ci < sj + SEL_BLOCK) & (ci + CMP_LEN > sj) & (ci < (ncp - 1) * CMP_STRIDE)).astype(BF16)
    blk = lax.broadcasted_iota(jnp.int32, (n_slc, qb), 0)
    blk_f = blk.astype(F32)
    cur = t_q // SEL_BLOCK
    forced = (blk == 0) | (blk == cur) | (blk == cur - 1)
    kb = min(seq, _NSA_KEY_BLOCK)
    span = min(seq, WINDOW + qb)
    blk_row = lax.broadcasted_iota(jnp.int32, (SEL_BLOCK, qb), 0)
    w0 = pl.multiple_of(jnp.maximum(qi * qb + qb - span, 0), qb)
    diff = t_q - (w0 + lax.broadcasted_iota(jnp.int32, (span, qb), 0))
    win_bias = tile_all(jnp.where((diff >= 0) & (diff < WINDOW), 0.0, NEG))
    kvc = kvc_ref[0]

    zeros_q = jnp.zeros((d, cols), BF16)
    q_h = [jnp.concatenate([q_t[(h * grp + g) * d:(h * grp + g + 1) * d, :] for g in range(grp)],
                           axis=1).astype(BF16) for h in range(NSA_KV_HEADS)]
    q2 = jnp.concatenate([jnp.concatenate([q_h[0], zeros_q], axis=0),
                          jnp.concatenate([zeros_q, q_h[1]], axis=0)], axis=1)

    m_c, l_c, o_c, p_c = _attend(kvc[:, :LANE].astype(BF16), kvc[:, LANE:].astype(BF16), q2, cmp_bias)
    inv_l = jnp.where(m_c > 0.5 * NEG, 1.0 / l_c, 0.0)
    o_c = o_c * inv_l
    p_c = p_c * inv_l

    for h in range(NSA_KV_HEADS):
        p_sum = p_c[:, h * cols:h * cols + qb]
        for g in range(1, grp):
            p_sum = p_sum + p_c[:, h * cols + g * qb:h * cols + (g + 1) * qb]
        imp = _dot_split3_rhs(overlap_t, p_sum)
        work = jnp.where(forced, FORCE, jnp.where(blk * SEL_BLOCK <= t_q, imp, -1.0))
        sel = jnp.zeros((n_slc, qb), F32)
        for _ in range(topk):
            best = jnp.max(work, axis=0, keepdims=True)
            idx = jnp.min(jnp.where(work == best, blk_f, 1e9), axis=0, keepdims=True)
            pick = blk_f == idx
            sel = jnp.where(pick, 1.0, sel)
            work = jnp.where(pick, -3.0, work)
        for b in range(n_slc):
            mask_ref[b * SEL_BLOCK:(b + 1) * SEL_BLOCK, h * qb:(h + 1) * qb] = jnp.where(
                (sel[b:b + 1, :] > 0.5) & (b * SEL_BLOCK + blk_row <= t_q), 0.0, NEG)

    def sel_body(j, carry):
        m, l, acc = carry
        off = pl.multiple_of(j * kb, kb)
        kv = ksel_ref[0, pl.ds(off, kb), :]
        mask = mask_ref[pl.ds(off, kb), :]
        bias = jnp.concatenate([mask[:, :qb]] * grp + [mask[:, qb:]] * grp, axis=1)
        m_b, l_b, pv_b, _ = _attend(kv[:, :LANE].astype(BF16), kv[:, LANE:].astype(BF16), q2, bias)
        m_new = jnp.maximum(m, m_b)
        w_old = jnp.exp(m - m_new)
        w_blk = jnp.exp(m_b - m_new)
        return m_new, w_old * l + w_blk * l_b, w_old * acc + w_blk * pv_b

    init = (jnp.full((1, 2 * cols), NEG, F32), jnp.zeros((1, 2 * cols), F32), jnp.zeros((d, 2 * cols), F32))
    _, l_s, acc_s = lax.fori_loop(0, (qi * qb + qb + kb - 1) // kb, sel_body, init)
    o_s = acc_s * (1.0 / l_s)

    kv = kwin_ref[0, pl.ds(w0, span), :]
    _, l_w, o_w, _ = _attend(kv[:, :LANE].astype(BF16), kv[:, LANE:].astype(BF16), q2, win_bias)
    o_w = o_w * (1.0 / l_w)

    out_rows = []
    for hd in range(heads):
        cs = slice(hd * qb, (hd + 1) * qb)
        out_rows.append(gate_t[3 * hd:3 * hd + 1] * o_c[:, cs] + gate_t[3 * hd + 1:3 * hd + 2] * o_s[:, cs]
                        + gate_t[3 * hd + 2:3 * hd + 3] * o_w[:, cs])
    o_ref[0] = jnp.concatenate(out_rows, axis=0).T


def _nsa_attention(q_r, proj, kvc, ksel, kwin):
    bsz, seq, _ = q_r.shape
    qb = Q_BLOCK
    ncp = kvc.shape[1]
    kv = 2 * LANE
    return pl.pallas_call(
        functools.partial(_nsa_kernel, seq=seq),
        grid=(bsz, seq // qb),
        in_specs=[pl.BlockSpec((1, qb, NSA_DIM), lambda b, i: (b, i, 0)),
                  pl.BlockSpec((1, qb, LANE), lambda b, i: (b, i, COL_GATE // LANE)),
                  pl.BlockSpec((1, ncp, kv), lambda b, i: (b, 0, 0)),
                  pl.BlockSpec((1, seq, kv), lambda b, i: (b, 0, 0)),
                  pl.BlockSpec((1, seq, kv), lambda b, i: (b, 0, 0))],
        out_specs=pl.BlockSpec((1, qb, NSA_DIM), lambda b, i: (b, i, 0)),
        out_shape=jax.ShapeDtypeStruct((bsz, seq, NSA_DIM), F32),
        scratch_shapes=[pltpu.VMEM((seq, NSA_KV_HEADS * qb), F32)],
        compiler_params=_cparams(("parallel", "arbitrary")),
    )(q_r, proj, kvc, ksel, kwin)


_CC_HALO = 32
_CC_ROWS = 64


def _cconv_kernel(u_ref, w_ref, b_ref, lg_ref, lb_ref, o_ref, xbuf):
    t = u_ref.shape[1]
    s = pl.program_id(1)

    @pl.when(s == 0)
    def _():
        xbuf[0:_CC_HALO, :] = jnp.zeros((_CC_HALO, CONV_CH), F32)

    @pl.when(s != 0)
    def _():
        xbuf[0:_CC_HALO, :] = xbuf[t:t + _CC_HALO, :]

    u = u_ref[0]
    xbuf[_CC_HALO:_CC_HALO + t, :] = u[:, :CONV_CH] * _sigmoid(u[:, CONV_CH:])
    w = w_ref[...]
    first = _CC_HALO - (CONV_WIDTH - 1)
    for r in range(t // _CC_ROWS):
        acc = jnp.broadcast_to(b_ref[...], (_CC_ROWS, CONV_CH))
        for j in range(CONV_WIDTH):
            acc = acc + w[j:j + 1] * xbuf[pl.ds(r * _CC_ROWS + first + j, _CC_ROWS), :]
        mu = jnp.mean(acc, axis=-1, keepdims=True)
        var = jnp.mean(jnp.square(acc - mu), axis=-1, keepdims=True)
        hn = (acc - mu) * lax.rsqrt(var + EPS) * lg_ref[...] + lb_ref[...]
        o_ref[0, r * _CC_ROWS:(r + 1) * _CC_ROWS, :] = _silu(hn)


def _cconv(proj, dw_w, dw_b, ln_g, ln_b):
    bsz, seq, _ = proj.shape
    t = min(seq, 256)
    wu = 2 * CONV_CH
    vec = pl.BlockSpec((1, CONV_CH), lambda b, s: (0, 0))
    return pl.pallas_call(
        _cconv_kernel,
        grid=(bsz, seq // t),
        in_specs=[pl.BlockSpec((1, t, wu), lambda b, s: (b, s, COL_CU // wu)),
                  pl.BlockSpec((CONV_WIDTH, CONV_CH), lambda b, s: (0, 0)),
                  vec, vec, vec],
        out_specs=pl.BlockSpec((1, t, CONV_CH), lambda b, s: (b, s, 0)),
        out_shape=jax.ShapeDtypeStruct((bsz, seq, CONV_CH), F32),
        scratch_shapes=[pltpu.VMEM((t + _CC_HALO, CONV_CH), F32)],
        compiler_params=_cparams(("parallel", "arbitrary")),
    )(proj, dw_w, dw_b, ln_g, ln_b)


def _outproj_kernel(x_ref, oa_ref, ob_ref, oc_ref, wa_ref, wb_ref, wc_ref, g_ref, o_ref):
    mix = (jnp.dot(oa_ref[...].astype(BF16), wa_ref[...], preferred_element_type=F32)
           + jnp.dot(ob_ref[...].astype(BF16), wb_ref[...], preferred_element_type=F32)
           + jnp.dot(oc_ref[...].astype(BF16), wc_ref[...], preferred_element_type=F32))
    o_ref[...] = x_ref[...] + _rms(mix, g_ref[...])


def _outproj(x2d, o_a, o_b, o_c, w_out, g):
    m, d = x2d.shape
    tm = min(m, 512)
    wa = w_out[:GDN_DIM]
    wb = w_out[GDN_DIM:GDN_DIM + NSA_DIM]
    wc = w_out[GDN_DIM + NSA_DIM:]
    row = lambda width: pl.BlockSpec((tm, width), lambda i: (i, 0))
    full = lambda arr: pl.BlockSpec(arr.shape, lambda i: (0, 0))
    return pl.pallas_call(
        _outproj_kernel,
        grid=(m // tm,),
        in_specs=[row(d), row(GDN_DIM), row(NSA_DIM), row(CONV_CH), full(wa), full(wb), full(wc), full(g)],
        out_specs=row(d),
        out_shape=jax.ShapeDtypeStruct((m, d), F32),
        compiler_params=_cparams(("parallel",)),
    )(x2d, o_a, o_b, o_c, wa, wb, wc, g)


_FFN_TILE = 256


def _ffn_kernel(x_ref, gpre_ref, wup_ref, cw_ref, wd_ref, gpost_ref, o_ref,
                xn_ref, act_ref, stage_ref, carry_ref):
    tm = x_ref.shape[1]
    d_ff = wd_ref.shape[0]
    tf = _FFN_TILE
    hal = SUBLANE

    @pl.when(pl.program_id(1) == 0)
    def _():
        carry_ref[...] = jnp.zeros_like(carry_ref)

    x = x_ref[0]
    xn_ref[...] = _rms(x, gpre_ref[...]).astype(BF16)
    for f in range(d_ff // tf):
        ys = []
        for part in range(2):
            cols = slice(part * d_ff + f * tf, part * d_ff + (f + 1) * tf)
            h = jnp.dot(xn_ref[...], wup_ref[:, cols], preferred_element_type=F32)
            stage = stage_ref.at[f % 2, part]
            stage[0:hal, :] = carry_ref[:, cols]
            stage[hal:hal + tm, :] = h
            carry_ref[:, cols] = h[tm - hal:tm, :]
            cw = cw_ref[:, cols]
            ys.append(cw[0:1] * stage[pl.ds(hal - 2, tm), :] + cw[1:2] * stage[pl.ds(hal - 1, tm), :]
                      + cw[2:3] * h)
        act_ref[:, f * tf:(f + 1) * tf] = (_silu(ys[0]) * ys[1]).astype(BF16)
    out = jnp.dot(act_ref[...], wd_ref[...], preferred_element_type=F32)
    o_ref[0] = x + _rms(out, gpost_ref[...])


def _ffn(x, g_pre, w_up, conv_w, w_down, g_post):
    bsz, seq, d = x.shape
    d_ff = w_down.shape[0]
    tm = min(seq, 512)
    kw = conv_w.shape[0]
    resident = lambda shape: pl.BlockSpec(shape, lambda b, i: (0, 0), pipeline_mode=pl.Buffered(1))
    return pl.pallas_call(
        _ffn_kernel,
        grid=(bsz, seq // tm),
        in_specs=[pl.BlockSpec((1, tm, d), lambda b, i: (b, i, 0)),
                  pl.BlockSpec((1, d), lambda b, i: (0, 0)),
                  resident((d, 2 * d_ff)),
                  resident((kw, 2 * d_ff)),
                  resident((d_ff, d)),
                  pl.BlockSpec((1, d), lambda b, i: (0, 0))],
        out_specs=pl.BlockSpec((1, tm, d), lambda b, i: (b, i, 0)),
        out_shape=jax.ShapeDtypeStruct((bsz, seq, d), F32),
        scratch_shapes=[pltpu.VMEM((tm, d), BF16),
                        pltpu.VMEM((tm, d_ff), BF16),
                        pltpu.VMEM((2, 2, tm + SUBLANE, _FFN_TILE), F32),
                        pltpu.VMEM((SUBLANE, 2 * d_ff), F32)],
        compiler_params=_cparams(("parallel", "arbitrary")),
    )(x, g_pre, w_up, conv_w, w_down, g_post)


def _pack_w_in(w_in):
    depth, d, _ = w_in.shape
    sizes = (GDN_DIM,) * 4 + (GDN_HEADS,) * 2 + (NSA_DIM,) + (LANE,) * 6 + (3 * NSA_HEADS, 2 * CONV_CH)
    offs = np.concatenate([[0], np.cumsum(sizes)])
    piece = lambda k: w_in[:, :, offs[k]:offs[k + 1]]
    zeros = lambda n: jnp.zeros((depth, d, n), w_in.dtype)
    gq, gk, gv, gz, ga, gb, nq, nkc, nvc, nks, nvs, nkw, nvw, ngate, cu = [piece(k) for k in range(15)]
    cols = [gq, gk, gv, gz, nq,
            ga, gb, zeros(LANE - 2 * GDN_HEADS),
            ngate, zeros(LANE - 3 * NSA_HEADS),
            zeros(LANE),
            nkc, nvc, nks, nvs, nkw, nvw, cu]
    packed = jnp.concatenate(cols, axis=-1)
    assert packed.shape[-1] == PROJ_DIM
    return packed.astype(BF16)


def _pack_compress(wk, wv, pe_k, pe_v):
    half = CMP_LEN // 2
    d = HEAD_DIM
    big = jnp.zeros((CMP_LEN, 4, d, 4, d), F32)
    for slot, w in enumerate((wk, wk, wv, wv)):
        big = big.at[:, slot, :, slot, :].set(w)
    big = big.reshape(CMP_LEN, 4 * d, 4 * d)
    wa = big[:half].reshape(half * 4 * d, 4 * d).astype(BF16)
    wb = big[half:].reshape(half * 4 * d, 4 * d).astype(BF16)
    pe = jnp.concatenate([pe_k, pe_k, pe_v, pe_v], axis=-1)
    pea = pe[:half].reshape(1, half * 4 * d)
    peb = pe[half:].reshape(1, half * 4 * d)
    return wa, wb, pea, peb


def kernel(x, positions, norm_mix_pre, norm_mix_post, norm_ffn_pre, norm_ffn_post, w_in, w_out, gdn_conv_w, gdn_a_log, gdn_dt_bias, gdn_norm_g, nsa_cmp_wk, nsa_cmp_wv, nsa_cmp_pe_k, nsa_cmp_pe_v, cc_dw_w, cc_dw_b, cc_ln_g, cc_ln_b, ffn_w_up, ffn_conv_w, ffn_w_down):
    bsz, seq, d = x.shape
    depth = w_in.shape[0]
    assert seq % Q_BLOCK == 0 and d == GDN_DIM + NSA_DIM + CONV_CH
    m = bsz * seq

    cos, sin = _rope_tables(positions)
    ncp = seq // CMP_STRIDE
    pad_rows = lambda t: jnp.pad(t[:, CMP_LEN - 1::CMP_STRIDE], ((0, 0), (0, 1), (0, 0)))
    cos_c, sin_c = pad_rows(cos), pad_rows(sin)

    w_in_p = _pack_w_in(w_in)
    w_out_b = w_out.astype(BF16)
    w_up_b = ffn_w_up.astype(BF16)
    w_down_b = ffn_w_down.astype(BF16)
    head_params = jnp.zeros((depth, SUBLANE, LANE), F32)
    head_params = head_params.at[:, 0, :GDN_HEADS].set(gdn_a_log).at[:, 1, :GDN_HEADS].set(gdn_dt_bias)

    for l in range(depth):
        proj = _inproj(x.reshape(m, d), norm_mix_pre[l][None], w_in_p[l]).reshape(bsz, seq, PROJ_DIM)
        o_a = _gdn(proj, gdn_conv_w[l], head_params[l], jnp.tile(gdn_norm_g[l], GDN_HEADS)[None])
        wa, wb, pea, peb = _pack_compress(nsa_cmp_wk[l], nsa_cmp_wv[l], nsa_cmp_pe_k[l], nsa_cmp_pe_v[l])
        cmp_rows = proj[:, :, COL_CMP:COL_CMP + 2 * LANE].reshape(bsz, ncp, CMP_STRIDE * 2 * LANE)
        kvc = _compress(cmp_rows, wa, wb, pea, peb, cos_c, sin_c)
        q_r, ksel, kwin = _rope_prep(proj, cos, sin)
        o_b = _nsa_attention(q_r, proj, kvc, ksel, kwin)
        o_c = _cconv(proj, cc_dw_w[l], cc_dw_b[l][None], cc_ln_g[l][None], cc_ln_b[l][None])
        x = _outproj(x.reshape(m, d), o_a.reshape(m, -1), o_b.reshape(m, -1), o_c.reshape(m, -1),
                     w_out_b[l], norm_mix_post[l][None]).reshape(bsz, seq, d)
        x = _ffn(x, norm_ffn_pre[l][None], w_up_b[l], ffn_conv_w[l], w_down_b[l], norm_ffn_post[l][None])
    return x
```

```python
import functools

import jax
import jax.numpy as jnp
import numpy as np
from jax import lax
from jax.experimental import pallas as pl
from jax.experimental.pallas import tpu as pltpu

F32 = jnp.float32
BF16 = jnp.bfloat16

HEAD_DIM = 64
GDN_HEADS = 6
GDN_DIM = GDN_HEADS * HEAD_DIM
GDN_CONV = 4
GDN_CHUNK = 64
NSA_HEADS = 6
NSA_KV_HEADS = 2
NSA_GROUP = NSA_HEADS // NSA_KV_HEADS
NSA_DIM = NSA_HEADS * HEAD_DIM
CMP_STRIDE = 16
CMP_LEN = 32
SEL_BLOCK = 64
SEL_TOPK = 8
WINDOW = 512
Q_BLOCK = 128
CONV_CH = 256
CONV_WIDTH = 31
ROPE_THETA = 10000.0
EPS = 1e-6
NEG = -1e30
FORCE = 1e4

LANE = 128
SUBLANE = 8
VMEM_LIMIT = 56 * 1024 * 1024

COL_QKV = 0
COL_Z = 1152
COL_NQ = 1536
COL_GAB = 1920
COL_CU = 2048
COL_SEL = 2560
COL_WIN = 2816
COL_GATE = 3072
PROJ_OUT = 3200
COL_CMP = 3200
PROJ_DIM = 3456


def _cparams(sem):
    return pltpu.CompilerParams(dimension_semantics=sem, vmem_limit_bytes=VMEM_LIMIT)


def _sigmoid(x):
    return 1.0 / (1.0 + jnp.exp(-x))


def _silu(x):
    return x * _sigmoid(x)


def _dot(a, b):
    return jnp.dot(a.astype(BF16), b.astype(BF16), preferred_element_type=F32)


def _dot_nt(a, b):
    return lax.dot_general(a.astype(BF16), b.astype(BF16), (((1,), (1,)), ((), ())),
                           preferred_element_type=F32)


def _dot_split3_rhs(sel, x):
    hi = x.astype(BF16)
    r1 = x - hi.astype(F32)
    mid = r1.astype(BF16)
    lo = (r1 - mid.astype(F32)).astype(BF16)
    return (jnp.dot(sel, hi, preferred_element_type=F32) + jnp.dot(sel, mid, preferred_element_type=F32)
            + jnp.dot(sel, lo, preferred_element_type=F32))


def _dot_split(x, sel):
    hi = x.astype(BF16)
    lo = (x - hi.astype(F32)).astype(BF16)
    return (jnp.dot(hi, sel, preferred_element_type=F32) + jnp.dot(lo, sel, preferred_element_type=F32))


def _rms(x, g):
    return x * lax.rsqrt(jnp.mean(x * x, axis=-1, keepdims=True) + EPS) * g


def _rope_table_kernel(pos_ref, inv_ref, sign_ref, cos_ref, sin_ref):
    ang = pos_ref[0].astype(F32) * inv_ref[...]
    cos_ref[0] = jnp.cos(ang)
    sin_ref[0] = jnp.sin(ang) * sign_ref[...]


def _rope_tables(positions):
    bsz, seq = positions.shape
    t = min(seq, 512)
    inv = 1.0 / (ROPE_THETA ** (jnp.arange(0, HEAD_DIM, 2, dtype=F32) / HEAD_DIM))
    inv = jnp.tile(inv, LANE // (HEAD_DIM // 2))[None, :]
    sign = jnp.tile(jnp.concatenate([-jnp.ones(HEAD_DIM // 2, F32), jnp.ones(HEAD_DIM // 2, F32)]),
                    LANE // HEAD_DIM)[None, :]
    out = jax.ShapeDtypeStruct((bsz, seq, LANE), F32)
    return pl.pallas_call(
        _rope_table_kernel,
        grid=(bsz, seq // t),
        in_specs=[pl.BlockSpec((1, t, 1), lambda b, s: (b, s, 0)),
                  pl.BlockSpec((1, LANE), lambda b, s: (0, 0)),
                  pl.BlockSpec((1, LANE), lambda b, s: (0, 0))],
        out_specs=[pl.BlockSpec((1, t, LANE), lambda b, s: (b, s, 0))] * 2,
        out_shape=[out, out],
        compiler_params=_cparams(("parallel", "parallel")),
    )(positions[:, :, None], inv, sign)


def _rope(x, cos, sin):
    lane = lax.broadcasted_iota(jnp.int32, x.shape, 1)
    first_half = (lane % HEAD_DIM) < (HEAD_DIM // 2)
    partner = jnp.where(first_half, pltpu.roll(x, LANE - HEAD_DIM // 2, 1), pltpu.roll(x, HEAD_DIM // 2, 1))
    return x * cos + partner * sin


def _inproj_kernel(x_ref, g_ref, w_ref, cos_ref, sin_ref, o_ref, cmp_ref):
    xn = _rms(x_ref[...], g_ref[...]).astype(BF16)
    y = jnp.dot(xn, w_ref[...], preferred_element_type=F32)
    cos = cos_ref[...]
    sin = sin_ref[...]
    scale = HEAD_DIM ** -0.5
    o_ref[:, :COL_NQ] = y[:, :COL_NQ]
    for c in range(COL_NQ, COL_NQ + NSA_DIM, LANE):
        o_ref[:, c:c + LANE] = _rope(y[:, c:c + LANE], cos, sin) * scale
    o_ref[:, COL_GAB:COL_SEL] = y[:, COL_GAB:COL_SEL]
    for c in (COL_SEL, COL_WIN):
        o_ref[:, c:c + LANE] = _rope(y[:, c:c + LANE], cos, sin)
        o_ref[:, c + LANE:c + 2 * LANE] = y[:, c + LANE:c + 2 * LANE]
    o_ref[:, COL_GATE:PROJ_OUT] = y[:, COL_GATE:PROJ_OUT]
    cmp_ref[...] = y[:, COL_CMP:]


def _inproj(x2d, g, w, cos2d, sin2d):
    m, d = x2d.shape
    n = w.shape[1]
    tm = min(m, 512)
    row = lambda width: pl.BlockSpec((tm, width), lambda i: (i, 0))
    return pl.pallas_call(
        _inproj_kernel,
        grid=(m // tm,),
        in_specs=[row(d),
                  pl.BlockSpec((1, d), lambda i: (0, 0)),
                  pl.BlockSpec((d, n), lambda i: (0, 0), pipeline_mode=pl.Buffered(1)),
                  row(LANE), row(LANE)],
        out_specs=[row(PROJ_OUT), row(n - PROJ_OUT)],
        out_shape=[jax.ShapeDtypeStruct((m, PROJ_OUT), F32), jax.ShapeDtypeStruct((m, n - PROJ_OUT), F32)],
        compiler_params=_cparams(("parallel",)),
    )(x2d, g, w, cos2d, sin2d)


_GDN_PAIR = 2 * GDN_CHUNK


def _gdn_kernel(qkv_ref, z_ref, gab_ref, cw_ref, hp_ref, ng_ref, o_ref, xbuf, state):
    c = GDN_CHUNK
    d = HEAD_DIM
    pr = _GDN_PAIR
    t = qkv_ref.shape[1]
    s = pl.program_id(1)

    @pl.when(s == 0)
    def _():
        xbuf[0:SUBLANE, :] = jnp.zeros((SUBLANE, 3 * GDN_DIM), F32)
        state[...] = jnp.zeros_like(state)

    @pl.when(s != 0)
    def _():
        xbuf[0:SUBLANE, :] = xbuf[t:t + SUBLANE, :]

    xbuf[SUBLANE:SUBLANE + t, :] = qkv_ref[0]
    cw = cw_ref[...]
    y = cw[0:1] * xbuf[pl.ds(SUBLANE - 3, t), :]
    for j in range(1, GDN_CONV):
        y = y + cw[j:j + 1] * xbuf[pl.ds(SUBLANE - 3 + j, t), :]
    y = _silu(y)

    gab = gab_ref[0]
    hp = hp_ref[...]
    sp_in = gab + hp[1:2]
    softplus = jnp.maximum(sp_in, 0.0) + jnp.log(1.0 + jnp.exp(-jnp.abs(sp_in)))
    gcum = -jnp.exp(hp[0:1]) * softplus
    beta = _sigmoid(gab)
    in_chunk = lax.broadcasted_iota(jnp.int32, (t, LANE), 0) % c
    shift = 1
    while shift < c:
        gcum = gcum + jnp.where(in_chunk >= shift, pltpu.roll(gcum, shift, 0), 0.0)
        shift *= 2
    g_last = jnp.concatenate(
        [jnp.broadcast_to(gcum[(i + 1) * c - 1:(i + 1) * c, :], (c, LANE)) for i in range(t // c)], axis=0)
    eg = jnp.exp(gcum)
    k_dec = jnp.exp(g_last - gcum)

    lane_h = lax.broadcasted_iota(jnp.int32, (LANE, GDN_DIM), 0)
    col_h = lax.broadcasted_iota(jnp.int32, (LANE, GDN_DIM), 1) // d
    expand_a = (lane_h == col_h).astype(BF16)
    expand_b = (lane_h == col_h + GDN_HEADS).astype(BF16)
    same_head = (lax.broadcasted_iota(jnp.int32, (GDN_DIM, GDN_DIM), 0) // d
                 == lax.broadcasted_iota(jnp.int32, (GDN_DIM, GDN_DIM), 1) // d).astype(BF16)
    beta_e = _dot_split(beta, expand_b)
    eg_e = _dot_split(eg, expand_a)
    kdec_e = _dot_split(k_dec, expand_a)

    q = y[:, :GDN_DIM]
    k = y[:, GDN_DIM:2 * GDN_DIM]
    v = y[:, 2 * GDN_DIM:]
    q = q * (lax.rsqrt(_dot_split(q * q, same_head) + EPS) * (d ** -0.5))
    k = k * lax.rsqrt(_dot_split(k * k, same_head) + EPS)
    k16 = k.astype(BF16)
    kbeta = k * beta_e
    first = (lax.broadcasted_iota(jnp.int32, (t, GDN_DIM), 1) % LANE) < d
    q16 = [jnp.where(first, q, 0.0).astype(BF16), jnp.where(first, 0.0, q).astype(BF16)]
    kb16 = [jnp.where(first, kbeta, 0.0).astype(BF16), jnp.where(first, 0.0, kbeta).astype(BF16)]
    vb = v * beta_e
    kbe = kbeta * eg_e
    qe16 = (q * eg_e).astype(BF16)
    kd16 = (k * kdec_e).astype(BF16)

    row = lax.broadcasted_iota(jnp.int32, (pr, pr), 0)
    col = lax.broadcasted_iota(jnp.int32, (pr, pr), 1)
    same_chunk = (row // c) == (col // c)
    tri = same_chunk & (row >= col)
    tri_strict = same_chunk & (row > col)
    low = lax.broadcasted_iota(jnp.int32, (pr, LANE), 1) < d
    low_c = lax.broadcasted_iota(jnp.int32, (c, LANE), 1) < d
    n_pair = t // pr
    gcum_t = [gcum[p * pr:(p + 1) * pr].T for p in range(n_pair)]
    chains = [(h, p) for p in range(n_pair) for h in range(GDN_HEADS)]

    sol, pw, qk = {}, {}, {}
    for h, p in chains:
        r = slice(p * pr, (p + 1) * pr)
        g = slice((h // 2) * LANE, (h // 2 + 1) * LANE)
        decay = jnp.exp(jnp.where(tri, gcum[r, h:h + 1] - gcum_t[p][h:h + 1, :], NEG))
        k2 = k16[r, g]
        kk = _dot_nt(kb16[h % 2][r, g], k2)
        pw[h, p] = jnp.where(tri_strict, -(kk * decay), 0.0).astype(BF16)
        qk[h, p] = (_dot_nt(q16[h % 2][r, g], k2) * decay).astype(BF16)
        ke_sw = pltpu.roll(kbe[r, g], d, 1)
        sol[h, p] = jnp.where(low, vb[r, g], ke_sw) if h % 2 == 0 else jnp.where(low, ke_sw, vb[r, g])
    for hp_ in chains:
        sol[hp_] = sol[hp_] + _dot(pw[hp_], sol[hp_])
    for _ in range(5):
        for hp_ in chains:
            pw[hp_] = _dot(pw[hp_], pw[hp_]).astype(BF16)
        for hp_ in chains:
            sol[hp_] = sol[hp_] + _dot(pw[hp_], sol[hp_])

    st = [state[h] for h in range(GDN_HEADS)]
    zeros = jnp.zeros((c, LANE), BF16)
    v_new, q_st = {}, {}
    for i in range(t // c):
        p, ic = divmod(i, pr // c)
        rc = slice(i * c, (i + 1) * c)
        sc = slice(ic * c, (ic + 1) * c)
        for h in range(GDN_HEADS):
            g = slice((h // 2) * LANE, (h // 2 + 1) * LANE)
            s16 = st[h].astype(BF16)
            even = h % 2 == 0
            s_w = jnp.concatenate([zeros, s16] if even else [s16, zeros], axis=0)
            s_q = jnp.concatenate([s16, zeros] if even else [zeros, s16], axis=0)
            sol_c = sol[h, p][sc]
            vn = sol_c - _dot(sol_c, s_w)
            q_st[h, i] = _dot(qe16[rc, g], s_q)
            upd = lax.dot_general(kd16[rc, g], vn.astype(BF16), (((0,), (0,)), ((), ())),
                                  preferred_element_type=F32)
            upd = upd[:c] if even else upd[c:]
            el = eg[(i + 1) * c - 1:(i + 1) * c, h:h + 1]
            st[h] = jnp.where(low_c if even else ~low_c, st[h] * el + upd, 0.0)
            v_new[h, i] = vn
    for h in range(GDN_HEADS):
        state[h] = st[h]

    per = pr // c
    groups = []
    for j in range(GDN_HEADS // 2):
        halves = []
        for h in (2 * j, 2 * j + 1):
            rows_out = []
            for p in range(n_pair):
                vn_pair = jnp.concatenate([v_new[h, p * per + ic] for ic in range(per)], axis=0)
                qs_pair = jnp.concatenate([q_st[h, p * per + ic] for ic in range(per)], axis=0)
                rows_out.append(qs_pair + _dot(qk[h, p], vn_pair))
            halves.append(jnp.concatenate(rows_out, axis=0))
        low_t = lax.broadcasted_iota(jnp.int32, (t, LANE), 1) < d
        groups.append(jnp.where(low_t, halves[0], halves[1]))
    o = jnp.concatenate(groups, axis=-1)
    ms = _dot_split(o * o, same_head) * (1.0 / d)
    o_ref[0] = o * lax.rsqrt(ms + EPS) * ng_ref[...] * _silu(z_ref[0])


def _gdn(proj, conv_w, head_params, norm_g):
    bsz, seq, _ = proj.shape
    t = min(seq, 256)
    w_qkv = 3 * GDN_DIM
    return pl.pallas_call(
        _gdn_kernel,
        grid=(bsz, seq // t),
        in_specs=[pl.BlockSpec((1, t, w_qkv), lambda b, s: (b, s, COL_QKV // w_qkv)),
                  pl.BlockSpec((1, t, GDN_DIM), lambda b, s: (b, s, COL_Z // GDN_DIM)),
                  pl.BlockSpec((1, t, LANE), lambda b, s: (b, s, COL_GAB // LANE)),
                  pl.BlockSpec((GDN_CONV, w_qkv), lambda b, s: (0, 0)),
                  pl.BlockSpec((SUBLANE, LANE), lambda b, s: (0, 0)),
                  pl.BlockSpec((1, GDN_DIM), lambda b, s: (0, 0))],
        out_specs=pl.BlockSpec((1, t, GDN_DIM), lambda b, s: (b, s, 0)),
        out_shape=jax.ShapeDtypeStruct((bsz, seq, GDN_DIM), F32),
        scratch_shapes=[pltpu.VMEM((t + SUBLANE, w_qkv), F32),
                        pltpu.VMEM((GDN_HEADS, HEAD_DIM, LANE), F32)],
        compiler_params=_cparams(("parallel", "arbitrary")),
    )(proj, proj, proj, conv_w, head_params, norm_g)


def _cmp_kernel(c_ref, wa_ref, wb_ref, pea_ref, peb_ref, cos_ref, sin_ref, o_ref):
    cb = c_ref[0]
    ya = _dot(cb + pea_ref[...], wa_ref[...])
    yb = _dot(cb + peb_ref[...], wb_ref[...])
    n = ya.shape[0]
    y = ya + pltpu.roll(yb, n - 1, 0)
    kc = _rope(y[:, :LANE], cos_ref[0], sin_ref[0])
    o_ref[0] = jnp.concatenate([kc, y[:, LANE:]], axis=-1)


def _compress(cmp_rows, wa, wb, pea, peb, cos_c, sin_c):
    bsz, n, width = cmp_rows.shape
    return pl.pallas_call(
        _cmp_kernel,
        grid=(bsz,),
        in_specs=[pl.BlockSpec((1, n, width), lambda b: (b, 0, 0)),
                  pl.BlockSpec(wa.shape, lambda b: (0, 0)),
                  pl.BlockSpec(wb.shape, lambda b: (0, 0)),
                  pl.BlockSpec((1, width), lambda b: (0, 0)),
                  pl.BlockSpec((1, width), lambda b: (0, 0)),
                  pl.BlockSpec((1, n, LANE), lambda b: (b, 0, 0)),
                  pl.BlockSpec((1, n, LANE), lambda b: (b, 0, 0))],
        out_specs=pl.BlockSpec((1, n, 2 * LANE), lambda b: (b, 0, 0)),
        out_shape=jax.ShapeDtypeStruct((bsz, n, 2 * LANE), F32),
        compiler_params=_cparams(("parallel",)),
    )(cmp_rows, wa, wb, pea, peb, cos_c, sin_c)


_NSA_KEY_BLOCK = 512


def _attend(k16, v16, q2, bias):
    d = HEAD_DIM
    half = q2.shape[1] // NSA_KV_HEADS
    s = jnp.dot(k16, q2, preferred_element_type=F32) + bias
    m = jnp.max(s, axis=0, keepdims=True)
    p = jnp.exp(s - m)
    l = jnp.sum(p, axis=0, keepdims=True)
    pv = lax.dot_general(v16, p.astype(BF16), (((0,), (0,)), ((), ())), preferred_element_type=F32)
    return m, l, jnp.concatenate([pv[:d, :half], pv[d:, half:]], axis=1), p


def _nsa_kernel(q_ref, gate_ref, kvc_ref, ksel_ref, kwin_ref, o_ref, mask_ref, *, seq):
    qb = Q_BLOCK
    d = HEAD_DIM
    grp = NSA_GROUP
    cols = grp * qb
    n_slc = seq // SEL_BLOCK
    topk = min(SEL_TOPK, n_slc)
    ncp = seq // CMP_STRIDE
    qi = pl.program_id(1)
    q_t = q_ref[0].T
    gate_t = _sigmoid(gate_ref[0]).T
    t_q = qi * qb + lax.broadcasted_iota(jnp.int32, (1, qb), 1)

    heads = NSA_KV_HEADS * grp
    tile_all = lambda a: jnp.concatenate([a] * heads, axis=1)
    n_idx = lax.broadcasted_iota(jnp.int32, (ncp, qb), 0)
    cmp_bias = tile_all(jnp.where((n_idx * CMP_STRIDE + (CMP_LEN - 1) <= t_q) & (n_idx < ncp - 1), 0.0, NEG))
    sj = lax.broadcasted_iota(jnp.int32, (n_slc, ncp), 0) * SEL_BLOCK
    ci = lax.broadcasted_iota(jnp.int32, (n_slc, ncp), 1) * CMP_STRIDE
    overlap_t = ((ci < sj + SEL_BLOCK) & (ci + CMP_LEN > sj) & (ci < (ncp - 1) * CMP_STRIDE)).astype(BF16)
    blk = lax.broadcasted_iota(jnp.int32, (n_slc, qb), 0)
    blk_f = blk.astype(F32)
    cur = t_q // SEL_BLOCK
    forced = (blk == 0) | (blk == cur) | (blk == cur - 1)
    kb = min(seq, _NSA_KEY_BLOCK)
    span = min(seq, WINDOW + qb)
    causal_bias = jnp.where(lax.broadcasted_iota(jnp.int32, (qb, qb), 0)
                            <= lax.broadcasted_iota(jnp.int32, (qb, qb), 1), 0.0, NEG)
    w0 = pl.multiple_of(jnp.maximum(qi * qb + qb - span, 0), qb)
    diff = t_q - (w0 + lax.broadcasted_iota(jnp.int32, (span, qb), 0))
    win_bias = tile_all(jnp.where((diff >= 0) & (diff < WINDOW), 0.0, NEG))
    kvc = kvc_ref[0]

    zeros_q = jnp.zeros((d, cols), BF16)
    q_h = [jnp.concatenate([q_t[(h * grp + g) * d:(h * grp + g + 1) * d, :] for g in range(grp)],
                           axis=1).astype(BF16) for h in range(NSA_KV_HEADS)]
    q2 = jnp.concatenate([jnp.concatenate([q_h[0], zeros_q], axis=0),
                          jnp.concatenate([zeros_q, q_h[1]], axis=0)], axis=1)

    m_c, l_c, o_c, p_c = _attend(kvc[:, :LANE].astype(BF16), kvc[:, LANE:].astype(BF16), q2, cmp_bias)
    inv_l = jnp.where(m_c > 0.5 * NEG, 1.0 / l_c, 0.0)
    o_c = o_c * inv_l
    p_c = p_c * inv_l

    for h in range(NSA_KV_HEADS):
        p_sum = p_c[:, h * cols:h * cols + qb]
        for g in range(1, grp):
            p_sum = p_sum + p_c[:, h * cols + g * qb:h * cols + (g + 1) * qb]
        imp = _dot_split3_rhs(overlap_t, p_sum)
        work = jnp.where(forced, FORCE, jnp.where(blk * SEL_BLOCK <= t_q, imp, -1.0))
        sel = jnp.zeros((n_slc, qb), F32)
        for _ in range(topk):
            best = jnp.max(work, axis=0, keepdims=True)
            idx = jnp.min(jnp.where(work == best, blk_f, 1e9), axis=0, keepdims=True)
            pick = blk_f == idx
            sel = jnp.where(pick, 1.0, sel)
            work = jnp.where(pick, -3.0, work)
        sel_bias = jnp.where((sel > 0.5) & (blk * SEL_BLOCK <= t_q), 0.0, NEG)
        for b in range(n_slc):
            mask_ref[b * SEL_BLOCK:(b + 1) * SEL_BLOCK, h * qb:(h + 1) * qb] = jnp.broadcast_to(
                sel_bias[b:b + 1, :], (SEL_BLOCK, qb))
        own_keys = pl.ds(pl.multiple_of(qi * qb, qb), qb)
        mask_ref[own_keys, h * qb:(h + 1) * qb] = mask_ref[own_keys, h * qb:(h + 1) * qb] + causal_bias

    def sel_body(j, carry):
        m, l, acc = carry
        off = pl.multiple_of(j * kb, kb)
        kv = ksel_ref[0, pl.ds(off, kb), :]
        mask = mask_ref[pl.ds(off, kb), :]
        bias = jnp.concatenate([mask[:, :qb]] * grp + [mask[:, qb:]] * grp, axis=1)
        m_b, l_b, pv_b, _ = _attend(kv[:, :LANE].astype(BF16), kv[:, LANE:].astype(BF16), q2, bias)
        m_new = jnp.maximum(m, m_b)
        w_old = jnp.exp(m - m_new)
        w_blk = jnp.exp(m_b - m_new)
        return m_new, w_old * l + w_blk * l_b, w_old * acc + w_blk * pv_b

    init = (jnp.full((1, 2 * cols), NEG, F32), jnp.zeros((1, 2 * cols), F32), jnp.zeros((d, 2 * cols), F32))
    _, l_s, acc_s = lax.fori_loop(0, (qi * qb + qb + kb - 1) // kb, sel_body, init)
    o_s = acc_s * (1.0 / l_s)

    kv = kwin_ref[0, pl.ds(w0, span), :]
    _, l_w, o_w, _ = _attend(kv[:, :LANE].astype(BF16), kv[:, LANE:].astype(BF16), q2, win_bias)
    o_w = o_w * (1.0 / l_w)

    out_rows = []
    for hd in range(heads):
        cs = slice(hd * qb, (hd + 1) * qb)
        out_rows.append(gate_t[3 * hd:3 * hd + 1] * o_c[:, cs] + gate_t[3 * hd + 1:3 * hd + 2] * o_s[:, cs]
                        + gate_t[3 * hd + 2:3 * hd + 3] * o_w[:, cs])
    o_ref[0] = jnp.concatenate(out_rows, axis=0).T


def _nsa_attention(proj, kvc):
    bsz, seq, _ = proj.shape
    qb = Q_BLOCK
    ncp = kvc.shape[1]
    kv = 2 * LANE
    return pl.pallas_call(
        functools.partial(_nsa_kernel, seq=seq),
        grid=(bsz, seq // qb),
        in_specs=[pl.BlockSpec((1, qb, NSA_DIM), lambda b, i: (b, i, COL_NQ // NSA_DIM)),
                  pl.BlockSpec((1, qb, LANE), lambda b, i: (b, i, COL_GATE // LANE)),
                  pl.BlockSpec((1, ncp, kv), lambda b, i: (b, 0, 0)),
                  pl.BlockSpec((1, seq, kv), lambda b, i: (b, 0, COL_SEL // kv)),
                  pl.BlockSpec((1, seq, kv), lambda b, i: (b, 0, COL_WIN // kv))],
        out_specs=pl.BlockSpec((1, qb, NSA_DIM), lambda b, i: (b, i, 0)),
        out_shape=jax.ShapeDtypeStruct((bsz, seq, NSA_DIM), F32),
        scratch_shapes=[pltpu.VMEM((seq, NSA_KV_HEADS * qb), F32)],
        compiler_params=_cparams(("parallel", "arbitrary")),
    )(proj, proj, kvc, proj, proj)


_CC_HALO = 32
_CC_ROWS = 64


def _cconv_kernel(u_ref, w_ref, b_ref, lg_ref, lb_ref, o_ref, xbuf):
    t = u_ref.shape[1]
    s = pl.program_id(1)

    @pl.when(s == 0)
    def _():
        xbuf[0:_CC_HALO, :] = jnp.zeros((_CC_HALO, CONV_CH), F32)

    @pl.when(s != 0)
    def _():
        xbuf[0:_CC_HALO, :] = xbuf[t:t + _CC_HALO, :]

    u = u_ref[0]
    xbuf[_CC_HALO:_CC_HALO + t, :] = u[:, :CONV_CH] * _sigmoid(u[:, CONV_CH:])
    w = w_ref[...]
    first = _CC_HALO - (CONV_WIDTH - 1)
    for r in range(t // _CC_ROWS):
        acc = jnp.broadcast_to(b_ref[...], (_CC_ROWS, CONV_CH))
        for res in range(SUBLANE):
            offs = [off for off in range(res, first + CONV_WIDTH, SUBLANE) if off >= first]
            window = xbuf[pl.ds(r * _CC_ROWS + res, offs[-1] - res + _CC_ROWS), :]
            for off in offs:
                acc = acc + w[off - first:off - first + 1] * window[off - res:off - res + _CC_ROWS]
        mu = jnp.mean(acc, axis=-1, keepdims=True)
        var = jnp.mean(jnp.square(acc - mu), axis=-1, keepdims=True)
        hn = (acc - mu) * lax.rsqrt(var + EPS) * lg_ref[...] + lb_ref[...]
        o_ref[0, r * _CC_ROWS:(r + 1) * _CC_ROWS, :] = _silu(hn)


def _cconv(proj, dw_w, dw_b, ln_g, ln_b):
    bsz, seq, _ = proj.shape
    t = min(seq, 256)
    wu = 2 * CONV_CH
    vec = pl.BlockSpec((1, CONV_CH), lambda b, s: (0, 0))
    return pl.pallas_call(
        _cconv_kernel,
        grid=(bsz, seq // t),
        in_specs=[pl.BlockSpec((1, t, wu), lambda b, s: (b, s, COL_CU // wu)),
                  pl.BlockSpec((CONV_WIDTH, CONV_CH), lambda b, s: (0, 0)),
                  vec, vec, vec],
        out_specs=pl.BlockSpec((1, t, CONV_CH), lambda b, s: (b, s, 0)),
        out_shape=jax.ShapeDtypeStruct((bsz, seq, CONV_CH), F32),
        scratch_shapes=[pltpu.VMEM((t + _CC_HALO, CONV_CH), F32)],
        compiler_params=_cparams(("parallel", "arbitrary")),
    )(proj, dw_w, dw_b, ln_g, ln_b)


_FFN_TILE = 256


def _ffn_kernel(x_ref, oa_ref, ob_ref, oc_ref, wo_ref, gmix_ref, gpre_ref, wup_ref, cw_ref, wd_ref, gpost_ref,
                o_ref, xn_ref, act_ref, stage_ref, carry_ref):
    tm = x_ref.shape[1]
    d_ff = wd_ref.shape[0]
    tf = _FFN_TILE
    hal = SUBLANE

    @pl.when(pl.program_id(1) == 0)
    def _():
        carry_ref[...] = jnp.zeros_like(carry_ref)

    mix = (jnp.dot(oa_ref[0].astype(BF16), wo_ref[:GDN_DIM], preferred_element_type=F32)
           + jnp.dot(ob_ref[0].astype(BF16), wo_ref[GDN_DIM:GDN_DIM + NSA_DIM], preferred_element_type=F32)
           + jnp.dot(oc_ref[0].astype(BF16), wo_ref[GDN_DIM + NSA_DIM:], preferred_element_type=F32))
    x = x_ref[0] + _rms(mix, gmix_ref[...])
    xn_ref[...] = _rms(x, gpre_ref[...]).astype(BF16)
    for f in range(d_ff // tf):
        ys = []
        for part in range(2):
            cols = slice(part * d_ff + f * tf, part * d_ff + (f + 1) * tf)
            h = jnp.dot(xn_ref[...], wup_ref[:, cols], preferred_element_type=F32)
            stage = stage_ref.at[f % 2, part]
            stage[0:hal, :] = carry_ref[:, cols]
            stage[hal:hal + tm, :] = h
            carry_ref[:, cols] = h[tm - hal:tm, :]
            cw = cw_ref[:, cols]
            ys.append(cw[0:1] * stage[pl.ds(hal - 2, tm), :] + cw[1:2] * stage[pl.ds(hal - 1, tm), :]
                      + cw[2:3] * h)
        act_ref[:, f * tf:(f + 1) * tf] = (_silu(ys[0]) * ys[1]).astype(BF16)
    out = jnp.dot(act_ref[...], wd_ref[...], preferred_element_type=F32)
    o_ref[0] = x + _rms(out, gpost_ref[...])


def _outproj_ffn(x, o_a, o_b, o_c, w_out, g_mix, g_pre, w_up, conv_w, w_down, g_post):
    bsz, seq, d = x.shape
    d_ff = w_down.shape[0]
    tm = min(seq, 512)
    kw = conv_w.shape[0]
    resident = lambda shape: pl.BlockSpec(shape, lambda b, i: (0, 0), pipeline_mode=pl.Buffered(1))
    rows = lambda width: pl.BlockSpec((1, tm, width), lambda b, i: (b, i, 0))
    return pl.pallas_call(
        _ffn_kernel,
        grid=(bsz, seq // tm),
        in_specs=[rows(d), rows(GDN_DIM), rows(NSA_DIM), rows(CONV_CH),
                  resident((d, d)),
                  pl.BlockSpec((1, d), lambda b, i: (0, 0)),
                  pl.BlockSpec((1, d), lambda b, i: (0, 0)),
                  resident((d, 2 * d_ff)),
                  resident((kw, 2 * d_ff)),
                  resident((d_ff, d)),
                  pl.BlockSpec((1, d), lambda b, i: (0, 0))],
        out_specs=pl.BlockSpec((1, tm, d), lambda b, i: (b, i, 0)),
        out_shape=jax.ShapeDtypeStruct((bsz, seq, d), F32),
        scratch_shapes=[pltpu.VMEM((tm, d), BF16),
                        pltpu.VMEM((tm, d_ff), BF16),
                        pltpu.VMEM((2, 2, tm + SUBLANE, _FFN_TILE), F32),
                        pltpu.VMEM((SUBLANE, 2 * d_ff), F32)],
        compiler_params=_cparams(("parallel", "arbitrary")),
    )(x, o_a, o_b, o_c, w_out, g_mix, g_pre, w_up, conv_w, w_down, g_post)


def _pack_w_in(w_in):
    depth, d, _ = w_in.shape
    sizes = (GDN_DIM,) * 4 + (GDN_HEADS,) * 2 + (NSA_DIM,) + (LANE,) * 6 + (3 * NSA_HEADS, 2 * CONV_CH)
    offs = np.concatenate([[0], np.cumsum(sizes)])
    piece = lambda k: w_in[:, :, offs[k]:offs[k + 1]]
    zeros = lambda n: jnp.zeros((depth, d, n), w_in.dtype)
    gq, gk, gv, gz, ga, gb, nq, nkc, nvc, nks, nvs, nkw, nvw, ngate, cu = [piece(k) for k in range(15)]
    cols = [gq, gk, gv, gz, nq,
            ga, gb, zeros(LANE - 2 * GDN_HEADS),
            cu, nks, nvs, nkw, nvw,
            ngate, zeros(LANE - 3 * NSA_HEADS),
            nkc, nvc]
    packed = jnp.concatenate(cols, axis=-1)
    assert packed.shape[-1] == PROJ_DIM
    return packed.astype(BF16)


def _pack_compress(wk, wv, pe_k, pe_v):
    half = CMP_LEN // 2
    d = HEAD_DIM
    big = jnp.zeros((CMP_LEN, 4, d, 4, d), F32)
    for slot, w in enumerate((wk, wk, wv, wv)):
        big = big.at[:, slot, :, slot, :].set(w)
    big = big.reshape(CMP_LEN, 4 * d, 4 * d)
    wa = big[:half].reshape(half * 4 * d, 4 * d).astype(BF16)
    wb = big[half:].reshape(half * 4 * d, 4 * d).astype(BF16)
    pe = jnp.concatenate([pe_k, pe_k, pe_v, pe_v], axis=-1)
    pea = pe[:half].reshape(1, half * 4 * d)
    peb = pe[half:].reshape(1, half * 4 * d)
    return wa, wb, pea, peb


def kernel(x, positions, norm_mix_pre, norm_mix_post, norm_ffn_pre, norm_ffn_post, w_in, w_out, gdn_conv_w, gdn_a_log, gdn_dt_bias, gdn_norm_g, nsa_cmp_wk, nsa_cmp_wv, nsa_cmp_pe_k, nsa_cmp_pe_v, cc_dw_w, cc_dw_b, cc_ln_g, cc_ln_b, ffn_w_up, ffn_conv_w, ffn_w_down):
    bsz, seq, d = x.shape
    depth = w_in.shape[0]
    assert seq % Q_BLOCK == 0 and d == GDN_DIM + NSA_DIM + CONV_CH
    m = bsz * seq

    cos, sin = _rope_tables(positions)
    ncp = seq // CMP_STRIDE
    pad_rows = lambda t: jnp.pad(t[:, CMP_LEN - 1::CMP_STRIDE], ((0, 0), (0, 1), (0, 0)))
    cos_c, sin_c = pad_rows(cos), pad_rows(sin)

    w_in_p = _pack_w_in(w_in)
    w_out_b = w_out.astype(BF16)
    w_up_b = ffn_w_up.astype(BF16)
    w_down_b = ffn_w_down.astype(BF16)
    head_params = jnp.zeros((depth, SUBLANE, LANE), F32)
    head_params = head_params.at[:, 0, :GDN_HEADS].set(gdn_a_log).at[:, 1, :GDN_HEADS].set(gdn_dt_bias)

    cos2d = cos.reshape(m, LANE)
    sin2d = sin.reshape(m, LANE)
    for l in range(depth):
        proj, cmp_cols = _inproj(x.reshape(m, d), norm_mix_pre[l][None], w_in_p[l], cos2d, sin2d)
        proj = proj.reshape(bsz, seq, PROJ_OUT)
        o_a = _gdn(proj, gdn_conv_w[l], head_params[l], jnp.tile(gdn_norm_g[l], GDN_HEADS)[None])
        wa, wb, pea, peb = _pack_compress(nsa_cmp_wk[l], nsa_cmp_wv[l], nsa_cmp_pe_k[l], nsa_cmp_pe_v[l])
        kvc = _compress(cmp_cols.reshape(bsz, ncp, CMP_STRIDE * 2 * LANE), wa, wb, pea, peb, cos_c, sin_c)
        o_b = _nsa_attention(proj, kvc)
        o_c = _cconv(proj, cc_dw_w[l], cc_dw_b[l][None], cc_ln_g[l][None], cc_ln_b[l][None])
        x = _outproj_ffn(x, o_a, o_b, o_c, w_out_b[l], norm_mix_post[l][None], norm_ffn_pre[l][None],
                         w_up_b[l], ffn_conv_w[l], w_down_b[l], norm_ffn_post[l][None])
    return x
```

```python
import functools

import jax
import jax.numpy as jnp
import numpy as np
from jax import lax
from jax.experimental import pallas as pl
from jax.experimental.pallas import tpu as pltpu

F32 = jnp.float32
BF16 = jnp.bfloat16

HEAD_DIM = 64
GDN_HEADS = 6
GDN_DIM = GDN_HEADS * HEAD_DIM
GDN_CONV = 4
GDN_CHUNK = 64
NSA_HEADS = 6
NSA_KV_HEADS = 2
NSA_GROUP = NSA_HEADS // NSA_KV_HEADS
NSA_DIM = NSA_HEADS * HEAD_DIM
CMP_STRIDE = 16
CMP_LEN = 32
SEL_BLOCK = 64
SEL_TOPK = 8
WINDOW = 512
Q_BLOCK = 128
CONV_CH = 256
CONV_WIDTH = 31
ROPE_THETA = 10000.0
EPS = 1e-6
NEG = -1e30
FORCE = 1e4

LANE = 128
SUBLANE = 8
VMEM_LIMIT = 56 * 1024 * 1024

COL_QKV = 0
COL_Z = 1152
COL_NQ = 1536
COL_GAB = 1920
COL_CU = 2048
COL_SEL = 2560
COL_WIN = 2816
COL_GATE = 3072
PROJ_OUT = 3200
COL_CMP = 3200
PROJ_DIM = 3456


def _cparams(sem):
    return pltpu.CompilerParams(dimension_semantics=sem, vmem_limit_bytes=VMEM_LIMIT)


def _sigmoid(x):
    return 1.0 / (1.0 + jnp.exp(-x))


def _silu(x):
    return x * _sigmoid(x)


def _dot(a, b):
    return jnp.dot(a.astype(BF16), b.astype(BF16), preferred_element_type=F32)


def _dot_nt(a, b):
    return lax.dot_general(a.astype(BF16), b.astype(BF16), (((1,), (1,)), ((), ())),
                           preferred_element_type=F32)


def _dot_split3_rhs(sel, x):
    hi = x.astype(BF16)
    r1 = x - hi.astype(F32)
    mid = r1.astype(BF16)
    lo = (r1 - mid.astype(F32)).astype(BF16)
    return (jnp.dot(sel, hi, preferred_element_type=F32) + jnp.dot(sel, mid, preferred_element_type=F32)
            + jnp.dot(sel, lo, preferred_element_type=F32))


def _dot_split(x, sel):
    hi = x.astype(BF16)
    lo = (x - hi.astype(F32)).astype(BF16)
    return (jnp.dot(hi, sel, preferred_element_type=F32) + jnp.dot(lo, sel, preferred_element_type=F32))


def _rms(x, g):
    return x * lax.rsqrt(jnp.mean(x * x, axis=-1, keepdims=True) + EPS) * g


def _rope_table_kernel(pos_ref, inv_ref, sign_ref, cos_ref, sin_ref):
    ang = pos_ref[0].astype(F32) * inv_ref[...]
    cos_ref[0] = jnp.cos(ang)
    sin_ref[0] = jnp.sin(ang) * sign_ref[...]


def _rope_tables(positions):
    bsz, seq = positions.shape
    t = min(seq, 512)
    inv = 1.0 / (ROPE_THETA ** (jnp.arange(0, HEAD_DIM, 2, dtype=F32) / HEAD_DIM))
    inv = jnp.tile(inv, LANE // (HEAD_DIM // 2))[None, :]
    sign = jnp.tile(jnp.concatenate([-jnp.ones(HEAD_DIM // 2, F32), jnp.ones(HEAD_DIM // 2, F32)]),
                    LANE // HEAD_DIM)[None, :]
    out = jax.ShapeDtypeStruct((bsz, seq, LANE), F32)
    return pl.pallas_call(
        _rope_table_kernel,
        grid=(bsz, seq // t),
        in_specs=[pl.BlockSpec((1, t, 1), lambda b, s: (b, s, 0)),
                  pl.BlockSpec((1, LANE), lambda b, s: (0, 0)),
                  pl.BlockSpec((1, LANE), lambda b, s: (0, 0))],
        out_specs=[pl.BlockSpec((1, t, LANE), lambda b, s: (b, s, 0))] * 2,
        out_shape=[out, out],
        compiler_params=_cparams(("parallel", "parallel")),
    )(positions[:, :, None], inv, sign)


def _rope(x, cos, sin):
    lane = lax.broadcasted_iota(jnp.int32, x.shape, 1)
    first_half = (lane % HEAD_DIM) < (HEAD_DIM // 2)
    partner = jnp.where(first_half, pltpu.roll(x, LANE - HEAD_DIM // 2, 1), pltpu.roll(x, HEAD_DIM // 2, 1))
    return x * cos + partner * sin


def _inproj_kernel(x_ref, g_ref, w_ref, cos_ref, sin_ref, o_ref, cmp_ref):
    xn = _rms(x_ref[...], g_ref[...]).astype(BF16)
    y = jnp.dot(xn, w_ref[...], preferred_element_type=F32)
    cos = cos_ref[...]
    sin = sin_ref[...]
    scale = HEAD_DIM ** -0.5
    o_ref[:, :COL_NQ] = y[:, :COL_NQ]
    for c in range(COL_NQ, COL_NQ + NSA_DIM, LANE):
        o_ref[:, c:c + LANE] = _rope(y[:, c:c + LANE], cos, sin) * scale
    o_ref[:, COL_GAB:COL_SEL] = y[:, COL_GAB:COL_SEL]
    for c in (COL_SEL, COL_WIN):
        o_ref[:, c:c + LANE] = _rope(y[:, c:c + LANE], cos, sin)
        o_ref[:, c + LANE:c + 2 * LANE] = y[:, c + LANE:c + 2 * LANE]
    o_ref[:, COL_GATE:PROJ_OUT] = y[:, COL_GATE:PROJ_OUT]
    cmp_ref[...] = y[:, COL_CMP:]


def _inproj(x2d, g, w, cos2d, sin2d):
    m, d = x2d.shape
    n = w.shape[1]
    tm = min(m, 512)
    row = lambda width: pl.BlockSpec((tm, width), lambda i: (i, 0))
    return pl.pallas_call(
        _inproj_kernel,
        grid=(m // tm,),
        in_specs=[row(d),
                  pl.BlockSpec((1, d), lambda i: (0, 0)),
                  pl.BlockSpec((d, n), lambda i: (0, 0), pipeline_mode=pl.Buffered(1)),
                  row(LANE), row(LANE)],
        out_specs=[row(PROJ_OUT), row(n - PROJ_OUT)],
        out_shape=[jax.ShapeDtypeStruct((m, PROJ_OUT), F32), jax.ShapeDtypeStruct((m, n - PROJ_OUT), F32)],
        compiler_params=_cparams(("parallel",)),
    )(x2d, g, w, cos2d, sin2d)


_GDN_PAIR = 2 * GDN_CHUNK


def _gdn_kernel(qkv_ref, z_ref, gab_ref, cw_ref, hp_ref, ng_ref, o_ref, xbuf, state):
    c = GDN_CHUNK
    d = HEAD_DIM
    pr = _GDN_PAIR
    t = qkv_ref.shape[1]
    s = pl.program_id(1)

    @pl.when(s == 0)
    def _():
        xbuf[0:SUBLANE, :] = jnp.zeros((SUBLANE, 3 * GDN_DIM), F32)
        state[...] = jnp.zeros_like(state)

    @pl.when(s != 0)
    def _():
        xbuf[0:SUBLANE, :] = xbuf[t:t + SUBLANE, :]

    xbuf[SUBLANE:SUBLANE + t, :] = qkv_ref[0]
    cw = cw_ref[...]
    y = cw[GDN_CONV - 1:GDN_CONV] * xbuf[pl.ds(SUBLANE, t), :]
    for j in range(GDN_CONV - 2, -1, -1):
        y = y + cw[j:j + 1] * xbuf[pl.ds(SUBLANE - (GDN_CONV - 1) + j, t), :]
    y = _silu(y)

    gab = gab_ref[0]
    hp = hp_ref[...]
    sp_in = gab + hp[1:2]
    softplus = jnp.maximum(sp_in, 0.0) + jnp.log(1.0 + jnp.exp(-jnp.abs(sp_in)))
    gcum = -jnp.exp(hp[0:1]) * softplus
    beta = _sigmoid(gab)
    in_chunk = lax.broadcasted_iota(jnp.int32, (t, LANE), 0) % c
    shift = 1
    while shift < c:
        gcum = gcum + jnp.where(in_chunk >= shift, pltpu.roll(gcum, shift, 0), 0.0)
        shift *= 2
    g_last = jnp.concatenate(
        [jnp.broadcast_to(gcum[(i + 1) * c - 1:(i + 1) * c, :], (c, LANE)) for i in range(t // c)], axis=0)
    eg = jnp.exp(gcum)
    k_dec = jnp.exp(g_last - gcum)

    lane_h = lax.broadcasted_iota(jnp.int32, (LANE, GDN_DIM), 0)
    col_h = lax.broadcasted_iota(jnp.int32, (LANE, GDN_DIM), 1) // d
    expand_a = (lane_h == col_h).astype(BF16)
    expand_b = (lane_h == col_h + GDN_HEADS).astype(BF16)
    same_head = (lax.broadcasted_iota(jnp.int32, (GDN_DIM, GDN_DIM), 0) // d
                 == lax.broadcasted_iota(jnp.int32, (GDN_DIM, GDN_DIM), 1) // d).astype(BF16)
    beta_e = _dot_split(beta, expand_b)
    eg_e = _dot_split(eg, expand_a)
    kdec_e = _dot_split(k_dec, expand_a)

    q = y[:, :GDN_DIM]
    k = y[:, GDN_DIM:2 * GDN_DIM]
    v = y[:, 2 * GDN_DIM:]
    q = q * (lax.rsqrt(_dot_split(q * q, same_head) + EPS) * (d ** -0.5))
    k = k * lax.rsqrt(_dot_split(k * k, same_head) + EPS)
    k16 = k.astype(BF16)
    kbeta = k * beta_e
    first = (lax.broadcasted_iota(jnp.int32, (t, GDN_DIM), 1) % LANE) < d
    q16 = [jnp.where(first, q, 0.0).astype(BF16), jnp.where(first, 0.0, q).astype(BF16)]
    kb16 = [jnp.where(first, kbeta, 0.0).astype(BF16), jnp.where(first, 0.0, kbeta).astype(BF16)]
    vb = v * beta_e
    kbe = kbeta * eg_e
    qe16 = (q * eg_e).astype(BF16)
    kd16 = (k * kdec_e).astype(BF16)

    row = lax.broadcasted_iota(jnp.int32, (pr, pr), 0)
    col = lax.broadcasted_iota(jnp.int32, (pr, pr), 1)
    same_chunk = (row // c) == (col // c)
    tri = same_chunk & (row >= col)
    tri_strict = same_chunk & (row > col)
    low = lax.broadcasted_iota(jnp.int32, (pr, LANE), 1) < d
    low_c = lax.broadcasted_iota(jnp.int32, (c, LANE), 1) < d
    n_pair = t // pr
    gcum_t = [gcum[p * pr:(p + 1) * pr].T for p in range(n_pair)]
    chains = [(h, p) for p in range(n_pair) for h in range(GDN_HEADS)]

    sol, pw, qk = {}, {}, {}
    for h, p in chains:
        r = slice(p * pr, (p + 1) * pr)
        g = slice((h // 2) * LANE, (h // 2 + 1) * LANE)
        decay = jnp.exp(jnp.where(tri, gcum[r, h:h + 1] - gcum_t[p][h:h + 1, :], NEG))
        k2 = k16[r, g]
        kk = _dot_nt(kb16[h % 2][r, g], k2)
        pw[h, p] = jnp.where(tri_strict, -(kk * decay), 0.0).astype(BF16)
        qk[h, p] = (_dot_nt(q16[h % 2][r, g], k2) * decay).astype(BF16)
        ke_sw = pltpu.roll(kbe[r, g], d, 1)
        sol[h, p] = jnp.where(low, vb[r, g], ke_sw) if h % 2 == 0 else jnp.where(low, ke_sw, vb[r, g])
    for hp_ in chains:
        sol[hp_] = sol[hp_] + _dot(pw[hp_], sol[hp_])
    for _ in range(5):
        for hp_ in chains:
            pw[hp_] = _dot(pw[hp_], pw[hp_]).astype(BF16)
        for hp_ in chains:
            sol[hp_] = sol[hp_] + _dot(pw[hp_], sol[hp_])

    st = [state[h] for h in range(GDN_HEADS)]
    zeros = jnp.zeros((c, LANE), BF16)
    v_new, q_st = {}, {}
    for i in range(t // c):
        p, ic = divmod(i, pr // c)
        rc = slice(i * c, (i + 1) * c)
        sc = slice(ic * c, (ic + 1) * c)
        for h in range(GDN_HEADS):
            g = slice((h // 2) * LANE, (h // 2 + 1) * LANE)
            s16 = st[h].astype(BF16)
            even = h % 2 == 0
            s_w = jnp.concatenate([zeros, s16] if even else [s16, zeros], axis=0)
            s_q = jnp.concatenate([s16, zeros] if even else [zeros, s16], axis=0)
            sol_c = sol[h, p][sc]
            vn = sol_c - _dot(sol_c, s_w)
            q_st[h, i] = _dot(qe16[rc, g], s_q)
            upd = lax.dot_general(kd16[rc, g], vn.astype(BF16), (((0,), (0,)), ((), ())),
                                  preferred_element_type=F32)
            upd = upd[:c] if even else upd[c:]
            el = eg[(i + 1) * c - 1:(i + 1) * c, h:h + 1]
            st[h] = jnp.where(low_c if even else ~low_c, st[h] * el + upd, 0.0)
            v_new[h, i] = vn
    for h in range(GDN_HEADS):
        state[h] = st[h]

    per = pr // c
    groups = []
    for j in range(GDN_HEADS // 2):
        halves = []
        for h in (2 * j, 2 * j + 1):
            rows_out = []
            for p in range(n_pair):
                vn_pair = jnp.concatenate([v_new[h, p * per + ic] for ic in range(per)], axis=0)
                qs_pair = jnp.concatenate([q_st[h, p * per + ic] for ic in range(per)], axis=0)
                rows_out.append(qs_pair + _dot(qk[h, p], vn_pair))
            halves.append(jnp.concatenate(rows_out, axis=0))
        low_t = lax.broadcasted_iota(jnp.int32, (t, LANE), 1) < d
        groups.append(jnp.where(low_t, halves[0], halves[1]))
    o = jnp.concatenate(groups, axis=-1)
    ms = _dot_split(o * o, same_head) * (1.0 / d)
    o_ref[0] = o * lax.rsqrt(ms + EPS) * ng_ref[...] * _silu(z_ref[0])


def _gdn(proj, conv_w, head_params, norm_g):
    bsz, seq, _ = proj.shape
    t = min(seq, 256)
    w_qkv = 3 * GDN_DIM
    return pl.pallas_call(
        _gdn_kernel,
        grid=(bsz, seq // t),
        in_specs=[pl.BlockSpec((1, t, w_qkv), lambda b, s: (b, s, COL_QKV // w_qkv)),
                  pl.BlockSpec((1, t, GDN_DIM), lambda b, s: (b, s, COL_Z // GDN_DIM)),
                  pl.BlockSpec((1, t, LANE), lambda b, s: (b, s, COL_GAB // LANE)),
                  pl.BlockSpec((GDN_CONV, w_qkv), lambda b, s: (0, 0)),
                  pl.BlockSpec((SUBLANE, LANE), lambda b, s: (0, 0)),
                  pl.BlockSpec((1, GDN_DIM), lambda b, s: (0, 0))],
        out_specs=pl.BlockSpec((1, t, GDN_DIM), lambda b, s: (b, s, 0)),
        out_shape=jax.ShapeDtypeStruct((bsz, seq, GDN_DIM), F32),
        scratch_shapes=[pltpu.VMEM((t + SUBLANE, w_qkv), F32),
                        pltpu.VMEM((GDN_HEADS, HEAD_DIM, LANE), F32)],
        compiler_params=_cparams(("parallel", "arbitrary")),
    )(proj, proj, proj, conv_w, head_params, norm_g)


def _cmp_kernel(c_ref, wa_ref, wb_ref, pea_ref, peb_ref, cos_ref, sin_ref, o_ref):
    cb = c_ref[0]
    ya = _dot(cb + pea_ref[...], wa_ref[...])
    yb = _dot(cb + peb_ref[...], wb_ref[...])
    n = ya.shape[0]
    y = ya + pltpu.roll(yb, n - 1, 0)
    kc = _rope(y[:, :LANE], cos_ref[0], sin_ref[0])
    o_ref[0] = jnp.concatenate([kc, y[:, LANE:]], axis=-1)


def _compress(cmp_rows, wa, wb, pea, peb, cos_c, sin_c):
    bsz, n, width = cmp_rows.shape
    return pl.pallas_call(
        _cmp_kernel,
        grid=(bsz,),
        in_specs=[pl.BlockSpec((1, n, width), lambda b: (b, 0, 0)),
                  pl.BlockSpec(wa.shape, lambda b: (0, 0)),
                  pl.BlockSpec(wb.shape, lambda b: (0, 0)),
                  pl.BlockSpec((1, width), lambda b: (0, 0)),
                  pl.BlockSpec((1, width), lambda b: (0, 0)),
                  pl.BlockSpec((1, n, LANE), lambda b: (b, 0, 0)),
                  pl.BlockSpec((1, n, LANE), lambda b: (b, 0, 0))],
        out_specs=pl.BlockSpec((1, n, 2 * LANE), lambda b: (b, 0, 0)),
        out_shape=jax.ShapeDtypeStruct((bsz, n, 2 * LANE), F32),
        compiler_params=_cparams(("parallel",)),
    )(cmp_rows, wa, wb, pea, peb, cos_c, sin_c)


_NSA_KEY_BLOCK = 512


def _attend(k16, v16, q2, bias):
    d = HEAD_DIM
    half = q2.shape[1] // NSA_KV_HEADS
    s = jnp.dot(k16, q2, preferred_element_type=F32) + bias
    m = jnp.max(s, axis=0, keepdims=True)
    p = jnp.exp(s - m)
    l = jnp.sum(p, axis=0, keepdims=True)
    pv = lax.dot_general(v16, p.astype(BF16), (((0,), (0,)), ((), ())), preferred_element_type=F32)
    return m, l, jnp.concatenate([pv[:d, :half], pv[d:, half:]], axis=1), p


def _attend_split(kv, q2, bias, parts):
    d = HEAD_DIM
    half = q2.shape[1] // NSA_KV_HEADS
    rows = kv.shape[0] // parts
    sl = [slice(i * rows, (i + 1) * rows) for i in range(parts)]
    ss = [jnp.dot(kv[r, :LANE].astype(BF16), q2, preferred_element_type=F32) + bias[r] for r in sl]
    ms = [jnp.max(s, axis=0, keepdims=True) for s in ss]
    ps = [jnp.exp(s - m) for s, m in zip(ss, ms)]
    ls = [jnp.sum(p, axis=0, keepdims=True) for p in ps]
    pvs = [lax.dot_general(kv[r, LANE:].astype(BF16), p.astype(BF16), (((0,), (0,)), ((), ())),
                           preferred_element_type=F32) for r, p in zip(sl, ps)]
    pvs = [jnp.concatenate([pv[:d, :half], pv[d:, half:]], axis=1) for pv in pvs]
    m = ms[0]
    for mi in ms[1:]:
        m = jnp.maximum(m, mi)
    ws = [jnp.exp(mi - m) for mi in ms]
    l = ws[0] * ls[0]
    o = ws[0] * pvs[0]
    for w, li, pvi in zip(ws[1:], ls[1:], pvs[1:]):
        l = l + w * li
        o = o + w * pvi
    return m, l, o


def _nsa_kernel(q_ref, gate_ref, kvc_ref, ksel_ref, kwin_ref, o_ref, mask_ref, *, seq):
    qb = Q_BLOCK
    d = HEAD_DIM
    grp = NSA_GROUP
    cols = grp * qb
    n_slc = seq // SEL_BLOCK
    topk = min(SEL_TOPK, n_slc)
    ncp = seq // CMP_STRIDE
    qi = pl.program_id(1)
    q_t = q_ref[0].T
    gate_t = _sigmoid(gate_ref[0]).T
    t_q = qi * qb + lax.broadcasted_iota(jnp.int32, (1, qb), 1)

    heads = NSA_KV_HEADS * grp
    tile_all = lambda a: jnp.concatenate([a] * heads, axis=1)
    n_idx = lax.broadcasted_iota(jnp.int32, (ncp, qb), 0)
    cmp_bias = tile_all(jnp.where((n_idx * CMP_STRIDE + (CMP_LEN - 1) <= t_q) & (n_idx < ncp - 1), 0.0, NEG))
    sj = lax.broadcasted_iota(jnp.int32, (n_slc, ncp), 0) * SEL_BLOCK
    ci = lax.broadcasted_iota(jnp.int32, (n_slc, ncp), 1) * CMP_STRIDE
    overlap_t = ((ci < sj + SEL_BLOCK) & (ci + CMP_LEN > sj) & (ci < (ncp - 1) * CMP_STRIDE)).astype(BF16)
    blk = lax.broadcasted_iota(jnp.int32, (n_slc, qb), 0)
    blk_f = blk.astype(F32)
    cur = t_q // SEL_BLOCK
    forced = (blk == 0) | (blk == cur) | (blk == cur - 1)
    kb = min(seq, _NSA_KEY_BLOCK)
    span = min(seq, WINDOW + qb)
    causal_bias = jnp.where(lax.broadcasted_iota(jnp.int32, (qb, qb), 0)
                            <= lax.broadcasted_iota(jnp.int32, (qb, qb), 1), 0.0, NEG)
    w0 = pl.multiple_of(jnp.maximum(qi * qb + qb - span, 0), qb)
    diff = t_q - (w0 + lax.broadcasted_iota(jnp.int32, (span, qb), 0))
    win_bias = tile_all(jnp.where((diff >= 0) & (diff < WINDOW), 0.0, NEG))
    kvc = kvc_ref[0]

    zeros_q = jnp.zeros((d, cols), BF16)
    q_h = [jnp.concatenate([q_t[(h * grp + g) * d:(h * grp + g + 1) * d, :] for g in range(grp)],
                           axis=1).astype(BF16) for h in range(NSA_KV_HEADS)]
    q2 = jnp.concatenate([jnp.concatenate([q_h[0], zeros_q], axis=0),
                          jnp.concatenate([zeros_q, q_h[1]], axis=0)], axis=1)

    m_c, l_c, o_c, p_c = _attend(kvc[:, :LANE].astype(BF16), kvc[:, LANE:].astype(BF16), q2, cmp_bias)
    inv_l = jnp.where(m_c > 0.5 * NEG, 1.0 / l_c, 0.0)
    o_c = o_c * inv_l
    p_c = p_c * inv_l

    _, l_w, o_w = _attend_split(kwin_ref[0, pl.ds(w0, span), :], q2, win_bias, 2)
    o_w = o_w * (1.0 / l_w)

    for h in range(NSA_KV_HEADS):
        p_sum = p_c[:, h * cols:h * cols + qb]
        for g in range(1, grp):
            p_sum = p_sum + p_c[:, h * cols + g * qb:h * cols + (g + 1) * qb]
        imp = _dot_split3_rhs(overlap_t, p_sum)
        work = jnp.where(forced, FORCE, jnp.where(blk * SEL_BLOCK <= t_q, imp, -1.0))
        sel = jnp.zeros((n_slc, qb), F32)
        for _ in range(topk):
            best = jnp.max(work, axis=0, keepdims=True)
            idx = jnp.min(jnp.where(work == best, blk_f, 1e9), axis=0, keepdims=True)
            pick = blk_f == idx
            sel = jnp.where(pick, 1.0, sel)
            work = jnp.where(pick, -3.0, work)
        sel_bias = jnp.where((sel > 0.5) & (blk * SEL_BLOCK <= t_q), 0.0, NEG)
        for b in range(n_slc):
            mask_ref[b * SEL_BLOCK:(b + 1) * SEL_BLOCK, h * qb:(h + 1) * qb] = jnp.broadcast_to(
                sel_bias[b:b + 1, :], (SEL_BLOCK, qb))
        own_keys = pl.ds(pl.multiple_of(qi * qb, qb), qb)
        mask_ref[own_keys, h * qb:(h + 1) * qb] = mask_ref[own_keys, h * qb:(h + 1) * qb] + causal_bias

    def sel_body(j, carry):
        m, l, acc = carry
        off = pl.multiple_of(j * kb, kb)
        kv = ksel_ref[0, pl.ds(off, kb), :]
        mask = mask_ref[pl.ds(off, kb), :]
        bias = jnp.concatenate([mask[:, :qb]] * grp + [mask[:, qb:]] * grp, axis=1)
        m_b, l_b, pv_b = _attend_split(kv, q2, bias, 2)
        m_new = jnp.maximum(m, m_b)
        w_old = jnp.exp(m - m_new)
        w_blk = jnp.exp(m_b - m_new)
        return m_new, w_old * l + w_blk * l_b, w_old * acc + w_blk * pv_b

    init = (jnp.full((1, 2 * cols), NEG, F32), jnp.zeros((1, 2 * cols), F32), jnp.zeros((d, 2 * cols), F32))
    _, l_s, acc_s = lax.fori_loop(0, (qi * qb + qb + kb - 1) // kb, sel_body, init)
    o_s = acc_s * (1.0 / l_s)

    out_rows = []
    for hd in range(heads):
        cs = slice(hd * qb, (hd + 1) * qb)
        out_rows.append(gate_t[3 * hd:3 * hd + 1] * o_c[:, cs] + gate_t[3 * hd + 1:3 * hd + 2] * o_s[:, cs]
                        + gate_t[3 * hd + 2:3 * hd + 3] * o_w[:, cs])
    o_ref[0] = jnp.concatenate(out_rows, axis=0).T


def _nsa_attention(proj, kvc):
    bsz, seq, _ = proj.shape
    qb = Q_BLOCK
    ncp = kvc.shape[1]
    kv = 2 * LANE
    return pl.pallas_call(
        functools.partial(_nsa_kernel, seq=seq),
        grid=(bsz, seq // qb),
        in_specs=[pl.BlockSpec((1, qb, NSA_DIM), lambda b, i: (b, i, COL_NQ // NSA_DIM)),
                  pl.BlockSpec((1, qb, LANE), lambda b, i: (b, i, COL_GATE // LANE)),
                  pl.BlockSpec((1, ncp, kv), lambda b, i: (b, 0, 0)),
                  pl.BlockSpec((1, seq, kv), lambda b, i: (b, 0, COL_SEL // kv)),
                  pl.BlockSpec((1, seq, kv), lambda b, i: (b, 0, COL_WIN // kv))],
        out_specs=pl.BlockSpec((1, qb, NSA_DIM), lambda b, i: (b, i, 0)),
        out_shape=jax.ShapeDtypeStruct((bsz, seq, NSA_DIM), F32),
        scratch_shapes=[pltpu.VMEM((seq, NSA_KV_HEADS * qb), F32)],
        compiler_params=_cparams(("parallel", "arbitrary")),
    )(proj, proj, kvc, proj, proj)


_CC_HALO = 32
_CC_ROWS = 64


def _cconv_kernel(u_ref, w_ref, b_ref, lg_ref, lb_ref, o_ref, xbuf, shifted):
    t = u_ref.shape[1]
    s = pl.program_id(1)

    @pl.when(s == 0)
    def _():
        xbuf[0:_CC_HALO, :] = jnp.zeros((_CC_HALO, CONV_CH), F32)

    @pl.when(s != 0)
    def _():
        xbuf[0:_CC_HALO, :] = xbuf[t:t + _CC_HALO, :]

    u = u_ref[0]
    xbuf[_CC_HALO:_CC_HALO + t, :] = u[:, :CONV_CH] * _sigmoid(u[:, CONV_CH:])
    w = w_ref[...]
    first = _CC_HALO - (CONV_WIDTH - 1)
    rows_kept = t + _CC_HALO - SUBLANE
    for res in range(1, SUBLANE):
        shifted[res - 1, 0:rows_kept, :] = xbuf[pl.ds(res, rows_kept), :]
    for r in range(t // _CC_ROWS):
        acc = jnp.broadcast_to(b_ref[...], (_CC_ROWS, CONV_CH))
        for j in range(CONV_WIDTH):
            res = (first + j) % SUBLANE
            base = r * _CC_ROWS + first + j - res
            tap = xbuf[base:base + _CC_ROWS, :] if res == 0 else shifted[res - 1, base:base + _CC_ROWS, :]
            acc = acc + w[j:j + 1] * tap
        mu = jnp.mean(acc, axis=-1, keepdims=True)
        var = jnp.mean(jnp.square(acc - mu), axis=-1, keepdims=True)
        hn = (acc - mu) * lax.rsqrt(var + EPS) * lg_ref[...] + lb_ref[...]
        o_ref[0, r * _CC_ROWS:(r + 1) * _CC_ROWS, :] = _silu(hn)


def _cconv(proj, dw_w, dw_b, ln_g, ln_b):
    bsz, seq, _ = proj.shape
    t = min(seq, 256)
    wu = 2 * CONV_CH
    vec = pl.BlockSpec((1, CONV_CH), lambda b, s: (0, 0))
    return pl.pallas_call(
        _cconv_kernel,
        grid=(bsz, seq // t),
        in_specs=[pl.BlockSpec((1, t, wu), lambda b, s: (b, s, COL_CU // wu)),
                  pl.BlockSpec((CONV_WIDTH, CONV_CH), lambda b, s: (0, 0)),
                  vec, vec, vec],
        out_specs=pl.BlockSpec((1, t, CONV_CH), lambda b, s: (b, s, 0)),
        out_shape=jax.ShapeDtypeStruct((bsz, seq, CONV_CH), F32),
        scratch_shapes=[pltpu.VMEM((t + _CC_HALO, CONV_CH), F32),
                        pltpu.VMEM((SUBLANE - 1, t + _CC_HALO, CONV_CH), F32)],
        compiler_params=_cparams(("parallel", "arbitrary")),
    )(proj, dw_w, dw_b, ln_g, ln_b)


_FFN_TILE = 256


def _ffn_kernel(x_ref, oa_ref, ob_ref, oc_ref, wo_ref, gmix_ref, gpre_ref, wup_ref, cw_ref, wd_ref, gpost_ref,
                o_ref, xn_ref, act_ref, stage_ref, carry_ref):
    tm = x_ref.shape[1]
    d_ff = wd_ref.shape[0]
    tf = _FFN_TILE
    hal = SUBLANE

    @pl.when(pl.program_id(1) == 0)
    def _():
        carry_ref[...] = jnp.zeros_like(carry_ref)

    mix = (jnp.dot(oa_ref[0].astype(BF16), wo_ref[:GDN_DIM], preferred_element_type=F32)
           + jnp.dot(ob_ref[0].astype(BF16), wo_ref[GDN_DIM:GDN_DIM + NSA_DIM], preferred_element_type=F32)
           + jnp.dot(oc_ref[0].astype(BF16), wo_ref[GDN_DIM + NSA_DIM:], preferred_element_type=F32))
    x = x_ref[0] + _rms(mix, gmix_ref[...])
    xn_ref[...] = _rms(x, gpre_ref[...]).astype(BF16)
    for f in range(d_ff // tf):
        ys = []
        for part in range(2):
            cols = slice(part * d_ff + f * tf, part * d_ff + (f + 1) * tf)
            h = jnp.dot(xn_ref[...], wup_ref[:, cols], preferred_element_type=F32)
            stage = stage_ref.at[f % 2, part]
            stage[0:hal, :] = carry_ref[:, cols]
            stage[hal:hal + tm, :] = h
            carry_ref[:, cols] = h[tm - hal:tm, :]
            cw = cw_ref[:, cols]
            ys.append(cw[2:3] * h + cw[1:2] * stage[pl.ds(hal - 1, tm), :]
                      + cw[0:1] * stage[pl.ds(hal - 2, tm), :])
        act_ref[:, f * tf:(f + 1) * tf] = (_silu(ys[0]) * ys[1]).astype(BF16)
    out = jnp.dot(act_ref[...], wd_ref[...], preferred_element_type=F32)
    o_ref[0] = x + _rms(out, gpost_ref[...])


def _outproj_ffn(x, o_a, o_b, o_c, w_out, g_mix, g_pre, w_up, conv_w, w_down, g_post):
    bsz, seq, d = x.shape
    d_ff = w_down.shape[0]
    tm = min(seq, 512)
    kw = conv_w.shape[0]
    resident = lambda shape: pl.BlockSpec(shape, lambda b, i: (0, 0), pipeline_mode=pl.Buffered(1))
    rows = lambda width: pl.BlockSpec((1, tm, width), lambda b, i: (b, i, 0))
    return pl.pallas_call(
        _ffn_kernel,
        grid=(bsz, seq // tm),
        in_specs=[rows(d), rows(GDN_DIM), rows(NSA_DIM), rows(CONV_CH),
                  resident((d, d)),
                  pl.BlockSpec((1, d), lambda b, i: (0, 0)),
                  pl.BlockSpec((1, d), lambda b, i: (0, 0)),
                  resident((d, 2 * d_ff)),
                  resident((kw, 2 * d_ff)),
                  resident((d_ff, d)),
                  pl.BlockSpec((1, d), lambda b, i: (0, 0))],
        out_specs=pl.BlockSpec((1, tm, d), lambda b, i: (b, i, 0)),
        out_shape=jax.ShapeDtypeStruct((bsz, seq, d), F32),
        scratch_shapes=[pltpu.VMEM((tm, d), BF16),
                        pltpu.VMEM((tm, d_ff), BF16),
                        pltpu.VMEM((2, 2, tm + SUBLANE, _FFN_TILE), F32),
                        pltpu.VMEM((SUBLANE, 2 * d_ff), F32)],
        compiler_params=_cparams(("parallel", "arbitrary")),
    )(x, o_a, o_b, o_c, w_out, g_mix, g_pre, w_up, conv_w, w_down, g_post)


def _pack_w_in(w_in):
    depth, d, _ = w_in.shape
    sizes = (GDN_DIM,) * 4 + (GDN_HEADS,) * 2 + (NSA_DIM,) + (LANE,) * 6 + (3 * NSA_HEADS, 2 * CONV_CH)
    offs = np.concatenate([[0], np.cumsum(sizes)])
    piece = lambda k: w_in[:, :, offs[k]:offs[k + 1]]
    zeros = lambda n: jnp.zeros((depth, d, n), w_in.dtype)
    gq, gk, gv, gz, ga, gb, nq, nkc, nvc, nks, nvs, nkw, nvw, ngate, cu = [piece(k) for k in range(15)]
    cols = [gq, gk, gv, gz, nq,
            ga, gb, zeros(LANE - 2 * GDN_HEADS),
            cu, nks, nvs, nkw, nvw,
            ngate, zeros(LANE - 3 * NSA_HEADS),
            nkc, nvc]
    packed = jnp.concatenate(cols, axis=-1)
    assert packed.shape[-1] == PROJ_DIM
    return packed.astype(BF16)


def _pack_compress(wk, wv, pe_k, pe_v):
    half = CMP_LEN // 2
    d = HEAD_DIM
    big = jnp.zeros((CMP_LEN, 4, d, 4, d), F32)
    for slot, w in enumerate((wk, wk, wv, wv)):
        big = big.at[:, slot, :, slot, :].set(w)
    big = big.reshape(CMP_LEN, 4 * d, 4 * d)
    wa = big[:half].reshape(half * 4 * d, 4 * d).astype(BF16)
    wb = big[half:].reshape(half * 4 * d, 4 * d).astype(BF16)
    pe = jnp.concatenate([pe_k, pe_k, pe_v, pe_v], axis=-1)
    pea = pe[:half].reshape(1, half * 4 * d)
    peb = pe[half:].reshape(1, half * 4 * d)
    return wa, wb, pea, peb


def kernel(x, positions, norm_mix_pre, norm_mix_post, norm_ffn_pre, norm_ffn_post, w_in, w_out, gdn_conv_w, gdn_a_log, gdn_dt_bias, gdn_norm_g, nsa_cmp_wk, nsa_cmp_wv, nsa_cmp_pe_k, nsa_cmp_pe_v, cc_dw_w, cc_dw_b, cc_ln_g, cc_ln_b, ffn_w_up, ffn_conv_w, ffn_w_down):
    bsz, seq, d = x.shape
    depth = w_in.shape[0]
    assert seq % Q_BLOCK == 0 and d == GDN_DIM + NSA_DIM + CONV_CH
    m = bsz * seq

    cos, sin = _rope_tables(positions)
    ncp = seq // CMP_STRIDE
    pad_rows = lambda t: jnp.pad(t[:, CMP_LEN - 1::CMP_STRIDE], ((0, 0), (0, 1), (0, 0)))
    cos_c, sin_c = pad_rows(cos), pad_rows(sin)

    w_in_p = _pack_w_in(w_in)
    w_out_b = w_out.astype(BF16)
    w_up_b = ffn_w_up.astype(BF16)
    w_down_b = ffn_w_down.astype(BF16)
    head_params = jnp.zeros((depth, SUBLANE, LANE), F32)
    head_params = head_params.at[:, 0, :GDN_HEADS].set(gdn_a_log).at[:, 1, :GDN_HEADS].set(gdn_dt_bias)

    cos2d = cos.reshape(m, LANE)
    sin2d = sin.reshape(m, LANE)
    for l in range(depth):
        proj, cmp_cols = _inproj(x.reshape(m, d), norm_mix_pre[l][None], w_in_p[l], cos2d, sin2d)
        proj = proj.reshape(bsz, seq, PROJ_OUT)
        o_a = _gdn(proj, gdn_conv_w[l], head_params[l], jnp.tile(gdn_norm_g[l], GDN_HEADS)[None])
        wa, wb, pea, peb = _pack_compress(nsa_cmp_wk[l], nsa_cmp_wv[l], nsa_cmp_pe_k[l], nsa_cmp_pe_v[l])
        kvc = _compress(cmp_cols.reshape(bsz, ncp, CMP_STRIDE * 2 * LANE), wa, wb, pea, peb, cos_c, sin_c)
        o_b = _nsa_attention(proj, kvc)
        o_c = _cconv(proj, cc_dw_w[l], cc_dw_b[l][None], cc_ln_g[l][None], cc_ln_b[l][None])
        x = _outproj_ffn(x, o_a, o_b, o_c, w_out_b[l], norm_mix_post[l][None], norm_ffn_pre[l][None],
                         w_up_b[l], ffn_conv_w[l], w_down_b[l], norm_ffn_post[l][None])
    return x
```

```python
import functools

import jax
import jax.numpy as jnp
import numpy as np
from jax import lax
from jax.experimental import pallas as pl
from jax.experimental.pallas import tpu as pltpu

F32 = jnp.float32
BF16 = jnp.bfloat16

HEAD_DIM = 64
GDN_HEADS = 6
GDN_DIM = GDN_HEADS * HEAD_DIM
GDN_CONV = 4
GDN_CHUNK = 64
NSA_HEADS = 6
NSA_KV_HEADS = 2
NSA_GROUP = NSA_HEADS // NSA_KV_HEADS
NSA_DIM = NSA_HEADS * HEAD_DIM
CMP_STRIDE = 16
CMP_LEN = 32
SEL_BLOCK = 64
SEL_TOPK = 8
WINDOW = 512
Q_BLOCK = 128
CONV_CH = 256
CONV_WIDTH = 31
ROPE_THETA = 10000.0
EPS = 1e-6
NEG = -1e30
FORCE = 1e4

LANE = 128
SUBLANE = 8
VMEM_LIMIT = 56 * 1024 * 1024

COL_QKV = 0
COL_Z = 1152
COL_NQ = 1536
COL_GAB = 1920
COL_CU = 2048
COL_SEL = 2560
COL_WIN = 2816
COL_GATE = 3072
PROJ_OUT = 3200
COL_CMP = 3200
PROJ_DIM = 3456


def _cparams(sem):
    return pltpu.CompilerParams(dimension_semantics=sem, vmem_limit_bytes=VMEM_LIMIT)


def _sigmoid(x):
    return 1.0 / (1.0 + jnp.exp(-x))


def _silu(x):
    return x * _sigmoid(x)


def _dot(a, b):
    return jnp.dot(a.astype(BF16), b.astype(BF16), preferred_element_type=F32)


def _dot_nt(a, b):
    return lax.dot_general(a.astype(BF16), b.astype(BF16), (((1,), (1,)), ((), ())),
                           preferred_element_type=F32)


def _dot_split3_rhs(sel, x):
    hi = x.astype(BF16)
    r1 = x - hi.astype(F32)
    mid = r1.astype(BF16)
    lo = (r1 - mid.astype(F32)).astype(BF16)
    return (jnp.dot(sel, hi, preferred_element_type=F32) + jnp.dot(sel, mid, preferred_element_type=F32)
            + jnp.dot(sel, lo, preferred_element_type=F32))


def _dot_split(x, sel):
    hi = x.astype(BF16)
    lo = (x - hi.astype(F32)).astype(BF16)
    return (jnp.dot(hi, sel, preferred_element_type=F32) + jnp.dot(lo, sel, preferred_element_type=F32))


def _rms(x, g):
    return x * lax.rsqrt(jnp.mean(x * x, axis=-1, keepdims=True) + EPS) * g


def _rope_table_kernel(pos_ref, inv_ref, sign_ref, cos_ref, sin_ref):
    ang = pos_ref[0].astype(F32) * inv_ref[...]
    cos_ref[0] = jnp.cos(ang)
    sin_ref[0] = jnp.sin(ang) * sign_ref[...]


def _rope_tables(positions):
    bsz, seq = positions.shape
    t = min(seq, 512)
    inv = 1.0 / (ROPE_THETA ** (jnp.arange(0, HEAD_DIM, 2, dtype=F32) / HEAD_DIM))
    inv = jnp.tile(inv, LANE // (HEAD_DIM // 2))[None, :]
    sign = jnp.tile(jnp.concatenate([-jnp.ones(HEAD_DIM // 2, F32), jnp.ones(HEAD_DIM // 2, F32)]),
                    LANE // HEAD_DIM)[None, :]
    out = jax.ShapeDtypeStruct((bsz, seq, LANE), F32)
    return pl.pallas_call(
        _rope_table_kernel,
        grid=(bsz, seq // t),
        in_specs=[pl.BlockSpec((1, t, 1), lambda b, s: (b, s, 0)),
                  pl.BlockSpec((1, LANE), lambda b, s: (0, 0)),
                  pl.BlockSpec((1, LANE), lambda b, s: (0, 0))],
        out_specs=[pl.BlockSpec((1, t, LANE), lambda b, s: (b, s, 0))] * 2,
        out_shape=[out, out],
        compiler_params=_cparams(("parallel", "parallel")),
    )(positions[:, :, None], inv, sign)


def _rope(x, cos, sin):
    lane = lax.broadcasted_iota(jnp.int32, x.shape, 1)
    first_half = (lane % HEAD_DIM) < (HEAD_DIM // 2)
    partner = jnp.where(first_half, pltpu.roll(x, LANE - HEAD_DIM // 2, 1), pltpu.roll(x, HEAD_DIM // 2, 1))
    return x * cos + partner * sin


def _inproj_kernel(x_ref, g_ref, w_ref, cos_ref, sin_ref, o_ref, cmp_ref):
    xn = _rms(x_ref[...], g_ref[...]).astype(BF16)
    y = jnp.dot(xn, w_ref[...], preferred_element_type=F32)
    cos = cos_ref[...]
    sin = sin_ref[...]
    scale = HEAD_DIM ** -0.5
    o_ref[:, :COL_NQ] = y[:, :COL_NQ]
    for c in range(COL_NQ, COL_NQ + NSA_DIM, LANE):
        o_ref[:, c:c + LANE] = _rope(y[:, c:c + LANE], cos, sin) * scale
    o_ref[:, COL_GAB:COL_SEL] = y[:, COL_GAB:COL_SEL]
    for c in (COL_SEL, COL_WIN):
        o_ref[:, c:c + LANE] = _rope(y[:, c:c + LANE], cos, sin)
        o_ref[:, c + LANE:c + 2 * LANE] = y[:, c + LANE:c + 2 * LANE]
    o_ref[:, COL_GATE:PROJ_OUT] = y[:, COL_GATE:PROJ_OUT]
    cmp_ref[...] = y[:, COL_CMP:]


def _inproj(x2d, g, w, cos2d, sin2d):
    m, d = x2d.shape
    n = w.shape[1]
    tm = min(m, 512)
    row = lambda width: pl.BlockSpec((tm, width), lambda i: (i, 0))
    return pl.pallas_call(
        _inproj_kernel,
        grid=(m // tm,),
        in_specs=[row(d),
                  pl.BlockSpec((1, d), lambda i: (0, 0)),
                  pl.BlockSpec((d, n), lambda i: (0, 0), pipeline_mode=pl.Buffered(1)),
                  row(LANE), row(LANE)],
        out_specs=[row(PROJ_OUT), row(n - PROJ_OUT)],
        out_shape=[jax.ShapeDtypeStruct((m, PROJ_OUT), F32), jax.ShapeDtypeStruct((m, n - PROJ_OUT), F32)],
        compiler_params=_cparams(("parallel",)),
    )(x2d, g, w, cos2d, sin2d)


_GDN_PAIR = 2 * GDN_CHUNK


def _gdn_kernel(qkv_ref, z_ref, gab_ref, cw_ref, hp_ref, ng_ref, o_ref, xbuf, state):
    c = GDN_CHUNK
    d = HEAD_DIM
    pr = _GDN_PAIR
    t = qkv_ref.shape[1]
    s = pl.program_id(1)

    @pl.when(s == 0)
    def _():
        xbuf[0:SUBLANE, :] = jnp.zeros((SUBLANE, 3 * GDN_DIM), F32)
        state[...] = jnp.zeros_like(state)

    @pl.when(s != 0)
    def _():
        xbuf[0:SUBLANE, :] = xbuf[t:t + SUBLANE, :]

    xbuf[SUBLANE:SUBLANE + t, :] = qkv_ref[0]
    cw = cw_ref[...]
    hp = hp_ref[...]

    lane_h = lax.broadcasted_iota(jnp.int32, (LANE, GDN_DIM), 0)
    col_h = lax.broadcasted_iota(jnp.int32, (LANE, GDN_DIM), 1) // d
    expand_a = (lane_h == col_h).astype(BF16)
    expand_b = (lane_h == col_h + GDN_HEADS).astype(BF16)
    same_head = (lax.broadcasted_iota(jnp.int32, (GDN_DIM, GDN_DIM), 0) // d
                 == lax.broadcasted_iota(jnp.int32, (GDN_DIM, GDN_DIM), 1) // d).astype(BF16)
    in_chunk = lax.broadcasted_iota(jnp.int32, (pr, LANE), 0) % c
    first = (lax.broadcasted_iota(jnp.int32, (pr, GDN_DIM), 1) % LANE) < d
    a = {}

    def prep_conv(p):
        r0 = SUBLANE + p * pr
        y = cw[GDN_CONV - 1:GDN_CONV] * xbuf[pl.ds(r0, pr), :]
        for j in range(GDN_CONV - 2, -1, -1):
            y = y + cw[j:j + 1] * xbuf[pl.ds(r0 - (GDN_CONV - 1) + j, pr), :]
        a[p] = {"y": _silu(y)}

    def prep_gates(p):
        gab = gab_ref[0, p * pr:(p + 1) * pr, :]
        sp_in = gab + hp[1:2]
        softplus = jnp.maximum(sp_in, 0.0) + jnp.log(1.0 + jnp.exp(-jnp.abs(sp_in)))
        gcum = -jnp.exp(hp[0:1]) * softplus
        shift = 1
        while shift < c:
            gcum = gcum + jnp.where(in_chunk >= shift, pltpu.roll(gcum, shift, 0), 0.0)
            shift *= 2
        g_last = jnp.concatenate(
            [jnp.broadcast_to(gcum[(i + 1) * c - 1:(i + 1) * c, :], (c, LANE)) for i in range(pr // c)], axis=0)
        eg = jnp.exp(gcum)
        a[p].update(gcum=gcum, gcum_t=gcum.T, eg=eg,
                    beta_e=_dot_split(_sigmoid(gab), expand_b),
                    eg_e=_dot_split(eg, expand_a),
                    kdec_e=_dot_split(jnp.exp(g_last - gcum), expand_a))

    def prep_qkv(p):
        ap = a[p]
        y = ap.pop("y")
        q = y[:, :GDN_DIM]
        k = y[:, GDN_DIM:2 * GDN_DIM]
        v = y[:, 2 * GDN_DIM:]
        q = q * (lax.rsqrt(_dot_split(q * q, same_head) + EPS) * (d ** -0.5))
        k = k * lax.rsqrt(_dot_split(k * k, same_head) + EPS)
        kbeta = k * ap["beta_e"]
        ap.update(k16=k.astype(BF16),
                  q16=[jnp.where(first, q, 0.0).astype(BF16), jnp.where(first, 0.0, q).astype(BF16)],
                  kb16=[jnp.where(first, kbeta, 0.0).astype(BF16), jnp.where(first, 0.0, kbeta).astype(BF16)],
                  vb=v * ap["beta_e"], kbe=kbeta * ap["eg_e"],
                  qe16=(q * ap["eg_e"]).astype(BF16), kd16=(k * ap["kdec_e"]).astype(BF16))

    row = lax.broadcasted_iota(jnp.int32, (pr, pr), 0)
    col = lax.broadcasted_iota(jnp.int32, (pr, pr), 1)
    same_chunk = (row // c) == (col // c)
    tri = same_chunk & (row >= col)
    tri_strict = same_chunk & (row > col)
    low = lax.broadcasted_iota(jnp.int32, (pr, LANE), 1) < d
    low_c = lax.broadcasted_iota(jnp.int32, (c, LANE), 1) < d
    n_pair = t // pr
    heads = range(GDN_HEADS)
    per = pr // c
    lanes_of = lambda h: slice((h // 2) * LANE, (h // 2 + 1) * LANE)
    sol, pw, qk = {}, {}, {}
    st = [state[h] for h in heads]
    zeros = jnp.zeros((c, LANE), BF16)
    ng = ng_ref[...]

    def setup_steps(p):
        def step(h):
            ap = a[p]
            g = lanes_of(h)
            decay = jnp.exp(jnp.where(tri, ap["gcum"][:, h:h + 1] - ap["gcum_t"][h:h + 1, :], NEG))
            k2 = ap["k16"][:, g]
            kk = _dot_nt(ap["kb16"][h % 2][:, g], k2)
            pw[h, p] = jnp.where(tri_strict, -(kk * decay), 0.0).astype(BF16)
            qk[h, p] = (_dot_nt(ap["q16"][h % 2][:, g], k2) * decay).astype(BF16)
            ke_sw = pltpu.roll(ap["kbe"][:, g], d, 1)
            vb = ap["vb"][:, g]
            sol[h, p] = jnp.where(low, vb, ke_sw) if h % 2 == 0 else jnp.where(low, ke_sw, vb)

        return ([functools.partial(fn, p) for fn in (prep_conv, prep_gates, prep_qkv)]
                + [functools.partial(step, h) for h in heads])

    def solve_steps(p):
        def apply():
            for h in heads:
                sol[h, p] = sol[h, p] + _dot(pw[h, p], sol[h, p])

        def square():
            for h in heads:
                pw[h, p] = _dot(pw[h, p], pw[h, p]).astype(BF16)

        return [apply] + [square, apply] * 5

    def finish_steps(p):
        ktuw, s_before, v_new, q_st = {}, {}, {}, {}
        rows = lambda ic: slice(ic * c, (ic + 1) * c)

        def operators(ic):
            for h in heads:
                full = lax.dot_general(a[p]["kd16"][rows(ic), lanes_of(h)], sol[h, p][rows(ic)].astype(BF16),
                                       (((0,), (0,)), ((), ())), preferred_element_type=F32)
                ktuw[h, ic] = full[:c] if h % 2 == 0 else full[c:]

        def recur(ic):
            last = (ic + 1) * c - 1
            for h in heads:
                even = h % 2 == 0
                s16 = st[h].astype(BF16)
                s_before[h, ic] = s16
                s_w = jnp.concatenate([zeros, s16] if even else [s16, zeros], axis=0)
                el = a[p]["eg"][last:last + 1, h:h + 1]
                st[h] = jnp.where(low_c if even else ~low_c,
                                  st[h] * el + ktuw[h, ic] - _dot(ktuw[h, ic], s_w), 0.0)

        def read_state(ic):
            for h in heads:
                even = h % 2 == 0
                s16 = s_before[h, ic]
                s_w = jnp.concatenate([zeros, s16] if even else [s16, zeros], axis=0)
                s_q = jnp.concatenate([s16, zeros] if even else [zeros, s16], axis=0)
                sol_c = sol[h, p][rows(ic)]
                v_new[h, ic] = sol_c - _dot(sol_c, s_w)
                q_st[h, ic] = _dot(a[p]["qe16"][rows(ic), lanes_of(h)], s_q)

        def outputs():
            r = slice(p * pr, (p + 1) * pr)
            groups = []
            for j in range(GDN_HEADS // 2):
                halves = []
                for h in (2 * j, 2 * j + 1):
                    vn_pair = jnp.concatenate([v_new[h, ic] for ic in range(per)], axis=0)
                    qs_pair = jnp.concatenate([q_st[h, ic] for ic in range(per)], axis=0)
                    halves.append(qs_pair + _dot(qk[h, p], vn_pair))
                groups.append(jnp.where(low, halves[0], halves[1]))
            o = jnp.concatenate(groups, axis=-1)
            ms = _dot_split(o * o, same_head) * (1.0 / d)
            o_ref[0, r, :] = o * lax.rsqrt(ms + EPS) * ng * _silu(z_ref[0, r, :])

        return ([functools.partial(operators, ic) for ic in range(per)]
                + [functools.partial(recur, ic) for ic in range(per)]
                + [functools.partial(read_state, ic) for ic in range(per)] + [outputs])

    for slot in range(n_pair + 2):
        lists = [steps(q) for steps, q in ((solve_steps, slot - 1), (setup_steps, slot), (finish_steps, slot - 2))
                 if 0 <= q < n_pair]
        merged = sorted(((i + 0.5) / len(lst), k, i, fn) for k, lst in enumerate(lists) for i, fn in enumerate(lst))
        for _, _, _, fn in merged:
            fn()
    for h in heads:
        state[h] = st[h]


def _gdn(proj, conv_w, head_params, norm_g):
    bsz, seq, _ = proj.shape
    t = min(seq, 512)
    w_qkv = 3 * GDN_DIM
    return pl.pallas_call(
        _gdn_kernel,
        grid=(bsz, seq // t),
        in_specs=[pl.BlockSpec((1, t, w_qkv), lambda b, s: (b, s, COL_QKV // w_qkv)),
                  pl.BlockSpec((1, t, GDN_DIM), lambda b, s: (b, s, COL_Z // GDN_DIM)),
                  pl.BlockSpec((1, t, LANE), lambda b, s: (b, s, COL_GAB // LANE)),
                  pl.BlockSpec((GDN_CONV, w_qkv), lambda b, s: (0, 0)),
                  pl.BlockSpec((SUBLANE, LANE), lambda b, s: (0, 0)),
                  pl.BlockSpec((1, GDN_DIM), lambda b, s: (0, 0))],
        out_specs=pl.BlockSpec((1, t, GDN_DIM), lambda b, s: (b, s, 0)),
        out_shape=jax.ShapeDtypeStruct((bsz, seq, GDN_DIM), F32),
        scratch_shapes=[pltpu.VMEM((t + SUBLANE, w_qkv), F32),
                        pltpu.VMEM((GDN_HEADS, HEAD_DIM, LANE), F32)],
        compiler_params=_cparams(("parallel", "arbitrary")),
    )(proj, proj, proj, conv_w, head_params, norm_g)


def _cmp_kernel(c_ref, wa_ref, wb_ref, pea_ref, peb_ref, cos_ref, sin_ref, o_ref):
    cb = c_ref[0]
    ya = _dot(cb + pea_ref[...], wa_ref[...])
    yb = _dot(cb + peb_ref[...], wb_ref[...])
    n = ya.shape[0]
    y = ya + pltpu.roll(yb, n - 1, 0)
    kc = _rope(y[:, :LANE], cos_ref[0], sin_ref[0])
    o_ref[0] = jnp.concatenate([kc, y[:, LANE:]], axis=-1)


def _compress(cmp_rows, wa, wb, pea, peb, cos_c, sin_c):
    bsz, n, width = cmp_rows.shape
    return pl.pallas_call(
        _cmp_kernel,
        grid=(bsz,),
        in_specs=[pl.BlockSpec((1, n, width), lambda b: (b, 0, 0)),
                  pl.BlockSpec(wa.shape, lambda b: (0, 0)),
                  pl.BlockSpec(wb.shape, lambda b: (0, 0)),
                  pl.BlockSpec((1, width), lambda b: (0, 0)),
                  pl.BlockSpec((1, width), lambda b: (0, 0)),
                  pl.BlockSpec((1, n, LANE), lambda b: (b, 0, 0)),
                  pl.BlockSpec((1, n, LANE), lambda b: (b, 0, 0))],
        out_specs=pl.BlockSpec((1, n, 2 * LANE), lambda b: (b, 0, 0)),
        out_shape=jax.ShapeDtypeStruct((bsz, n, 2 * LANE), F32),
        compiler_params=_cparams(("parallel",)),
    )(cmp_rows, wa, wb, pea, peb, cos_c, sin_c)


_NSA_KEY_BLOCK = 512


def _attend(k16, v16, q2, bias):
    d = HEAD_DIM
    half = q2.shape[1] // NSA_KV_HEADS
    s = jnp.dot(k16, q2, preferred_element_type=F32) + bias
    m = jnp.max(s, axis=0, keepdims=True)
    p = jnp.exp(s - m)
    l = jnp.sum(p, axis=0, keepdims=True)
    pv = lax.dot_general(v16, p.astype(BF16), (((0,), (0,)), ((), ())), preferred_element_type=F32)
    return m, l, jnp.concatenate([pv[:d, :half], pv[d:, half:]], axis=1), p


def _nsa_kernel(q_ref, gate_ref, kvc_ref, ksel_ref, kwin_ref, o_ref, mask_ref, *, seq):
    qb = Q_BLOCK
    d = HEAD_DIM
    grp = NSA_GROUP
    cols = grp * qb
    n_slc = seq // SEL_BLOCK
    topk = min(SEL_TOPK, n_slc)
    ncp = seq // CMP_STRIDE
    qi = pl.program_id(1)
    q_t = q_ref[0].T
    gate_t = _sigmoid(gate_ref[0]).T
    t_q = qi * qb + lax.broadcasted_iota(jnp.int32, (1, qb), 1)

    heads = NSA_KV_HEADS * grp
    tile_all = lambda a: jnp.concatenate([a] * heads, axis=1)
    n_idx = lax.broadcasted_iota(jnp.int32, (ncp, qb), 0)
    cmp_bias = tile_all(jnp.where((n_idx * CMP_STRIDE + (CMP_LEN - 1) <= t_q) & (n_idx < ncp - 1), 0.0, NEG))
    sj = lax.broadcasted_iota(jnp.int32, (n_slc, ncp), 0) * SEL_BLOCK
    ci = lax.broadcasted_iota(jnp.int32, (n_slc, ncp), 1) * CMP_STRIDE
    overlap_t = ((ci < sj + SEL_BLOCK) & (ci + CMP_LEN > sj) & (ci < (ncp - 1) * CMP_STRIDE)).astype(BF16)
    blk = lax.broadcasted_iota(jnp.int32, (n_slc, qb), 0)
    blk_f = blk.astype(F32)
    cur = t_q // SEL_BLOCK
    forced = (blk == 0) | (blk == cur) | (blk == cur - 1)
    kb = min(seq, _NSA_KEY_BLOCK)
    span = min(seq, WINDOW + qb)
    causal_bias = jnp.where(lax.broadcasted_iota(jnp.int32, (qb, qb), 0)
                            <= lax.broadcasted_iota(jnp.int32, (qb, qb), 1), 0.0, NEG)
    w0 = pl.multiple_of(jnp.maximum(qi * qb + qb - span, 0), qb)
    diff = t_q - (w0 + lax.broadcasted_iota(jnp.int32, (span, qb), 0))
    win_bias = tile_all(jnp.where((diff >= 0) & (diff < WINDOW), 0.0, NEG))
    kvc = kvc_ref[0]

    zeros_q = jnp.zeros((d, cols), BF16)
    q_h = [jnp.concatenate([q_t[(h * grp + g) * d:(h * grp + g + 1) * d, :] for g in range(grp)],
                           axis=1).astype(BF16) for h in range(NSA_KV_HEADS)]
    q2 = jnp.concatenate([jnp.concatenate([q_h[0], zeros_q], axis=0),
                          jnp.concatenate([zeros_q, q_h[1]], axis=0)], axis=1)

    m_c, l_c, o_c, p_c = _attend(kvc[:, :LANE].astype(BF16), kvc[:, LANE:].astype(BF16), q2, cmp_bias)
    inv_l = jnp.where(m_c > 0.5 * NEG, 1.0 / l_c, 0.0)
    o_c = o_c * inv_l
    p_c = p_c * inv_l

    kv = kwin_ref[0, pl.ds(w0, span), :]
    _, l_w, o_w, _ = _attend(kv[:, :LANE].astype(BF16), kv[:, LANE:].astype(BF16), q2, win_bias)
    o_w = o_w * (1.0 / l_w)

    for h in range(NSA_KV_HEADS):
        p_sum = p_c[:, h * cols:h * cols + qb]
        for g in range(1, grp):
            p_sum = p_sum + p_c[:, h * cols + g * qb:h * cols + (g + 1) * qb]
        imp = _dot_split3_rhs(overlap_t, p_sum)
        work = jnp.where(forced, FORCE, jnp.where(blk * SEL_BLOCK <= t_q, imp, -1.0))
        sel = jnp.zeros((n_slc, qb), F32)
        for _ in range(topk):
            best = jnp.max(work, axis=0, keepdims=True)
            idx = jnp.min(jnp.where(work == best, blk_f, 1e9), axis=0, keepdims=True)
            pick = blk_f == idx
            sel = jnp.where(pick, 1.0, sel)
            work = jnp.where(pick, -3.0, work)
        sel_bias = jnp.where((sel > 0.5) & (blk * SEL_BLOCK <= t_q), 0.0, NEG)
        for b in range(n_slc):
            mask_ref[b * SEL_BLOCK:(b + 1) * SEL_BLOCK, h * qb:(h + 1) * qb] = jnp.broadcast_to(
                sel_bias[b:b + 1, :], (SEL_BLOCK, qb))
        own_keys = pl.ds(pl.multiple_of(qi * qb, qb), qb)
        mask_ref[own_keys, h * qb:(h + 1) * qb] = mask_ref[own_keys, h * qb:(h + 1) * qb] + causal_bias

    def sel_body(j, carry):
        m, l, acc = carry
        off = pl.multiple_of(j * kb, kb)
        kv = ksel_ref[0, pl.ds(off, kb), :]
        mask = mask_ref[pl.ds(off, kb), :]
        bias = jnp.concatenate([mask[:, :qb]] * grp + [mask[:, qb:]] * grp, axis=1)
        m_b, l_b, pv_b, _ = _attend(kv[:, :LANE].astype(BF16), kv[:, LANE:].astype(BF16), q2, bias)
        m_new = jnp.maximum(m, m_b)
        w_old = jnp.exp(m - m_new)
        w_blk = jnp.exp(m_b - m_new)
        return m_new, w_old * l + w_blk * l_b, w_old * acc + w_blk * pv_b

    init = (jnp.full((1, 2 * cols), NEG, F32), jnp.zeros((1, 2 * cols), F32), jnp.zeros((d, 2 * cols), F32))
    _, l_s, acc_s = lax.fori_loop(0, (qi * qb + qb + kb - 1) // kb, sel_body, init)
    o_s = acc_s * (1.0 / l_s)

    out_rows = []
    for hd in range(heads):
        cs = slice(hd * qb, (hd + 1) * qb)
        out_rows.append(gate_t[3 * hd:3 * hd + 1] * o_c[:, cs] + gate_t[3 * hd + 1:3 * hd + 2] * o_s[:, cs]
                        + gate_t[3 * hd + 2:3 * hd + 3] * o_w[:, cs])
    o_ref[0] = jnp.concatenate(out_rows, axis=0).T


def _nsa_attention(proj, kvc):
    bsz, seq, _ = proj.shape
    qb = Q_BLOCK
    ncp = kvc.shape[1]
    kv = 2 * LANE
    return pl.pallas_call(
        functools.partial(_nsa_kernel, seq=seq),
        grid=(bsz, seq // qb),
        in_specs=[pl.BlockSpec((1, qb, NSA_DIM), lambda b, i: (b, i, COL_NQ // NSA_DIM)),
                  pl.BlockSpec((1, qb, LANE), lambda b, i: (b, i, COL_GATE // LANE)),
                  pl.BlockSpec((1, ncp, kv), lambda b, i: (b, 0, 0)),
                  pl.BlockSpec((1, seq, kv), lambda b, i: (b, 0, COL_SEL // kv)),
                  pl.BlockSpec((1, seq, kv), lambda b, i: (b, 0, COL_WIN // kv))],
        out_specs=pl.BlockSpec((1, qb, NSA_DIM), lambda b, i: (b, i, 0)),
        out_shape=jax.ShapeDtypeStruct((bsz, seq, NSA_DIM), F32),
        scratch_shapes=[pltpu.VMEM((seq, NSA_KV_HEADS * qb), F32)],
        compiler_params=_cparams(("parallel", "arbitrary")),
    )(proj, proj, kvc, proj, proj)


_CC_HALO = 32
_CC_ROWS = 64


def _cconv_kernel(u_ref, w_ref, b_ref, lg_ref, lb_ref, o_ref, xbuf, shifted):
    t = u_ref.shape[1]
    s = pl.program_id(1)

    @pl.when(s == 0)
    def _():
        xbuf[0:_CC_HALO, :] = jnp.zeros((_CC_HALO, CONV_CH), F32)

    @pl.when(s != 0)
    def _():
        xbuf[0:_CC_HALO, :] = xbuf[t:t + _CC_HALO, :]

    u = u_ref[0]
    xbuf[_CC_HALO:_CC_HALO + t, :] = u[:, :CONV_CH] * _sigmoid(u[:, CONV_CH:])
    w = w_ref[...]
    first = _CC_HALO - (CONV_WIDTH - 1)
    rows_kept = t + _CC_HALO - SUBLANE
    for res in range(1, SUBLANE):
        shifted[res - 1, 0:rows_kept, :] = xbuf[pl.ds(res, rows_kept), :]
    for r in range(t // _CC_ROWS):
        acc = jnp.broadcast_to(b_ref[...], (_CC_ROWS, CONV_CH))
        for j in range(CONV_WIDTH):
            res = (first + j) % SUBLANE
            base = r * _CC_ROWS + first + j - res
            tap = xbuf[base:base + _CC_ROWS, :] if res == 0 else shifted[res - 1, base:base + _CC_ROWS, :]
            acc = acc + w[j:j + 1] * tap
        mu = jnp.mean(acc, axis=-1, keepdims=True)
        var = jnp.mean(jnp.square(acc - mu), axis=-1, keepdims=True)
        hn = (acc - mu) * lax.rsqrt(var + EPS) * lg_ref[...] + lb_ref[...]
        o_ref[0, r * _CC_ROWS:(r + 1) * _CC_ROWS, :] = _silu(hn)


def _cconv(proj, dw_w, dw_b, ln_g, ln_b):
    bsz, seq, _ = proj.shape
    t = min(seq, 256)
    wu = 2 * CONV_CH
    vec = pl.BlockSpec((1, CONV_CH), lambda b, s: (0, 0))
    return pl.pallas_call(
        _cconv_kernel,
        grid=(bsz, seq // t),
        in_specs=[pl.BlockSpec((1, t, wu), lambda b, s: (b, s, COL_CU // wu)),
                  pl.BlockSpec((CONV_WIDTH, CONV_CH), lambda b, s: (0, 0)),
                  vec, vec, vec],
        out_specs=pl.BlockSpec((1, t, CONV_CH), lambda b, s: (b, s, 0)),
        out_shape=jax.ShapeDtypeStruct((bsz, seq, CONV_CH), F32),
        scratch_shapes=[pltpu.VMEM((t + _CC_HALO, CONV_CH), F32),
                        pltpu.VMEM((SUBLANE - 1, t + _CC_HALO, CONV_CH), F32)],
        compiler_params=_cparams(("parallel", "arbitrary")),
    )(proj, dw_w, dw_b, ln_g, ln_b)


_FFN_TILE = 256


def _ffn_kernel(x_ref, oa_ref, ob_ref, oc_ref, wo_ref, gmix_ref, gpre_ref, wup_ref, cw_ref, wd_ref, gpost_ref,
                o_ref, xn_ref, act_ref, stage_ref, carry_ref):
    tm = x_ref.shape[1]
    d_ff = wd_ref.shape[0]
    tf = _FFN_TILE
    hal = SUBLANE

    @pl.when(pl.program_id(1) == 0)
    def _():
        carry_ref[...] = jnp.zeros_like(carry_ref)

    mix = (jnp.dot(oa_ref[0].astype(BF16), wo_ref[:GDN_DIM], preferred_element_type=F32)
           + jnp.dot(ob_ref[0].astype(BF16), wo_ref[GDN_DIM:GDN_DIM + NSA_DIM], preferred_element_type=F32)
           + jnp.dot(oc_ref[0].astype(BF16), wo_ref[GDN_DIM + NSA_DIM:], preferred_element_type=F32))
    x = x_ref[0] + _rms(mix, gmix_ref[...])
    xn_ref[...] = _rms(x, gpre_ref[...]).astype(BF16)
    for f in range(d_ff // tf):
        ys = []
        for part in range(2):
            cols = slice(part * d_ff + f * tf, part * d_ff + (f + 1) * tf)
            h = jnp.dot(xn_ref[...], wup_ref[:, cols], preferred_element_type=F32)
            stage = stage_ref.at[f % 2, part]
            stage[0:hal, :] = carry_ref[:, cols]
            stage[hal:hal + tm, :] = h
            carry_ref[:, cols] = h[tm - hal:tm, :]
            cw = cw_ref[:, cols]
            ys.append(cw[2:3] * h + cw[1:2] * stage[pl.ds(hal - 1, tm), :]
                      + cw[0:1] * stage[pl.ds(hal - 2, tm), :])
        act_ref[:, f * tf:(f + 1) * tf] = (_silu(ys[0]) * ys[1]).astype(BF16)
    out = jnp.dot(act_ref[...], wd_ref[...], preferred_element_type=F32)
    o_ref[0] = x + _rms(out, gpost_ref[...])


def _outproj_ffn(x, o_a, o_b, o_c, w_out, g_mix, g_pre, w_up, conv_w, w_down, g_post):
    bsz, seq, d = x.shape
    d_ff = w_down.shape[0]
    tm = min(seq, 512)
    kw = conv_w.shape[0]
    resident = lambda shape: pl.BlockSpec(shape, lambda b, i: (0, 0), pipeline_mode=pl.Buffered(1))
    rows = lambda width: pl.BlockSpec((1, tm, width), lambda b, i: (b, i, 0))
    return pl.pallas_call(
        _ffn_kernel,
        grid=(bsz, seq // tm),
        in_specs=[rows(d), rows(GDN_DIM), rows(NSA_DIM), rows(CONV_CH),
                  resident((d, d)),
                  pl.BlockSpec((1, d), lambda b, i: (0, 0)),
                  pl.BlockSpec((1, d), lambda b, i: (0, 0)),
                  resident((d, 2 * d_ff)),
                  resident((kw, 2 * d_ff)),
                  resident((d_ff, d)),
                  pl.BlockSpec((1, d), lambda b, i: (0, 0))],
        out_specs=pl.BlockSpec((1, tm, d), lambda b, i: (b, i, 0)),
        out_shape=jax.ShapeDtypeStruct((bsz, seq, d), F32),
        scratch_shapes=[pltpu.VMEM((tm, d), BF16),
                        pltpu.VMEM((tm, d_ff), BF16),
                        pltpu.VMEM((2, 2, tm + SUBLANE, _FFN_TILE), F32),
                        pltpu.VMEM((SUBLANE, 2 * d_ff), F32)],
        compiler_params=_cparams(("parallel", "arbitrary")),
    )(x, o_a, o_b, o_c, w_out, g_mix, g_pre, w_up, conv_w, w_down, g_post)


def _pack_w_in(w_in):
    depth, d, _ = w_in.shape
    sizes = (GDN_DIM,) * 4 + (GDN_HEADS,) * 2 + (NSA_DIM,) + (LANE,) * 6 + (3 * NSA_HEADS, 2 * CONV_CH)
    offs = np.concatenate([[0], np.cumsum(sizes)])
    piece = lambda k: w_in[:, :, offs[k]:offs[k + 1]]
    zeros = lambda n: jnp.zeros((depth, d, n), w_in.dtype)
    gq, gk, gv, gz, ga, gb, nq, nkc, nvc, nks, nvs, nkw, nvw, ngate, cu = [piece(k) for k in range(15)]
    cols = [gq, gk, gv, gz, nq,
            ga, gb, zeros(LANE - 2 * GDN_HEADS),
            cu, nks, nvs, nkw, nvw,
            ngate, zeros(LANE - 3 * NSA_HEADS),
            nkc, nvc]
    packed = jnp.concatenate(cols, axis=-1)
    assert packed.shape[-1] == PROJ_DIM
    return packed.astype(BF16)


def _pack_compress(wk, wv, pe_k, pe_v):
    half = CMP_LEN // 2
    d = HEAD_DIM
    big = jnp.zeros((CMP_LEN, 4, d, 4, d), F32)
    for slot, w in enumerate((wk, wk, wv, wv)):
        big = big.at[:, slot, :, slot, :].set(w)
    big = big.reshape(CMP_LEN, 4 * d, 4 * d)
    wa = big[:half].reshape(half * 4 * d, 4 * d).astype(BF16)
    wb = big[half:].reshape(half * 4 * d, 4 * d).astype(BF16)
    pe = jnp.concatenate([pe_k, pe_k, pe_v, pe_v], axis=-1)
    pea = pe[:half].reshape(1, half * 4 * d)
    peb = pe[half:].reshape(1, half * 4 * d)
    return wa, wb, pea, peb


def kernel(x, positions, norm_mix_pre, norm_mix_post, norm_ffn_pre, norm_ffn_post, w_in, w_out, gdn_conv_w, gdn_a_log, gdn_dt_bias, gdn_norm_g, nsa_cmp_wk, nsa_cmp_wv, nsa_cmp_pe_k, nsa_cmp_pe_v, cc_dw_w, cc_dw_b, cc_ln_g, cc_ln_b, ffn_w_up, ffn_conv_w, ffn_w_down):
    bsz, seq, d = x.shape
    depth = w_in.shape[0]
    assert seq % Q_BLOCK == 0 and d == GDN_DIM + NSA_DIM + CONV_CH
    m = bsz * seq

    cos, sin = _rope_tables(positions)
    ncp = seq // CMP_STRIDE
    pad_rows = lambda t: jnp.pad(t[:, CMP_LEN - 1::CMP_STRIDE], ((0, 0), (0, 1), (0, 0)))
    cos_c, sin_c = pad_rows(cos), pad_rows(sin)

    w_in_p = _pack_w_in(w_in)
    w_out_b = w_out.astype(BF16)
    w_up_b = ffn_w_up.astype(BF16)
    w_down_b = ffn_w_down.astype(BF16)
    head_params = jnp.zeros((depth, SUBLANE, LANE), F32)
    head_params = head_params.at[:, 0, :GDN_HEADS].set(gdn_a_log).at[:, 1, :GDN_HEADS].set(gdn_dt_bias)

    cos2d = cos.reshape(m, LANE)
    sin2d = sin.reshape(m, LANE)
    for l in range(depth):
        proj, cmp_cols = _inproj(x.reshape(m, d), norm_mix_pre[l][None], w_in_p[l], cos2d, sin2d)
        proj = proj.reshape(bsz, seq, PROJ_OUT)
        o_a = _gdn(proj, gdn_conv_w[l], head_params[l], jnp.tile(gdn_norm_g[l], GDN_HEADS)[None])
        wa, wb, pea, peb = _pack_compress(nsa_cmp_wk[l], nsa_cmp_wv[l], nsa_cmp_pe_k[l], nsa_cmp_pe_v[l])
        kvc = _compress(cmp_cols.reshape(bsz, ncp, CMP_STRIDE * 2 * LANE), wa, wb, pea, peb, cos_c, sin_c)
        o_b = _nsa_attention(proj, kvc)
        o_c = _cconv(proj, cc_dw_w[l], cc_dw_b[l][None], cc_ln_g[l][None], cc_ln_b[l][None])
        x = _outproj_ffn(x, o_a, o_b, o_c, w_out_b[l], norm_mix_post[l][None], norm_ffn_pre[l][None],
                         w_up_b[l], ffn_conv_w[l], w_down_b[l], norm_ffn_post[l][None])
    return x
```

```python
import functools

import jax
import jax.numpy as jnp
import numpy as np
from jax import lax
from jax.experimental import pallas as pl
from jax.experimental.pallas import tpu as pltpu

F32 = jnp.float32
BF16 = jnp.bfloat16

HEAD_DIM = 64
GDN_HEADS = 6
GDN_DIM = GDN_HEADS * HEAD_DIM
GDN_CONV = 4
GDN_CHUNK = 64
NSA_HEADS = 6
NSA_KV_HEADS = 2
NSA_GROUP = NSA_HEADS // NSA_KV_HEADS
NSA_DIM = NSA_HEADS * HEAD_DIM
CMP_STRIDE = 16
CMP_LEN = 32
SEL_BLOCK = 64
SEL_TOPK = 8
WINDOW = 512
Q_BLOCK = 128
CONV_CH = 256
CONV_WIDTH = 31
ROPE_THETA = 10000.0
EPS = 1e-6
NEG = -1e30
FORCE = 1e4
_LOG2_E = 1.4426950408889634

LANE = 128
SUBLANE = 8
VMEM_LIMIT = 56 * 1024 * 1024

COL_QKV = 0
COL_Z = 1152
COL_NQ = 1536
COL_GAB = 1920
COL_CU = 2048
COL_SEL = 2560
COL_WIN = 2816
COL_GATE = 3072
PROJ_OUT = 3200
COL_CMP = 3200
PROJ_DIM = 3456


def _cparams(sem):
    return pltpu.CompilerParams(dimension_semantics=sem, vmem_limit_bytes=VMEM_LIMIT)


def _sigmoid(x):
    return 1.0 / (1.0 + jnp.exp(-x))


def _silu(x):
    return x * _sigmoid(x)


def _dot(a, b):
    return jnp.dot(a.astype(BF16), b.astype(BF16), preferred_element_type=F32)


def _dot_nt(a, b):
    return lax.dot_general(a.astype(BF16), b.astype(BF16), (((1,), (1,)), ((), ())),
                           preferred_element_type=F32)


def _dot_split3_rhs(sel, x):
    hi = x.astype(BF16)
    r1 = x - hi.astype(F32)
    mid = r1.astype(BF16)
    lo = (r1 - mid.astype(F32)).astype(BF16)
    return (jnp.dot(sel, hi, preferred_element_type=F32) + jnp.dot(sel, mid, preferred_element_type=F32)
            + jnp.dot(sel, lo, preferred_element_type=F32))


def _dot_split(x, sel):
    hi = x.astype(BF16)
    lo = (x - hi.astype(F32)).astype(BF16)
    return (jnp.dot(hi, sel, preferred_element_type=F32) + jnp.dot(lo, sel, preferred_element_type=F32))


def _rms(x, g):
    return x * lax.rsqrt(jnp.mean(x * x, axis=-1, keepdims=True) + EPS) * g


def _rope_table_kernel(pos_ref, inv_ref, sign_ref, cos_ref, sin_ref):
    ang = pos_ref[0].astype(F32) * inv_ref[...]
    cos_ref[0] = jnp.cos(ang)
    sin_ref[0] = jnp.sin(ang) * sign_ref[...]


def _rope_tables(positions):
    bsz, seq = positions.shape
    t = min(seq, 512)
    inv = 1.0 / (ROPE_THETA ** (jnp.arange(0, HEAD_DIM, 2, dtype=F32) / HEAD_DIM))
    inv = jnp.tile(inv, LANE // (HEAD_DIM // 2))[None, :]
    sign = jnp.tile(jnp.concatenate([-jnp.ones(HEAD_DIM // 2, F32), jnp.ones(HEAD_DIM // 2, F32)]),
                    LANE // HEAD_DIM)[None, :]
    out = jax.ShapeDtypeStruct((bsz, seq, LANE), F32)
    return pl.pallas_call(
        _rope_table_kernel,
        grid=(bsz, seq // t),
        in_specs=[pl.BlockSpec((1, t, 1), lambda b, s: (b, s, 0)),
                  pl.BlockSpec((1, LANE), lambda b, s: (0, 0)),
                  pl.BlockSpec((1, LANE), lambda b, s: (0, 0))],
        out_specs=[pl.BlockSpec((1, t, LANE), lambda b, s: (b, s, 0))] * 2,
        out_shape=[out, out],
        compiler_params=_cparams(("parallel", "parallel")),
    )(positions[:, :, None], inv, sign)


def _rope(x, cos, sin):
    lane = lax.broadcasted_iota(jnp.int32, x.shape, 1)
    first_half = (lane % HEAD_DIM) < (HEAD_DIM // 2)
    partner = jnp.where(first_half, pltpu.roll(x, LANE - HEAD_DIM // 2, 1), pltpu.roll(x, HEAD_DIM // 2, 1))
    return x * cos + partner * sin


def _inproj_kernel(x_ref, g_ref, w_ref, cos_ref, sin_ref, o_ref, cmp_ref):
    xn = _rms(x_ref[...], g_ref[...]).astype(BF16)
    y = jnp.dot(xn, w_ref[...], preferred_element_type=F32)
    cos = cos_ref[...]
    sin = sin_ref[...]
    scale = HEAD_DIM ** -0.5 * _LOG2_E
    o_ref[:, :COL_NQ] = y[:, :COL_NQ]
    for c in range(COL_NQ, COL_NQ + NSA_DIM, LANE):
        o_ref[:, c:c + LANE] = _rope(y[:, c:c + LANE], cos, sin) * scale
    o_ref[:, COL_GAB:COL_SEL] = y[:, COL_GAB:COL_SEL]
    for c in (COL_SEL, COL_WIN):
        o_ref[:, c:c + LANE] = _rope(y[:, c:c + LANE], cos, sin)
        o_ref[:, c + LANE:c + 2 * LANE] = y[:, c + LANE:c + 2 * LANE]
    o_ref[:, COL_GATE:PROJ_OUT] = y[:, COL_GATE:PROJ_OUT]
    cmp_ref[...] = y[:, COL_CMP:]


def _inproj(x2d, g, w_all, layer, cos2d, sin2d):
    m, d = x2d.shape
    n = w_all.shape[2]
    tm = min(m, 512)
    row = lambda width: pl.BlockSpec((tm, width), lambda i: (i, 0))
    return pl.pallas_call(
        _inproj_kernel,
        grid=(m // tm,),
        in_specs=[row(d),
                  pl.BlockSpec((1, d), lambda i: (0, 0)),
                  pl.BlockSpec((None, d, n), lambda i: (layer, 0, 0), pipeline_mode=pl.Buffered(1)),
                  row(LANE), row(LANE)],
        out_specs=[row(PROJ_OUT), row(n - PROJ_OUT)],
        out_shape=[jax.ShapeDtypeStruct((m, PROJ_OUT), F32), jax.ShapeDtypeStruct((m, n - PROJ_OUT), F32)],
        compiler_params=_cparams(("parallel",)),
    )(x2d, g, w_all, cos2d, sin2d)


_GDN_PAIR = 2 * GDN_CHUNK


def _gdn_kernel(qkv_ref, z_ref, gab_ref, cw_ref, hp_ref, ng_ref, o_ref, xbuf, state):
    c = GDN_CHUNK
    d = HEAD_DIM
    pr = _GDN_PAIR
    t = qkv_ref.shape[1]
    s = pl.program_id(1)

    @pl.when(s == 0)
    def _():
        xbuf[0:SUBLANE, :] = jnp.zeros((SUBLANE, 3 * GDN_DIM), F32)
        state[...] = jnp.zeros_like(state)

    @pl.when(s != 0)
    def _():
        xbuf[0:SUBLANE, :] = xbuf[t:t + SUBLANE, :]

    xbuf[SUBLANE:SUBLANE + t, :] = qkv_ref[0]
    cw = cw_ref[...]
    hp = hp_ref[...]

    lane_h = lax.broadcasted_iota(jnp.int32, (LANE, GDN_DIM), 0)
    col_h = lax.broadcasted_iota(jnp.int32, (LANE, GDN_DIM), 1) // d
    expand_a = (lane_h == col_h).astype(BF16)
    expand_b = (lane_h == col_h + GDN_HEADS).astype(BF16)
    same_head = (lax.broadcasted_iota(jnp.int32, (GDN_DIM, GDN_DIM), 0) // d
                 == lax.broadcasted_iota(jnp.int32, (GDN_DIM, GDN_DIM), 1) // d).astype(BF16)
    in_chunk = lax.broadcasted_iota(jnp.int32, (pr, LANE), 0) % c
    first = (lax.broadcasted_iota(jnp.int32, (pr, GDN_DIM), 1) % LANE) < d
    a = {}

    def prep_conv(p):
        r0 = SUBLANE + p * pr
        y = cw[GDN_CONV - 1:GDN_CONV] * xbuf[pl.ds(r0, pr), :]
        for j in range(GDN_CONV - 2, -1, -1):
            y = y + cw[j:j + 1] * xbuf[pl.ds(r0 - (GDN_CONV - 1) + j, pr), :]
        a[p] = {"y": _silu(y)}

    def prep_gates(p):
        gab = gab_ref[0, p * pr:(p + 1) * pr, :]
        sp_in = gab + hp[1:2]
        softplus = jnp.maximum(sp_in, 0.0) + jnp.log(1.0 + jnp.exp(-jnp.abs(sp_in)))
        gcum = -jnp.exp(hp[0:1]) * softplus
        shift = 1
        while shift < c:
            gcum = gcum + jnp.where(in_chunk >= shift, pltpu.roll(gcum, shift, 0), 0.0)
            shift *= 2
        g_last = jnp.concatenate(
            [jnp.broadcast_to(gcum[(i + 1) * c - 1:(i + 1) * c, :], (c, LANE)) for i in range(pr // c)], axis=0)
        eg = jnp.exp(gcum)
        a[p].update(gcum=gcum, gcum_t=gcum.T, eg=eg,
                    beta_e=_dot_split(_sigmoid(gab), expand_b),
                    eg_e=_dot_split(eg, expand_a),
                    kdec_e=_dot_split(jnp.exp(g_last - gcum), expand_a))

    def prep_qkv(p):
        ap = a[p]
        y = ap.pop("y")
        q = y[:, :GDN_DIM]
        k = y[:, GDN_DIM:2 * GDN_DIM]
        v = y[:, 2 * GDN_DIM:]
        q = q * (lax.rsqrt(_dot_split(q * q, same_head) + EPS) * (d ** -0.5))
        k = k * lax.rsqrt(_dot_split(k * k, same_head) + EPS)
        kbeta = k * ap["beta_e"]
        ap.update(k16=k.astype(BF16),
                  q16=[jnp.where(first, q, 0.0).astype(BF16), jnp.where(first, 0.0, q).astype(BF16)],
                  kb16=[jnp.where(first, kbeta, 0.0).astype(BF16), jnp.where(first, 0.0, kbeta).astype(BF16)],
                  vb=v * ap["beta_e"], kbe=kbeta * ap["eg_e"],
                  qe16=(q * ap["eg_e"]).astype(BF16), kd16=(k * ap["kdec_e"]).astype(BF16))

    row = lax.broadcasted_iota(jnp.int32, (pr, pr), 0)
    col = lax.broadcasted_iota(jnp.int32, (pr, pr), 1)
    same_chunk = (row // c) == (col // c)
    tri = same_chunk & (row >= col)
    tri_strict = same_chunk & (row > col)
    low = lax.broadcasted_iota(jnp.int32, (pr, LANE), 1) < d
    low_c = lax.broadcasted_iota(jnp.int32, (c, LANE), 1) < d
    n_pair = t // pr
    heads = range(GDN_HEADS)
    per = pr // c
    lanes_of = lambda h: slice((h // 2) * LANE, (h // 2 + 1) * LANE)
    sol, pw, qk = {}, {}, {}
    st = [state[h] for h in heads]
    zeros = jnp.zeros((c, LANE), BF16)
    ng = ng_ref[...]

    def setup_steps(p):
        def step(h):
            ap = a[p]
            g = lanes_of(h)
            decay = jnp.exp(jnp.where(tri, ap["gcum"][:, h:h + 1] - ap["gcum_t"][h:h + 1, :], NEG))
            k2 = ap["k16"][:, g]
            kk = _dot_nt(ap["kb16"][h % 2][:, g], k2)
            pw[h, p] = jnp.where(tri_strict, -(kk * decay), 0.0).astype(BF16)
            qk[h, p] = (_dot_nt(ap["q16"][h % 2][:, g], k2) * decay).astype(BF16)
            ke_sw = pltpu.roll(ap["kbe"][:, g], d, 1)
            vb = ap["vb"][:, g]
            sol[h, p] = jnp.where(low, vb, ke_sw) if h % 2 == 0 else jnp.where(low, ke_sw, vb)

        return ([functools.partial(fn, p) for fn in (prep_conv, prep_gates, prep_qkv)]
                + [functools.partial(step, h) for h in heads])

    def solve_steps(p):
        def apply():
            for h in heads:
                sol[h, p] = sol[h, p] + _dot(pw[h, p], sol[h, p])

        def square():
            for h in heads:
                pw[h, p] = _dot(pw[h, p], pw[h, p]).astype(BF16)

        return [apply] + [square, apply] * 5

    def finish_steps(p):
        ktuw, s_before, v_new, q_st = {}, {}, {}, {}
        rows = lambda ic: slice(ic * c, (ic + 1) * c)

        def operators(ic):
            for h in heads:
                full = lax.dot_general(a[p]["kd16"][rows(ic), lanes_of(h)], sol[h, p][rows(ic)].astype(BF16),
                                       (((0,), (0,)), ((), ())), preferred_element_type=F32)
                ktuw[h, ic] = full[:c] if h % 2 == 0 else full[c:]

        def recur(ic):
            last = (ic + 1) * c - 1
            for h in heads:
                even = h % 2 == 0
                s16 = st[h].astype(BF16)
                s_before[h, ic] = s16
                s_w = jnp.concatenate([zeros, s16] if even else [s16, zeros], axis=0)
                el = a[p]["eg"][last:last + 1, h:h + 1]
                st[h] = jnp.where(low_c if even else ~low_c,
                                  st[h] * el + ktuw[h, ic] - _dot(ktuw[h, ic], s_w), 0.0)

        def read_state(ic):
            for h in heads:
                even = h % 2 == 0
                s16 = s_before[h, ic]
                s_w = jnp.concatenate([zeros, s16] if even else [s16, zeros], axis=0)
                s_q = jnp.concatenate([s16, zeros] if even else [zeros, s16], axis=0)
                sol_c = sol[h, p][rows(ic)]
                v_new[h, ic] = sol_c - _dot(sol_c, s_w)
                q_st[h, ic] = _dot(a[p]["qe16"][rows(ic), lanes_of(h)], s_q)

        def outputs():
            r = slice(p * pr, (p + 1) * pr)
            groups = []
            for j in range(GDN_HEADS // 2):
                halves = []
                for h in (2 * j, 2 * j + 1):
                    vn_pair = jnp.concatenate([v_new[h, ic] for ic in range(per)], axis=0)
                    qs_pair = jnp.concatenate([q_st[h, ic] for ic in range(per)], axis=0)
                    halves.append(qs_pair + _dot(qk[h, p], vn_pair))
                groups.append(jnp.where(low, halves[0], halves[1]))
            o = jnp.concatenate(groups, axis=-1)
            ms = _dot_split(o * o, same_head) * (1.0 / d)
            o_ref[0, r, :] = o * lax.rsqrt(ms + EPS) * ng * _silu(z_ref[0, r, :])

        return ([functools.partial(operators, ic) for ic in range(per)]
                + [functools.partial(recur, ic) for ic in range(per)]
                + [functools.partial(read_state, ic) for ic in range(per)] + [outputs])

    for slot in range(n_pair + 2):
        lists = [steps(q) for steps, q in ((solve_steps, slot - 1), (setup_steps, slot), (finish_steps, slot - 2))
                 if 0 <= q < n_pair]
        merged = sorted(((i + 0.5) / len(lst), k, i, fn) for k, lst in enumerate(lists) for i, fn in enumerate(lst))
        for _, _, _, fn in merged:
            fn()
    for h in heads:
        state[h] = st[h]


def _gdn(proj, conv_w, head_params, norm_g):
    bsz, seq, _ = proj.shape
    t = min(seq, 512)
    w_qkv = 3 * GDN_DIM
    return pl.pallas_call(
        _gdn_kernel,
        grid=(bsz, seq // t),
        in_specs=[pl.BlockSpec((1, t, w_qkv), lambda b, s: (b, s, COL_QKV // w_qkv)),
                  pl.BlockSpec((1, t, GDN_DIM), lambda b, s: (b, s, COL_Z // GDN_DIM)),
                  pl.BlockSpec((1, t, LANE), lambda b, s: (b, s, COL_GAB // LANE)),
                  pl.BlockSpec((GDN_CONV, w_qkv), lambda b, s: (0, 0)),
                  pl.BlockSpec((SUBLANE, LANE), lambda b, s: (0, 0)),
                  pl.BlockSpec((1, GDN_DIM), lambda b, s: (0, 0))],
        out_specs=pl.BlockSpec((1, t, GDN_DIM), lambda b, s: (b, s, 0)),
        out_shape=jax.ShapeDtypeStruct((bsz, seq, GDN_DIM), F32),
        scratch_shapes=[pltpu.VMEM((t + SUBLANE, w_qkv), F32),
                        pltpu.VMEM((GDN_HEADS, HEAD_DIM, LANE), F32)],
        compiler_params=_cparams(("parallel", "arbitrary")),
    )(proj, proj, proj, conv_w, head_params, norm_g)


def _cmp_kernel(c_ref, wa_ref, wb_ref, pea_ref, peb_ref, cos_ref, sin_ref, o_ref):
    cb = c_ref[0]
    ya = _dot(cb + pea_ref[...], wa_ref[...])
    yb = _dot(cb + peb_ref[...], wb_ref[...])
    n = ya.shape[0]
    y = ya + pltpu.roll(yb, n - 1, 0)
    kc = _rope(y[:, :LANE], cos_ref[0], sin_ref[0])
    o_ref[0] = jnp.concatenate([kc, y[:, LANE:]], axis=-1)


def _compress(cmp_rows, wa, wb, pea, peb, cos_c, sin_c):
    bsz, n, width = cmp_rows.shape
    return pl.pallas_call(
        _cmp_kernel,
        grid=(bsz,),
        in_specs=[pl.BlockSpec((1, n, width), lambda b: (b, 0, 0)),
                  pl.BlockSpec(wa.shape, lambda b: (0, 0)),
                  pl.BlockSpec(wb.shape, lambda b: (0, 0)),
                  pl.BlockSpec((1, width), lambda b: (0, 0)),
                  pl.BlockSpec((1, width), lambda b: (0, 0)),
                  pl.BlockSpec((1, n, LANE), lambda b: (b, 0, 0)),
                  pl.BlockSpec((1, n, LANE), lambda b: (b, 0, 0))],
        out_specs=pl.BlockSpec((1, n, 2 * LANE), lambda b: (b, 0, 0)),
        out_shape=jax.ShapeDtypeStruct((bsz, n, 2 * LANE), F32),
        compiler_params=_cparams(("parallel",)),
    )(cmp_rows, wa, wb, pea, peb, cos_c, sin_c)


_NSA_KEY_BLOCK = 512


def _attend(k16, v16, q2, bias):
    d = HEAD_DIM
    half = q2.shape[1] // NSA_KV_HEADS
    s = jnp.dot(k16, q2, preferred_element_type=F32) + bias
    m = jnp.max(s, axis=0, keepdims=True)
    p = jnp.exp2(s - m)
    l = jnp.sum(p, axis=0, keepdims=True)
    pv = lax.dot_general(v16, p.astype(BF16), (((0,), (0,)), ((), ())), preferred_element_type=F32)
    return m, l, jnp.concatenate([pv[:d, :half], pv[d:, half:]], axis=1), p


def _nsa_kernel(q_ref, gate_ref, kvc_ref, ksel_ref, kwin_ref, o_ref, mask_ref, *, seq):
    qb = Q_BLOCK
    d = HEAD_DIM
    grp = NSA_GROUP
    cols = grp * qb
    n_slc = seq // SEL_BLOCK
    topk = min(SEL_TOPK, n_slc)
    ncp = seq // CMP_STRIDE
    qi = pl.program_id(1)
    q_t = q_ref[0].T
    gate_t = _sigmoid(gate_ref[0]).T
    t_q = qi * qb + lax.broadcasted_iota(jnp.int32, (1, qb), 1)

    heads = NSA_KV_HEADS * grp
    tile_all = lambda a: jnp.concatenate([a] * heads, axis=1)
    n_idx = lax.broadcasted_iota(jnp.int32, (ncp, qb), 0)
    cmp_bias = tile_all(jnp.where((n_idx * CMP_STRIDE + (CMP_LEN - 1) <= t_q) & (n_idx < ncp - 1), 0.0, NEG))
    sj = lax.broadcasted_iota(jnp.int32, (n_slc, ncp), 0) * SEL_BLOCK
    ci = lax.broadcasted_iota(jnp.int32, (n_slc, ncp), 1) * CMP_STRIDE
    overlap_t = ((ci < sj + SEL_BLOCK) & (ci + CMP_LEN > sj) & (ci < (ncp - 1) * CMP_STRIDE)).astype(BF16)
    blk = lax.broadcasted_iota(jnp.int32, (n_slc, qb), 0)
    blk_f = blk.astype(F32)
    cur = t_q // SEL_BLOCK
    forced = (blk == 0) | (blk == cur) | (blk == cur - 1)
    kb = min(seq, _NSA_KEY_BLOCK)
    span = min(seq, WINDOW + qb)
    causal_bias = jnp.where(lax.broadcasted_iota(jnp.int32, (qb, qb), 0)
                            <= lax.broadcasted_iota(jnp.int32, (qb, qb), 1), 0.0, NEG)
    w0 = pl.multiple_of(jnp.maximum(qi * qb + qb - span, 0), qb)
    diff = t_q - (w0 + lax.broadcasted_iota(jnp.int32, (span, qb), 0))
    win_bias = tile_all(jnp.where((diff >= 0) & (diff < WINDOW), 0.0, NEG))
    kvc = kvc_ref[0]

    zeros_q = jnp.zeros((d, cols), BF16)
    q_h = [jnp.concatenate([q_t[(h * grp + g) * d:(h * grp + g + 1) * d, :] for g in range(grp)],
                           axis=1).astype(BF16) for h in range(NSA_KV_HEADS)]
    q2 = jnp.concatenate([jnp.concatenate([q_h[0], zeros_q], axis=0),
                          jnp.concatenate([zeros_q, q_h[1]], axis=0)], axis=1)

    m_c, l_c, o_c, p_c = _attend(kvc[:, :LANE].astype(BF16), kvc[:, LANE:].astype(BF16), q2, cmp_bias)
    inv_l = jnp.where(m_c > 0.5 * NEG, 1.0 / l_c, 0.0)
    o_c = o_c * inv_l
    p_c = p_c * inv_l

    kv = kwin_ref[0, pl.ds(w0, span), :]
    _, l_w, o_w, _ = _attend(kv[:, :LANE].astype(BF16), kv[:, LANE:].astype(BF16), q2, win_bias)
    o_w = o_w * (1.0 / l_w)

    for h in range(NSA_KV_HEADS):
        p_sum = p_c[:, h * cols:h * cols + qb]
        for g in range(1, grp):
            p_sum = p_sum + p_c[:, h * cols + g * qb:h * cols + (g + 1) * qb]
        imp = _dot_split3_rhs(overlap_t, p_sum)
        work = jnp.where(forced, FORCE, jnp.where(blk * SEL_BLOCK <= t_q, imp, -1.0))
        sel = jnp.zeros((n_slc, qb), F32)
        for _ in range(topk):
            best = jnp.max(work, axis=0, keepdims=True)
            idx = jnp.min(jnp.where(work == best, blk_f, 1e9), axis=0, keepdims=True)
            pick = blk_f == idx
            sel = jnp.where(pick, 1.0, sel)
            work = jnp.where(pick, -3.0, work)
        sel_bias = jnp.where((sel > 0.5) & (blk * SEL_BLOCK <= t_q), 0.0, NEG)
        for b in range(n_slc):
            mask_ref[b * SEL_BLOCK:(b + 1) * SEL_BLOCK, h * qb:(h + 1) * qb] = jnp.broadcast_to(
                sel_bias[b:b + 1, :], (SEL_BLOCK, qb))
        own_keys = pl.ds(pl.multiple_of(qi * qb, qb), qb)
        mask_ref[own_keys, h * qb:(h + 1) * qb] = mask_ref[own_keys, h * qb:(h + 1) * qb] + causal_bias

    def sel_body(j, carry):
        m, l, acc = carry
        off = pl.multiple_of(j * kb, kb)
        kv = ksel_ref[0, pl.ds(off, kb), :]
        mask = mask_ref[pl.ds(off, kb), :]
        bias = jnp.concatenate([mask[:, :qb]] * grp + [mask[:, qb:]] * grp, axis=1)
        m_b, l_b, pv_b, _ = _attend(kv[:, :LANE].astype(BF16), kv[:, LANE:].astype(BF16), q2, bias)
        m_new = jnp.maximum(m, m_b)
        w_old = jnp.exp2(m - m_new)
        w_blk = jnp.exp2(m_b - m_new)
        return m_new, w_old * l + w_blk * l_b, w_old * acc + w_blk * pv_b

    init = (jnp.full((1, 2 * cols), NEG, F32), jnp.zeros((1, 2 * cols), F32), jnp.zeros((d, 2 * cols), F32))
    _, l_s, acc_s = lax.fori_loop(0, (qi * qb + qb + kb - 1) // kb, sel_body, init)
    o_s = acc_s * (1.0 / l_s)

    out_rows = []
    for hd in range(heads):
        cs = slice(hd * qb, (hd + 1) * qb)
        out_rows.append(gate_t[3 * hd:3 * hd + 1] * o_c[:, cs] + gate_t[3 * hd + 1:3 * hd + 2] * o_s[:, cs]
                        + gate_t[3 * hd + 2:3 * hd + 3] * o_w[:, cs])
    o_ref[0] = jnp.concatenate(out_rows, axis=0).T


def _nsa_attention(proj, kvc):
    bsz, seq, _ = proj.shape
    qb = Q_BLOCK
    ncp = kvc.shape[1]
    kv = 2 * LANE
    return pl.pallas_call(
        functools.partial(_nsa_kernel, seq=seq),
        grid=(bsz, seq // qb),
        in_specs=[pl.BlockSpec((1, qb, NSA_DIM), lambda b, i: (b, i, COL_NQ // NSA_DIM)),
                  pl.BlockSpec((1, qb, LANE), lambda b, i: (b, i, COL_GATE // LANE)),
                  pl.BlockSpec((1, ncp, kv), lambda b, i: (b, 0, 0)),
                  pl.BlockSpec((1, seq, kv), lambda b, i: (b, 0, COL_SEL // kv)),
                  pl.BlockSpec((1, seq, kv), lambda b, i: (b, 0, COL_WIN // kv))],
        out_specs=pl.BlockSpec((1, qb, NSA_DIM), lambda b, i: (b, i, 0)),
        out_shape=jax.ShapeDtypeStruct((bsz, seq, NSA_DIM), F32),
        scratch_shapes=[pltpu.VMEM((seq, NSA_KV_HEADS * qb), F32)],
        compiler_params=_cparams(("parallel", "arbitrary")),
    )(proj, proj, kvc, proj, proj)


_CC_HALO = 32
_CC_ROWS = 64


def _cconv_kernel(u_ref, w_ref, b_ref, lg_ref, lb_ref, o_ref, xbuf, shifted):
    t = u_ref.shape[1]
    s = pl.program_id(1)

    @pl.when(s == 0)
    def _():
        xbuf[0:_CC_HALO, :] = jnp.zeros((_CC_HALO, CONV_CH), F32)

    @pl.when(s != 0)
    def _():
        xbuf[0:_CC_HALO, :] = xbuf[t:t + _CC_HALO, :]

    u = u_ref[0]
    xbuf[_CC_HALO:_CC_HALO + t, :] = u[:, :CONV_CH] * _sigmoid(u[:, CONV_CH:])
    w = w_ref[...]
    first = _CC_HALO - (CONV_WIDTH - 1)
    rows_kept = t + _CC_HALO - SUBLANE
    for res in range(1, SUBLANE):
        shifted[res - 1, 0:rows_kept, :] = xbuf[pl.ds(res, rows_kept), :]
    for r in range(t // _CC_ROWS):
        acc = jnp.broadcast_to(b_ref[...], (_CC_ROWS, CONV_CH))
        for j in range(CONV_WIDTH):
            res = (first + j) % SUBLANE
            base = r * _CC_ROWS + first + j - res
            tap = xbuf[base:base + _CC_ROWS, :] if res == 0 else shifted[res - 1, base:base + _CC_ROWS, :]
            acc = acc + w[j:j + 1] * tap
        mu = jnp.mean(acc, axis=-1, keepdims=True)
        var = jnp.mean(jnp.square(acc - mu), axis=-1, keepdims=True)
        hn = (acc - mu) * lax.rsqrt(var + EPS) * lg_ref[...] + lb_ref[...]
        o_ref[0, r * _CC_ROWS:(r + 1) * _CC_ROWS, :] = _silu(hn)


def _cconv(proj, dw_w, dw_b, ln_g, ln_b):
    bsz, seq, _ = proj.shape
    t = min(seq, 512)
    wu = 2 * CONV_CH
    vec = pl.BlockSpec((1, CONV_CH), lambda b, s: (0, 0))
    return pl.pallas_call(
        _cconv_kernel,
        grid=(bsz, seq // t),
        in_specs=[pl.BlockSpec((1, t, wu), lambda b, s: (b, s, COL_CU // wu)),
                  pl.BlockSpec((CONV_WIDTH, CONV_CH), lambda b, s: (0, 0)),
                  vec, vec, vec],
        out_specs=pl.BlockSpec((1, t, CONV_CH), lambda b, s: (b, s, 0)),
        out_shape=jax.ShapeDtypeStruct((bsz, seq, CONV_CH), F32),
        scratch_shapes=[pltpu.VMEM((t + _CC_HALO, CONV_CH), F32),
                        pltpu.VMEM((SUBLANE - 1, t + _CC_HALO, CONV_CH), F32)],
        compiler_params=_cparams(("parallel", "arbitrary")),
    )(proj, dw_w, dw_b, ln_g, ln_b)


_FFN_TILE = 256


def _ffn_kernel(x_ref, oa_ref, ob_ref, oc_ref, wo_ref, gmix_ref, gpre_ref, wup_ref, cw_ref, wd_ref, gpost_ref,
                o_ref, xn_ref, act_ref, stage_ref, carry_ref):
    tm = x_ref.shape[1]
    d_ff = wd_ref.shape[0]
    tf = _FFN_TILE
    hal = SUBLANE

    @pl.when(pl.program_id(1) == 0)
    def _():
        carry_ref[...] = jnp.zeros_like(carry_ref)

    mix = (jnp.dot(oa_ref[0].astype(BF16), wo_ref[:GDN_DIM], preferred_element_type=F32)
           + jnp.dot(ob_ref[0].astype(BF16), wo_ref[GDN_DIM:GDN_DIM + NSA_DIM], preferred_element_type=F32)
           + jnp.dot(oc_ref[0].astype(BF16), wo_ref[GDN_DIM + NSA_DIM:], preferred_element_type=F32))
    x = x_ref[0] + _rms(mix, gmix_ref[...])
    xn_ref[...] = _rms(x, gpre_ref[...]).astype(BF16)
    for f in range(d_ff // tf):
        ys = []
        for part in range(2):
            cols = slice(part * d_ff + f * tf, part * d_ff + (f + 1) * tf)
            h = jnp.dot(xn_ref[...], wup_ref[:, cols], preferred_element_type=F32)
            stage = stage_ref.at[f % 2, part]
            stage[0:hal, :] = carry_ref[:, cols]
            stage[hal:hal + tm, :] = h
            carry_ref[:, cols] = h[tm - hal:tm, :]
            cw = cw_ref[:, cols]
            ys.append(cw[2:3] * h + cw[1:2] * stage[pl.ds(hal - 1, tm), :]
                      + cw[0:1] * stage[pl.ds(hal - 2, tm), :])
        act_ref[:, f * tf:(f + 1) * tf] = (_silu(ys[0]) * ys[1]).astype(BF16)
    out = jnp.dot(act_ref[...], wd_ref[...], preferred_element_type=F32)
    o_ref[0] = x + _rms(out, gpost_ref[...])


def _outproj_ffn(x, o_a, o_b, o_c, layer, w_out, g_mix, g_pre, w_up, conv_w, w_down, g_post):
    bsz, seq, d = x.shape
    d_ff = w_down.shape[1]
    tm = min(seq, 512)
    kw = conv_w.shape[1]
    resident = lambda shape: pl.BlockSpec((None,) + shape, lambda b, i: (layer, 0, 0),
                                          pipeline_mode=pl.Buffered(1))
    rows = lambda width: pl.BlockSpec((1, tm, width), lambda b, i: (b, i, 0))
    return pl.pallas_call(
        _ffn_kernel,
        grid=(bsz, seq // tm),
        in_specs=[rows(d), rows(GDN_DIM), rows(NSA_DIM), rows(CONV_CH),
                  resident((d, d)),
                  pl.BlockSpec((1, d), lambda b, i: (0, 0)),
                  pl.BlockSpec((1, d), lambda b, i: (0, 0)),
                  resident((d, 2 * d_ff)),
                  resident((kw, 2 * d_ff)),
                  resident((d_ff, d)),
                  pl.BlockSpec((1, d), lambda b, i: (0, 0))],
        out_specs=pl.BlockSpec((1, tm, d), lambda b, i: (b, i, 0)),
        out_shape=jax.ShapeDtypeStruct((bsz, seq, d), F32),
        scratch_shapes=[pltpu.VMEM((tm, d), BF16),
                        pltpu.VMEM((tm, d_ff), BF16),
                        pltpu.VMEM((2, 2, tm + SUBLANE, _FFN_TILE), F32),
                        pltpu.VMEM((SUBLANE, 2 * d_ff), F32)],
        compiler_params=_cparams(("parallel", "arbitrary")),
    )(x, o_a, o_b, o_c, w_out, g_mix, g_pre, w_up, conv_w, w_down, g_post)


def _pack_w_in(w_in):
    depth, d, _ = w_in.shape
    sizes = (GDN_DIM,) * 4 + (GDN_HEADS,) * 2 + (NSA_DIM,) + (LANE,) * 6 + (3 * NSA_HEADS, 2 * CONV_CH)
    offs = np.concatenate([[0], np.cumsum(sizes)])
    piece = lambda k: w_in[:, :, offs[k]:offs[k + 1]]
    zeros = lambda n: jnp.zeros((depth, d, n), w_in.dtype)
    gq, gk, gv, gz, ga, gb, nq, nkc, nvc, nks, nvs, nkw, nvw, ngate, cu = [piece(k) for k in range(15)]
    cols = [gq, gk, gv, gz, nq,
            ga, gb, zeros(LANE - 2 * GDN_HEADS),
            cu, nks, nvs, nkw, nvw,
            ngate, zeros(LANE - 3 * NSA_HEADS),
            nkc, nvc]
    packed = jnp.concatenate(cols, axis=-1)
    assert packed.shape[-1] == PROJ_DIM
    return packed.astype(BF16)


def _pack_compress(wk, wv, pe_k, pe_v):
    half = CMP_LEN // 2
    d = HEAD_DIM
    big = jnp.zeros((CMP_LEN, 4, d, 4, d), F32)
    for slot, w in enumerate((wk, wk, wv, wv)):
        big = big.at[:, slot, :, slot, :].set(w)
    big = big.reshape(CMP_LEN, 4 * d, 4 * d)
    wa = big[:half].reshape(half * 4 * d, 4 * d).astype(BF16)
    wb = big[half:].reshape(half * 4 * d, 4 * d).astype(BF16)
    pe = jnp.concatenate([pe_k, pe_k, pe_v, pe_v], axis=-1)
    pea = pe[:half].reshape(1, half * 4 * d)
    peb = pe[half:].reshape(1, half * 4 * d)
    return wa, wb, pea, peb


def kernel(x, positions, norm_mix_pre, norm_mix_post, norm_ffn_pre, norm_ffn_post, w_in, w_out, gdn_conv_w, gdn_a_log, gdn_dt_bias, gdn_norm_g, nsa_cmp_wk, nsa_cmp_wv, nsa_cmp_pe_k, nsa_cmp_pe_v, cc_dw_w, cc_dw_b, cc_ln_g, cc_ln_b, ffn_w_up, ffn_conv_w, ffn_w_down):
    bsz, seq, d = x.shape
    depth = w_in.shape[0]
    assert seq % Q_BLOCK == 0 and d == GDN_DIM + NSA_DIM + CONV_CH
    m = bsz * seq

    cos, sin = _rope_tables(positions)
    ncp = seq // CMP_STRIDE
    pad_rows = lambda t: jnp.pad(t[:, CMP_LEN - 1::CMP_STRIDE], ((0, 0), (0, 1), (0, 0)))
    cos_c, sin_c = pad_rows(cos), pad_rows(sin)

    w_in_p = _pack_w_in(w_in)
    w_out_b = w_out.astype(BF16)
    w_up_b = ffn_w_up.astype(BF16)
    w_down_b = ffn_w_down.astype(BF16)
    head_params = jnp.zeros((depth, SUBLANE, LANE), F32)
    head_params = head_params.at[:, 0, :GDN_HEADS].set(gdn_a_log).at[:, 1, :GDN_HEADS].set(gdn_dt_bias)

    cos2d = cos.reshape(m, LANE)
    sin2d = sin.reshape(m, LANE)
    for l in range(depth):
        proj, cmp_cols = _inproj(x.reshape(m, d), norm_mix_pre[l][None], w_in_p, l, cos2d, sin2d)
        proj = proj.reshape(bsz, seq, PROJ_OUT)
        o_a = _gdn(proj, gdn_conv_w[l], head_params[l], jnp.tile(gdn_norm_g[l], GDN_HEADS)[None])
        wa, wb, pea, peb = _pack_compress(nsa_cmp_wk[l], nsa_cmp_wv[l], nsa_cmp_pe_k[l], nsa_cmp_pe_v[l])
        kvc = _compress(cmp_cols.reshape(bsz, ncp, CMP_STRIDE * 2 * LANE), wa, wb, pea, peb, cos_c, sin_c)
        o_b = _nsa_attention(proj, kvc)
        o_c = _cconv(proj, cc_dw_w[l], cc_dw_b[l][None], cc_ln_g[l][None], cc_ln_b[l][None])
        x = _outproj_ffn(x, o_a, o_b, o_c, l, w_out_b, norm_mix_post[l][None], norm_ffn_pre[l][None],
                         w_up_b, ffn_conv_w, w_down_b, norm_ffn_post[l][None])
    return x
```

```python
import functools

import jax
import jax.numpy as jnp
import numpy as np
from jax import lax
from jax.experimental import pallas as pl
from jax.experimental.pallas import tpu as pltpu

F32 = jnp.float32
BF16 = jnp.bfloat16

HEAD_DIM = 64
GDN_HEADS = 6
GDN_DIM = GDN_HEADS * HEAD_DIM
GDN_CONV = 4
GDN_CHUNK = 64
NSA_HEADS = 6
NSA_KV_HEADS = 2
NSA_GROUP = NSA_HEADS // NSA_KV_HEADS
NSA_DIM = NSA_HEADS * HEAD_DIM
CMP_STRIDE = 16
CMP_LEN = 32
SEL_BLOCK = 64
SEL_TOPK = 8
WINDOW = 512
Q_BLOCK = 128
CONV_CH = 256
CONV_WIDTH = 31
ROPE_THETA = 10000.0
EPS = 1e-6
NEG = -1e30
FORCE = 1e4
_LOG2_E = 1.4426950408889634

LANE = 128
SUBLANE = 8
VMEM_LIMIT = 56 * 1024 * 1024

COL_QKV = 0
COL_Z = 1152
COL_NQ = 1536
COL_GAB = 1920
COL_CU = 2048
COL_SEL = 2560
COL_WIN = 2816
COL_GATE = 3072
PROJ_OUT = 3200
COL_CMP = 3200
PROJ_DIM = 3456


def _cparams(sem):
    return pltpu.CompilerParams(dimension_semantics=sem, vmem_limit_bytes=VMEM_LIMIT)


def _sigmoid(x):
    return 1.0 / (1.0 + jnp.exp(-x))


def _silu(x):
    return x * _sigmoid(x)


def _dot(a, b):
    return jnp.dot(a.astype(BF16), b.astype(BF16), preferred_element_type=F32)


def _dot_nt(a, b):
    return lax.dot_general(a.astype(BF16), b.astype(BF16), (((1,), (1,)), ((), ())),
                           preferred_element_type=F32)


def _dot_split3_rhs(sel, x):
    hi = x.astype(BF16)
    r1 = x - hi.astype(F32)
    mid = r1.astype(BF16)
    lo = (r1 - mid.astype(F32)).astype(BF16)
    return (jnp.dot(sel, hi, preferred_element_type=F32) + jnp.dot(sel, mid, preferred_element_type=F32)
            + jnp.dot(sel, lo, preferred_element_type=F32))


def _dot_split(x, sel):
    hi = x.astype(BF16)
    lo = (x - hi.astype(F32)).astype(BF16)
    return (jnp.dot(hi, sel, preferred_element_type=F32) + jnp.dot(lo, sel, preferred_element_type=F32))


def _rms(x, g):
    return x * lax.rsqrt(jnp.mean(x * x, axis=-1, keepdims=True) + EPS) * g


def _rope_table_kernel(pos_ref, inv_ref, sign_ref, cos_ref, sin_ref):
    ang = pos_ref[0].astype(F32) * inv_ref[...]
    cos_ref[0] = jnp.cos(ang)
    sin_ref[0] = jnp.sin(ang) * sign_ref[...]


def _rope_tables(positions):
    bsz, seq = positions.shape
    t = min(seq, 512)
    inv = 1.0 / (ROPE_THETA ** (jnp.arange(0, HEAD_DIM, 2, dtype=F32) / HEAD_DIM))
    inv = jnp.tile(inv, LANE // (HEAD_DIM // 2))[None, :]
    sign = jnp.tile(jnp.concatenate([-jnp.ones(HEAD_DIM // 2, F32), jnp.ones(HEAD_DIM // 2, F32)]),
                    LANE // HEAD_DIM)[None, :]
    out = jax.ShapeDtypeStruct((bsz, seq, LANE), F32)
    return pl.pallas_call(
        _rope_table_kernel,
        grid=(bsz, seq // t),
        in_specs=[pl.BlockSpec((1, t, 1), lambda b, s: (b, s, 0)),
                  pl.BlockSpec((1, LANE), lambda b, s: (0, 0)),
                  pl.BlockSpec((1, LANE), lambda b, s: (0, 0))],
        out_specs=[pl.BlockSpec((1, t, LANE), lambda b, s: (b, s, 0))] * 2,
        out_shape=[out, out],
        compiler_params=_cparams(("parallel", "parallel")),
    )(positions[:, :, None], inv, sign)


def _rope(x, cos, sin):
    lane = lax.broadcasted_iota(jnp.int32, x.shape, 1)
    first_half = (lane % HEAD_DIM) < (HEAD_DIM // 2)
    partner = jnp.where(first_half, pltpu.roll(x, LANE - HEAD_DIM // 2, 1), pltpu.roll(x, HEAD_DIM // 2, 1))
    return x * cos + partner * sin


def _inproj_kernel(x_ref, g_ref, w_ref, cos_ref, sin_ref, o_ref, cmp_ref):
    xn = _rms(x_ref[...], g_ref[...]).astype(BF16)
    y = jnp.dot(xn, w_ref[...], preferred_element_type=F32)
    cos = cos_ref[...]
    sin = sin_ref[...]
    scale = HEAD_DIM ** -0.5 * _LOG2_E
    o_ref[:, :COL_NQ] = y[:, :COL_NQ]
    for c in range(COL_NQ, COL_NQ + NSA_DIM, LANE):
        o_ref[:, c:c + LANE] = _rope(y[:, c:c + LANE], cos, sin) * scale
    o_ref[:, COL_GAB:COL_SEL] = y[:, COL_GAB:COL_SEL]
    for c in (COL_SEL, COL_WIN):
        o_ref[:, c:c + LANE] = _rope(y[:, c:c + LANE], cos, sin)
        o_ref[:, c + LANE:c + 2 * LANE] = y[:, c + LANE:c + 2 * LANE]
    o_ref[:, COL_GATE:PROJ_OUT] = y[:, COL_GATE:PROJ_OUT]
    cmp_ref[...] = y[:, COL_CMP:]


def _inproj(x2d, g, w_all, layer, cos2d, sin2d):
    m, d = x2d.shape
    n = w_all.shape[2]
    tm = min(m, 512)
    row = lambda width: pl.BlockSpec((tm, width), lambda i: (i, 0))
    return pl.pallas_call(
        _inproj_kernel,
        grid=(m // tm,),
        in_specs=[row(d),
                  pl.BlockSpec((1, d), lambda i: (0, 0)),
                  pl.BlockSpec((None, d, n), lambda i: (layer, 0, 0), pipeline_mode=pl.Buffered(1)),
                  row(LANE), row(LANE)],
        out_specs=[row(PROJ_OUT), row(n - PROJ_OUT)],
        out_shape=[jax.ShapeDtypeStruct((m, PROJ_OUT), F32), jax.ShapeDtypeStruct((m, n - PROJ_OUT), F32)],
        compiler_params=_cparams(("parallel",)),
    )(x2d, g, w_all, cos2d, sin2d)


_GDN_PAIR = 2 * GDN_CHUNK


def _gdn_kernel(qkv_ref, z_ref, gab_ref, cw_ref, hp_ref, ng_ref, o_ref, xbuf, state):
    c = GDN_CHUNK
    d = HEAD_DIM
    pr = _GDN_PAIR
    t = qkv_ref.shape[1]
    s = pl.program_id(1)

    @pl.when(s == 0)
    def _():
        xbuf[0:SUBLANE, :] = jnp.zeros((SUBLANE, 3 * GDN_DIM), F32)
        state[...] = jnp.zeros_like(state)

    @pl.when(s != 0)
    def _():
        xbuf[0:SUBLANE, :] = xbuf[t:t + SUBLANE, :]

    xbuf[SUBLANE:SUBLANE + t, :] = qkv_ref[0]
    cw = cw_ref[...]
    hp = hp_ref[...]

    lane_h = lax.broadcasted_iota(jnp.int32, (LANE, GDN_DIM), 0)
    col_h = lax.broadcasted_iota(jnp.int32, (LANE, GDN_DIM), 1) // d
    expand_a = (lane_h == col_h).astype(BF16)
    expand_b = (lane_h == col_h + GDN_HEADS).astype(BF16)
    same_head = (lax.broadcasted_iota(jnp.int32, (GDN_DIM, GDN_DIM), 0) // d
                 == lax.broadcasted_iota(jnp.int32, (GDN_DIM, GDN_DIM), 1) // d).astype(BF16)
    in_chunk = lax.broadcasted_iota(jnp.int32, (pr, LANE), 0) % c
    first = (lax.broadcasted_iota(jnp.int32, (pr, GDN_DIM), 1) % LANE) < d
    a = {}

    def prep_conv(p):
        r0 = SUBLANE + p * pr
        y = cw[GDN_CONV - 1:GDN_CONV] * xbuf[pl.ds(r0, pr), :]
        for j in range(GDN_CONV - 2, -1, -1):
            y = y + cw[j:j + 1] * xbuf[pl.ds(r0 - (GDN_CONV - 1) + j, pr), :]
        a[p] = {"y": _silu(y)}

    def prep_gates(p):
        gab = gab_ref[0, p * pr:(p + 1) * pr, :]
        sp_in = gab + hp[1:2]
        softplus = jnp.maximum(sp_in, 0.0) + jnp.log(1.0 + jnp.exp(-jnp.abs(sp_in)))
        gcum = -jnp.exp(hp[0:1]) * softplus
        shift = 1
        while shift < c:
            gcum = gcum + jnp.where(in_chunk >= shift, pltpu.roll(gcum, shift, 0), 0.0)
            shift *= 2
        g_last = jnp.concatenate(
            [jnp.broadcast_to(gcum[(i + 1) * c - 1:(i + 1) * c, :], (c, LANE)) for i in range(pr // c)], axis=0)
        eg = jnp.exp(gcum)
        a[p].update(gcum=gcum, gcum_t=gcum.T, eg=eg,
                    beta_e=_dot_split(_sigmoid(gab), expand_b),
                    eg_e=_dot_split(eg, expand_a),
                    kdec_e=_dot_split(jnp.exp(g_last - gcum), expand_a))

    def prep_qkv(p):
        ap = a[p]
        y = ap.pop("y")
        q = y[:, :GDN_DIM]
        k = y[:, GDN_DIM:2 * GDN_DIM]
        v = y[:, 2 * GDN_DIM:]
        q = q * (lax.rsqrt(_dot_split(q * q, same_head) + EPS) * (d ** -0.5))
        k = k * lax.rsqrt(_dot_split(k * k, same_head) + EPS)
        kbeta = k * ap["beta_e"]
        ap.update(k16=k.astype(BF16),
                  q16=[jnp.where(first, q, 0.0).astype(BF16), jnp.where(first, 0.0, q).astype(BF16)],
                  kb16=[jnp.where(first, kbeta, 0.0).astype(BF16), jnp.where(first, 0.0, kbeta).astype(BF16)],
                  vb=v * ap["beta_e"], kbe=kbeta * ap["eg_e"],
                  qe16=(q * ap["eg_e"]).astype(BF16), kd16=(k * ap["kdec_e"]).astype(BF16))

    row = lax.broadcasted_iota(jnp.int32, (pr, pr), 0)
    col = lax.broadcasted_iota(jnp.int32, (pr, pr), 1)
    same_chunk = (row // c) == (col // c)
    tri = same_chunk & (row >= col)
    tri_strict = same_chunk & (row > col)
    low = lax.broadcasted_iota(jnp.int32, (pr, LANE), 1) < d
    low_c = lax.broadcasted_iota(jnp.int32, (c, LANE), 1) < d
    n_pair = t // pr
    heads = range(GDN_HEADS)
    per = pr // c
    lanes_of = lambda h: slice((h // 2) * LANE, (h // 2 + 1) * LANE)
    sol, pw, qk = {}, {}, {}
    st = [state[h] for h in heads]
    zeros = jnp.zeros((c, LANE), BF16)
    ng = ng_ref[...]

    def setup_steps(p):
        def step(j):
            ap = a[p]
            g = lanes_of(2 * j)
            lhs = jnp.concatenate([t16[:, g] for t16 in ap["kb16"] + ap["q16"]], axis=0)
            prod = _dot_nt(lhs, ap["k16"][:, g])
            ke_sw = pltpu.roll(ap["kbe"][:, g], d, 1)
            vb = ap["vb"][:, g]
            for i, h in enumerate((2 * j, 2 * j + 1)):
                decay = jnp.exp(jnp.where(tri, ap["gcum"][:, h:h + 1] - ap["gcum_t"][h:h + 1, :], NEG))
                pw[h, p] = jnp.where(tri_strict, -(prod[i * pr:(i + 1) * pr] * decay), 0.0).astype(BF16)
                qk[h, p] = (prod[(2 + i) * pr:(3 + i) * pr] * decay).astype(BF16)
                sol[h, p] = jnp.where(low, vb, ke_sw) if i == 0 else jnp.where(low, ke_sw, vb)

        return ([functools.partial(fn, p) for fn in (prep_conv, prep_gates, prep_qkv)]
                + [functools.partial(step, j) for j in range(GDN_HEADS // 2)])

    def solve_steps(p):
        def apply():
            for h in heads:
                sol[h, p] = sol[h, p] + _dot(pw[h, p], sol[h, p])

        def square():
            for h in heads:
                pw[h, p] = _dot(pw[h, p], pw[h, p]).astype(BF16)

        return [apply] + [square, apply] * 5

    def finish_steps(p):
        ktuw, s_before, v_new, q_st = {}, {}, {}, {}
        rows = lambda ic: slice(ic * c, (ic + 1) * c)

        def operators(ic):
            for h in heads:
                full = lax.dot_general(a[p]["kd16"][rows(ic), lanes_of(h)], sol[h, p][rows(ic)].astype(BF16),
                                       (((0,), (0,)), ((), ())), preferred_element_type=F32)
                ktuw[h, ic] = full[:c] if h % 2 == 0 else full[c:]

        def recur(ic):
            last = (ic + 1) * c - 1
            for h in heads:
                even = h % 2 == 0
                s16 = st[h].astype(BF16)
                s_before[h, ic] = s16
                s_w = jnp.concatenate([zeros, s16] if even else [s16, zeros], axis=0)
                el = a[p]["eg"][last:last + 1, h:h + 1]
                st[h] = jnp.where(low_c if even else ~low_c,
                                  st[h] * el + ktuw[h, ic] - _dot(ktuw[h, ic], s_w), 0.0)

        def read_state(ic):
            for h in heads:
                even = h % 2 == 0
                s16 = s_before[h, ic]
                s_w = jnp.concatenate([zeros, s16] if even else [s16, zeros], axis=0)
                s_q = jnp.concatenate([s16, zeros] if even else [zeros, s16], axis=0)
                sol_c = sol[h, p][rows(ic)]
                v_new[h, ic] = sol_c - _dot(sol_c, s_w)
                q_st[h, ic] = _dot(a[p]["qe16"][rows(ic), lanes_of(h)], s_q)

        def outputs():
            r = slice(p * pr, (p + 1) * pr)
            groups = []
            for j in range(GDN_HEADS // 2):
                halves = []
                for h in (2 * j, 2 * j + 1):
                    vn_pair = jnp.concatenate([v_new[h, ic] for ic in range(per)], axis=0)
                    qs_pair = jnp.concatenate([q_st[h, ic] for ic in range(per)], axis=0)
                    halves.append(qs_pair + _dot(qk[h, p], vn_pair))
                groups.append(jnp.where(low, halves[0], halves[1]))
            o = jnp.concatenate(groups, axis=-1)
            ms = _dot_split(o * o, same_head) * (1.0 / d)
            o_ref[0, r, :] = o * lax.rsqrt(ms + EPS) * ng * _silu(z_ref[0, r, :])

        return ([functools.partial(operators, ic) for ic in range(per)]
                + [functools.partial(recur, ic) for ic in range(per)]
                + [functools.partial(read_state, ic) for ic in range(per)] + [outputs])

    for slot in range(n_pair + 2):
        lists = [steps(q) for steps, q in ((solve_steps, slot - 1), (setup_steps, slot), (finish_steps, slot - 2))
                 if 0 <= q < n_pair]
        merged = sorted(((i + 0.5) / len(lst), k, i, fn) for k, lst in enumerate(lists) for i, fn in enumerate(lst))
        for _, _, _, fn in merged:
            fn()
    for h in heads:
        state[h] = st[h]


def _gdn(proj, conv_w, head_params, norm_g):
    bsz, seq, _ = proj.shape
    t = min(seq, 512)
    w_qkv = 3 * GDN_DIM
    return pl.pallas_call(
        _gdn_kernel,
        grid=(bsz, seq // t),
        in_specs=[pl.BlockSpec((1, t, w_qkv), lambda b, s: (b, s, COL_QKV // w_qkv)),
                  pl.BlockSpec((1, t, GDN_DIM), lambda b, s: (b, s, COL_Z // GDN_DIM)),
                  pl.BlockSpec((1, t, LANE), lambda b, s: (b, s, COL_GAB // LANE)),
                  pl.BlockSpec((GDN_CONV, w_qkv), lambda b, s: (0, 0)),
                  pl.BlockSpec((SUBLANE, LANE), lambda b, s: (0, 0)),
                  pl.BlockSpec((1, GDN_DIM), lambda b, s: (0, 0))],
        out_specs=pl.BlockSpec((1, t, GDN_DIM), lambda b, s: (b, s, 0)),
        out_shape=jax.ShapeDtypeStruct((bsz, seq, GDN_DIM), F32),
        scratch_shapes=[pltpu.VMEM((t + SUBLANE, w_qkv), F32),
                        pltpu.VMEM((GDN_HEADS, HEAD_DIM, LANE), F32)],
        compiler_params=_cparams(("parallel", "arbitrary")),
    )(proj, proj, proj, conv_w, head_params, norm_g)


def _cmp_kernel(c_ref, wa_ref, wb_ref, pea_ref, peb_ref, cos_ref, sin_ref, o_ref):
    cb = c_ref[0]
    ya = _dot(cb + pea_ref[...], wa_ref[...])
    yb = _dot(cb + peb_ref[...], wb_ref[...])
    n = ya.shape[0]
    y = ya + pltpu.roll(yb, n - 1, 0)
    kc = _rope(y[:, :LANE], cos_ref[0], sin_ref[0])
    o_ref[0] = jnp.concatenate([kc, y[:, LANE:]], axis=-1)


def _compress(cmp_rows, layer, wa, wb, pea, peb, cos_c, sin_c):
    bsz, n, width = cmp_rows.shape
    of_layer = lambda arr: pl.BlockSpec((None,) + arr.shape[1:], lambda b: (layer, 0, 0))
    return pl.pallas_call(
        _cmp_kernel,
        grid=(bsz,),
        in_specs=[pl.BlockSpec((1, n, width), lambda b: (b, 0, 0)),
                  of_layer(wa), of_layer(wb), of_layer(pea), of_layer(peb),
                  pl.BlockSpec((1, n, LANE), lambda b: (b, 0, 0)),
                  pl.BlockSpec((1, n, LANE), lambda b: (b, 0, 0))],
        out_specs=pl.BlockSpec((1, n, 2 * LANE), lambda b: (b, 0, 0)),
        out_shape=jax.ShapeDtypeStruct((bsz, n, 2 * LANE), F32),
        compiler_params=_cparams(("parallel",)),
    )(cmp_rows, wa, wb, pea, peb, cos_c, sin_c)


_NSA_KEY_BLOCK = 512


def _attend(k16, v16, q2, bias):
    d = HEAD_DIM
    half = q2.shape[1] // NSA_KV_HEADS
    s = jnp.dot(k16, q2, preferred_element_type=F32) + bias
    m = jnp.max(s, axis=0, keepdims=True)
    p = jnp.exp2(s - m)
    l = jnp.sum(p, axis=0, keepdims=True)
    pv = lax.dot_general(v16, p.astype(BF16), (((0,), (0,)), ((), ())), preferred_element_type=F32)
    return m, l, jnp.concatenate([pv[:d, :half], pv[d:, half:]], axis=1), p


def _nsa_kernel(q_ref, gate_ref, kvc_ref, ksel_ref, kwin_ref, o_ref, mask_ref, *, seq):
    qb = Q_BLOCK
    d = HEAD_DIM
    grp = NSA_GROUP
    cols = grp * qb
    n_slc = seq // SEL_BLOCK
    topk = min(SEL_TOPK, n_slc)
    ncp = seq // CMP_STRIDE
    qi = pl.program_id(1)
    q_t = q_ref[0].T
    gate_t = _sigmoid(gate_ref[0]).T
    t_q = qi * qb + lax.broadcasted_iota(jnp.int32, (1, qb), 1)

    heads = NSA_KV_HEADS * grp
    tile_all = lambda a: jnp.concatenate([a] * heads, axis=1)
    n_idx = lax.broadcasted_iota(jnp.int32, (ncp, qb), 0)
    cmp_bias = tile_all(jnp.where((n_idx * CMP_STRIDE + (CMP_LEN - 1) <= t_q) & (n_idx < ncp - 1), 0.0, NEG))
    sj = lax.broadcasted_iota(jnp.int32, (n_slc, ncp), 0) * SEL_BLOCK
    ci = lax.broadcasted_iota(jnp.int32, (n_slc, ncp), 1) * CMP_STRIDE
    overlap_t = ((ci < sj + SEL_BLOCK) & (ci + CMP_LEN > sj) & (ci < (ncp - 1) * CMP_STRIDE)).astype(BF16)
    blk = lax.broadcasted_iota(jnp.int32, (n_slc, qb), 0)
    blk_f = blk.astype(F32)
    cur = t_q // SEL_BLOCK
    forced = (blk == 0) | (blk == cur) | (blk == cur - 1)
    kb = min(seq, _NSA_KEY_BLOCK)
    span = min(seq, WINDOW + qb)
    causal_bias = jnp.where(lax.broadcasted_iota(jnp.int32, (qb, qb), 0)
                            <= lax.broadcasted_iota(jnp.int32, (qb, qb), 1), 0.0, NEG)
    w0 = pl.multiple_of(jnp.maximum(qi * qb + qb - span, 0), qb)
    diff = t_q - (w0 + lax.broadcasted_iota(jnp.int32, (span, qb), 0))
    win_bias = tile_all(jnp.where((diff >= 0) & (diff < WINDOW), 0.0, NEG))
    kvc = kvc_ref[0]

    zeros_q = jnp.zeros((d, cols), BF16)
    q_h = [jnp.concatenate([q_t[(h * grp + g) * d:(h * grp + g + 1) * d, :] for g in range(grp)],
                           axis=1).astype(BF16) for h in range(NSA_KV_HEADS)]
    q2 = jnp.concatenate([jnp.concatenate([q_h[0], zeros_q], axis=0),
                          jnp.concatenate([zeros_q, q_h[1]], axis=0)], axis=1)

    m_c, l_c, o_c, p_c = _attend(kvc[:, :LANE].astype(BF16), kvc[:, LANE:].astype(BF16), q2, cmp_bias)
    inv_l = jnp.where(m_c > 0.5 * NEG, 1.0 / l_c, 0.0)
    o_c = o_c * inv_l
    p_c = p_c * inv_l

    kv = kwin_ref[0, pl.ds(w0, span), :]
    _, l_w, o_w, _ = _attend(kv[:, :LANE].astype(BF16), kv[:, LANE:].astype(BF16), q2, win_bias)
    o_w = o_w * (1.0 / l_w)

    for h in range(NSA_KV_HEADS):
        p_sum = p_c[:, h * cols:h * cols + qb]
        for g in range(1, grp):
            p_sum = p_sum + p_c[:, h * cols + g * qb:h * cols + (g + 1) * qb]
        imp = _dot_split3_rhs(overlap_t, p_sum)
        work = jnp.where(forced, FORCE, jnp.where(blk * SEL_BLOCK <= t_q, imp, -1.0))
        sel = jnp.zeros((n_slc, qb), F32)
        for _ in range(topk):
            best = jnp.max(work, axis=0, keepdims=True)
            idx = jnp.min(jnp.where(work == best, blk_f, 1e9), axis=0, keepdims=True)
            pick = blk_f == idx
            sel = jnp.where(pick, 1.0, sel)
            work = jnp.where(pick, -3.0, work)
        sel_bias = jnp.where((sel > 0.5) & (blk * SEL_BLOCK <= t_q), 0.0, NEG)
        for b in range(n_slc):
            mask_ref[b * SEL_BLOCK:(b + 1) * SEL_BLOCK, h * qb:(h + 1) * qb] = jnp.broadcast_to(
                sel_bias[b:b + 1, :], (SEL_BLOCK, qb))
        own_keys = pl.ds(pl.multiple_of(qi * qb, qb), qb)
        mask_ref[own_keys, h * qb:(h + 1) * qb] = mask_ref[own_keys, h * qb:(h + 1) * qb] + causal_bias

    def sel_body(j, carry):
        m, l, acc = carry
        off = pl.multiple_of(j * kb, kb)
        kv = ksel_ref[0, pl.ds(off, kb), :]
        mask = mask_ref[pl.ds(off, kb), :]
        bias = jnp.concatenate([mask[:, :qb]] * grp + [mask[:, qb:]] * grp, axis=1)
        m_b, l_b, pv_b, _ = _attend(kv[:, :LANE].astype(BF16), kv[:, LANE:].astype(BF16), q2, bias)
        m_new = jnp.maximum(m, m_b)
        w_old = jnp.exp2(m - m_new)
        w_blk = jnp.exp2(m_b - m_new)
        return m_new, w_old * l + w_blk * l_b, w_old * acc + w_blk * pv_b

    init = (jnp.full((1, 2 * cols), NEG, F32), jnp.zeros((1, 2 * cols), F32), jnp.zeros((d, 2 * cols), F32))
    _, l_s, acc_s = lax.fori_loop(0, (qi * qb + qb + kb - 1) // kb, sel_body, init)
    o_s = acc_s * (1.0 / l_s)

    out_rows = []
    for hd in range(heads):
        cs = slice(hd * qb, (hd + 1) * qb)
        out_rows.append(gate_t[3 * hd:3 * hd + 1] * o_c[:, cs] + gate_t[3 * hd + 1:3 * hd + 2] * o_s[:, cs]
                        + gate_t[3 * hd + 2:3 * hd + 3] * o_w[:, cs])
    o_ref[0] = jnp.concatenate(out_rows, axis=0).T


def _nsa_attention(proj, kvc):
    bsz, seq, _ = proj.shape
    qb = Q_BLOCK
    ncp = kvc.shape[1]
    kv = 2 * LANE
    return pl.pallas_call(
        functools.partial(_nsa_kernel, seq=seq),
        grid=(bsz, seq // qb),
        in_specs=[pl.BlockSpec((1, qb, NSA_DIM), lambda b, i: (b, i, COL_NQ // NSA_DIM)),
                  pl.BlockSpec((1, qb, LANE), lambda b, i: (b, i, COL_GATE // LANE)),
                  pl.BlockSpec((1, ncp, kv), lambda b, i: (b, 0, 0)),
                  pl.BlockSpec((1, seq, kv), lambda b, i: (b, 0, COL_SEL // kv)),
                  pl.BlockSpec((1, seq, kv), lambda b, i: (b, 0, COL_WIN // kv))],
        out_specs=pl.BlockSpec((1, qb, NSA_DIM), lambda b, i: (b, i, 0)),
        out_shape=jax.ShapeDtypeStruct((bsz, seq, NSA_DIM), F32),
        scratch_shapes=[pltpu.VMEM((seq, NSA_KV_HEADS * qb), F32)],
        compiler_params=_cparams(("parallel", "arbitrary")),
    )(proj, proj, kvc, proj, proj)


_CC_HALO = 32
_CC_ROWS = 64


def _cconv_kernel(u_ref, w_ref, b_ref, lg_ref, lb_ref, o_ref, xbuf, shifted):
    t = u_ref.shape[1]
    s = pl.program_id(1)

    @pl.when(s == 0)
    def _():
        xbuf[0:_CC_HALO, :] = jnp.zeros((_CC_HALO, CONV_CH), F32)

    @pl.when(s != 0)
    def _():
        xbuf[0:_CC_HALO, :] = xbuf[t:t + _CC_HALO, :]

    u = u_ref[0]
    xbuf[_CC_HALO:_CC_HALO + t, :] = u[:, :CONV_CH] * _sigmoid(u[:, CONV_CH:])
    w = w_ref[...]
    first = _CC_HALO - (CONV_WIDTH - 1)
    rows_kept = t + _CC_HALO - SUBLANE
    for res in range(1, SUBLANE):
        shifted[res - 1, 0:rows_kept, :] = xbuf[pl.ds(res, rows_kept), :]
    for r in range(t // _CC_ROWS):
        acc = jnp.broadcast_to(b_ref[...], (_CC_ROWS, CONV_CH))
        for j in range(CONV_WIDTH):
            res = (first + j) % SUBLANE
            base = r * _CC_ROWS + first + j - res
            tap = xbuf[base:base + _CC_ROWS, :] if res == 0 else shifted[res - 1, base:base + _CC_ROWS, :]
            acc = acc + w[j:j + 1] * tap
        mu = jnp.mean(acc, axis=-1, keepdims=True)
        var = jnp.mean(jnp.square(acc - mu), axis=-1, keepdims=True)
        hn = (acc - mu) * lax.rsqrt(var + EPS) * lg_ref[...] + lb_ref[...]
        o_ref[0, r * _CC_ROWS:(r + 1) * _CC_ROWS, :] = _silu(hn)


def _cconv(proj, dw_w, dw_b, ln_g, ln_b):
    bsz, seq, _ = proj.shape
    t = min(seq, 512)
    wu = 2 * CONV_CH
    vec = pl.BlockSpec((1, CONV_CH), lambda b, s: (0, 0))
    return pl.pallas_call(
        _cconv_kernel,
        grid=(bsz, seq // t),
        in_specs=[pl.BlockSpec((1, t, wu), lambda b, s: (b, s, COL_CU // wu)),
                  pl.BlockSpec((CONV_WIDTH, CONV_CH), lambda b, s: (0, 0)),
                  vec, vec, vec],
        out_specs=pl.BlockSpec((1, t, CONV_CH), lambda b, s: (b, s, 0)),
        out_shape=jax.ShapeDtypeStruct((bsz, seq, CONV_CH), F32),
        scratch_shapes=[pltpu.VMEM((t + _CC_HALO, CONV_CH), F32),
                        pltpu.VMEM((SUBLANE - 1, t + _CC_HALO, CONV_CH), F32)],
        compiler_params=_cparams(("parallel", "arbitrary")),
    )(proj, dw_w, dw_b, ln_g, ln_b)


_FFN_TILE = 256


def _ffn_kernel(x_ref, oa_ref, ob_ref, oc_ref, wo_ref, gmix_ref, gpre_ref, wup_ref, cw_ref, wd_ref, gpost_ref,
                o_ref, xn_ref, act_ref, stage_ref, carry_ref):
    tm = x_ref.shape[1]
    d_ff = wd_ref.shape[0]
    tf = _FFN_TILE
    hal = SUBLANE

    @pl.when(pl.program_id(1) == 0)
    def _():
        carry_ref[...] = jnp.zeros_like(carry_ref)

    mix = (jnp.dot(oa_ref[0].astype(BF16), wo_ref[:GDN_DIM], preferred_element_type=F32)
           + jnp.dot(ob_ref[0].astype(BF16), wo_ref[GDN_DIM:GDN_DIM + NSA_DIM], preferred_element_type=F32)
           + jnp.dot(oc_ref[0].astype(BF16), wo_ref[GDN_DIM + NSA_DIM:], preferred_element_type=F32))
    x = x_ref[0] + _rms(mix, gmix_ref[...])
    xn_ref[...] = _rms(x, gpre_ref[...]).astype(BF16)
    for f in range(d_ff // tf):
        ys = []
        for part in range(2):
            cols = slice(part * d_ff + f * tf, part * d_ff + (f + 1) * tf)
            h = jnp.dot(xn_ref[...], wup_ref[:, cols], preferred_element_type=F32)
            stage = stage_ref.at[f % 2, part]
            stage[0:hal, :] = carry_ref[:, cols]
            stage[hal:hal + tm, :] = h
            carry_ref[:, cols] = h[tm - hal:tm, :]
            cw = cw_ref[:, cols]
            ys.append(cw[2:3] * h + cw[1:2] * stage[pl.ds(hal - 1, tm), :]
                      + cw[0:1] * stage[pl.ds(hal - 2, tm), :])
        act_ref[:, f * tf:(f + 1) * tf] = (_silu(ys[0]) * ys[1]).astype(BF16)
    out = jnp.dot(act_ref[...], wd_ref[...], preferred_element_type=F32)
    o_ref[0] = x + _rms(out, gpost_ref[...])


def _outproj_ffn(x, o_a, o_b, o_c, layer, w_out, g_mix, g_pre, w_up, conv_w, w_down, g_post):
    bsz, seq, d = x.shape
    d_ff = w_down.shape[1]
    tm = min(seq, 512)
    kw = conv_w.shape[1]
    resident = lambda shape: pl.BlockSpec((None,) + shape, lambda b, i: (layer, 0, 0),
                                          pipeline_mode=pl.Buffered(1))
    rows = lambda width: pl.BlockSpec((1, tm, width), lambda b, i: (b, i, 0))
    return pl.pallas_call(
        _ffn_kernel,
        grid=(bsz, seq // tm),
        in_specs=[rows(d), rows(GDN_DIM), rows(NSA_DIM), rows(CONV_CH),
                  resident((d, d)),
                  pl.BlockSpec((1, d), lambda b, i: (0, 0)),
                  pl.BlockSpec((1, d), lambda b, i: (0, 0)),
                  resident((d, 2 * d_ff)),
                  resident((kw, 2 * d_ff)),
                  resident((d_ff, d)),
                  pl.BlockSpec((1, d), lambda b, i: (0, 0))],
        out_specs=pl.BlockSpec((1, tm, d), lambda b, i: (b, i, 0)),
        out_shape=jax.ShapeDtypeStruct((bsz, seq, d), F32),
        scratch_shapes=[pltpu.VMEM((tm, d), BF16),
                        pltpu.VMEM((tm, d_ff), BF16),
                        pltpu.VMEM((2, 2, tm + SUBLANE, _FFN_TILE), F32),
                        pltpu.VMEM((SUBLANE, 2 * d_ff), F32)],
        compiler_params=_cparams(("parallel", "arbitrary")),
    )(x, o_a, o_b, o_c, w_out, g_mix, g_pre, w_up, conv_w, w_down, g_post)


def _pack_w_in(w_in):
    depth, d, _ = w_in.shape
    sizes = (GDN_DIM,) * 4 + (GDN_HEADS,) * 2 + (NSA_DIM,) + (LANE,) * 6 + (3 * NSA_HEADS, 2 * CONV_CH)
    offs = np.concatenate([[0], np.cumsum(sizes)])
    w_in = w_in.astype(BF16)
    piece = lambda k: w_in[:, :, offs[k]:offs[k + 1]]
    zeros = lambda n: jnp.zeros((depth, d, n), BF16)
    gq, gk, gv, gz, ga, gb, nq, nkc, nvc, nks, nvs, nkw, nvw, ngate, cu = [piece(k) for k in range(15)]
    cols = [gq, gk, gv, gz, nq,
            ga, gb, zeros(LANE - 2 * GDN_HEADS),
            cu, nks, nvs, nkw, nvw,
            ngate, zeros(LANE - 3 * NSA_HEADS),
            nkc, nvc]
    packed = jnp.concatenate(cols, axis=-1)
    assert packed.shape[-1] == PROJ_DIM
    return packed


def _pack_compress(wk, wv, pe_k, pe_v):
    depth = wk.shape[0]
    half = CMP_LEN // 2
    d = HEAD_DIM
    w4 = jnp.stack([wk, wk, wv, wv], axis=2).astype(BF16)
    same_slot = jnp.eye(4, dtype=BF16)[None, None, :, None, :, None]
    big = (w4[:, :, :, :, None, :] * same_slot).reshape(depth, 2, half * 4 * d, 4 * d)
    pe = jnp.concatenate([pe_k, pe_k, pe_v, pe_v], axis=-1).reshape(depth, 2, 1, half * 4 * d)
    return big[:, 0], big[:, 1], pe[:, 0], pe[:, 1]


def kernel(x, positions, norm_mix_pre, norm_mix_post, norm_ffn_pre, norm_ffn_post, w_in, w_out, gdn_conv_w, gdn_a_log, gdn_dt_bias, gdn_norm_g, nsa_cmp_wk, nsa_cmp_wv, nsa_cmp_pe_k, nsa_cmp_pe_v, cc_dw_w, cc_dw_b, cc_ln_g, cc_ln_b, ffn_w_up, ffn_conv_w, ffn_w_down):
    bsz, seq, d = x.shape
    depth = w_in.shape[0]
    assert seq % Q_BLOCK == 0 and d == GDN_DIM + NSA_DIM + CONV_CH
    m = bsz * seq

    cos, sin = _rope_tables(positions)
    ncp = seq // CMP_STRIDE
    pad_rows = lambda t: jnp.pad(t[:, CMP_LEN - 1::CMP_STRIDE], ((0, 0), (0, 1), (0, 0)))
    cos_c, sin_c = pad_rows(cos), pad_rows(sin)

    w_in_p = _pack_w_in(w_in)
    w_out_b = w_out.astype(BF16)
    w_up_b = ffn_w_up.astype(BF16)
    w_down_b = ffn_w_down.astype(BF16)
    head_params = jnp.zeros((depth, SUBLANE, LANE), F32)
    head_params = head_params.at[:, 0, :GDN_HEADS].set(gdn_a_log).at[:, 1, :GDN_HEADS].set(gdn_dt_bias)

    wa, wb, pea, peb = _pack_compress(nsa_cmp_wk, nsa_cmp_wv, nsa_cmp_pe_k, nsa_cmp_pe_v)
    cos2d = cos.reshape(m, LANE)
    sin2d = sin.reshape(m, LANE)
    for l in range(depth):
        proj, cmp_cols = _inproj(x.reshape(m, d), norm_mix_pre[l][None], w_in_p, l, cos2d, sin2d)
        proj = proj.reshape(bsz, seq, PROJ_OUT)
        o_a = _gdn(proj, gdn_conv_w[l], head_params[l], jnp.tile(gdn_norm_g[l], GDN_HEADS)[None])
        kvc = _compress(cmp_cols.reshape(bsz, ncp, CMP_STRIDE * 2 * LANE), l, wa, wb, pea, peb, cos_c, sin_c)
        o_b = _nsa_attention(proj, kvc)
        o_c = _cconv(proj, cc_dw_w[l], cc_dw_b[l][None], cc_ln_g[l][None], cc_ln_b[l][None])
        x = _outproj_ffn(x, o_a, o_b, o_c, l, w_out_b, norm_mix_post[l][None], norm_ffn_pre[l][None],
                         w_up_b, ffn_conv_w, w_down_b, norm_ffn_post[l][None])
    return x
```

```python
import functools

import jax
import jax.numpy as jnp
import numpy as np
from jax import lax
from jax.experimental import pallas as pl
from jax.experimental.pallas import tpu as pltpu

F32 = jnp.float32
BF16 = jnp.bfloat16

HEAD_DIM = 64
GDN_HEADS = 6
GDN_DIM = GDN_HEADS * HEAD_DIM
GDN_CONV = 4
GDN_CHUNK = 64
NSA_HEADS = 6
NSA_KV_HEADS = 2
NSA_GROUP = NSA_HEADS // NSA_KV_HEADS
NSA_DIM = NSA_HEADS * HEAD_DIM
CMP_STRIDE = 16
CMP_LEN = 32
SEL_BLOCK = 64
SEL_TOPK = 8
WINDOW = 512
Q_BLOCK = 128
CONV_CH = 256
CONV_WIDTH = 31
ROPE_THETA = 10000.0
EPS = 1e-6
NEG = -1e30
FORCE = 1e4
_LOG2_E = 1.4426950408889634

LANE = 128
SUBLANE = 8
VMEM_LIMIT = 56 * 1024 * 1024

COL_QKV = 0
COL_Z = 1152
COL_NQ = 1536
COL_GAB = 1920
COL_CU = 2048
COL_SEL = 2560
COL_WIN = 2816
COL_GATE = 3072
PROJ_OUT = 3200
COL_CMP = 3200
PROJ_DIM = 3456


def _cparams(sem):
    return pltpu.CompilerParams(dimension_semantics=sem, vmem_limit_bytes=VMEM_LIMIT)


def _sigmoid(x):
    return 1.0 / (1.0 + jnp.exp(-x))


def _silu(x):
    return x * _sigmoid(x)


def _dot(a, b):
    return jnp.dot(a.astype(BF16), b.astype(BF16), preferred_element_type=F32)


def _dot_nt(a, b):
    return lax.dot_general(a.astype(BF16), b.astype(BF16), (((1,), (1,)), ((), ())),
                           preferred_element_type=F32)


def _dot_split3_rhs(sel, x):
    hi = x.astype(BF16)
    r1 = x - hi.astype(F32)
    mid = r1.astype(BF16)
    lo = (r1 - mid.astype(F32)).astype(BF16)
    return (jnp.dot(sel, hi, preferred_element_type=F32) + jnp.dot(sel, mid, preferred_element_type=F32)
            + jnp.dot(sel, lo, preferred_element_type=F32))


def _dot_split(x, sel):
    hi = x.astype(BF16)
    lo = (x - hi.astype(F32)).astype(BF16)
    return (jnp.dot(hi, sel, preferred_element_type=F32) + jnp.dot(lo, sel, preferred_element_type=F32))


def _rms(x, g):
    return x * lax.rsqrt(jnp.mean(x * x, axis=-1, keepdims=True) + EPS) * g


def _rope_table_kernel(pos_ref, inv_ref, sign_ref, cos_ref, sin_ref):
    ang = pos_ref[0].astype(F32) * inv_ref[...]
    cos_ref[0] = jnp.cos(ang)
    sin_ref[0] = jnp.sin(ang) * sign_ref[...]


def _rope_tables(positions):
    bsz, seq = positions.shape
    t = min(seq, 512)
    inv = 1.0 / (ROPE_THETA ** (jnp.arange(0, HEAD_DIM, 2, dtype=F32) / HEAD_DIM))
    inv = jnp.tile(inv, LANE // (HEAD_DIM // 2))[None, :]
    sign = jnp.tile(jnp.concatenate([-jnp.ones(HEAD_DIM // 2, F32), jnp.ones(HEAD_DIM // 2, F32)]),
                    LANE // HEAD_DIM)[None, :]
    out = jax.ShapeDtypeStruct((bsz, seq, LANE), F32)
    return pl.pallas_call(
        _rope_table_kernel,
        grid=(bsz, seq // t),
        in_specs=[pl.BlockSpec((1, t, 1), lambda b, s: (b, s, 0)),
                  pl.BlockSpec((1, LANE), lambda b, s: (0, 0)),
                  pl.BlockSpec((1, LANE), lambda b, s: (0, 0))],
        out_specs=[pl.BlockSpec((1, t, LANE), lambda b, s: (b, s, 0))] * 2,
        out_shape=[out, out],
        compiler_params=_cparams(("parallel", "parallel")),
    )(positions[:, :, None], inv, sign)


def _rope(x, cos, sin):
    lane = lax.broadcasted_iota(jnp.int32, x.shape, 1)
    first_half = (lane % HEAD_DIM) < (HEAD_DIM // 2)
    partner = jnp.where(first_half, pltpu.roll(x, LANE - HEAD_DIM // 2, 1), pltpu.roll(x, HEAD_DIM // 2, 1))
    return x * cos + partner * sin


def _inproj_kernel(x_ref, g_ref, w_ref, cos_ref, sin_ref, o_ref, cmp_ref):
    xn = _rms(x_ref[...], g_ref[...]).astype(BF16)
    y = jnp.dot(xn, w_ref[...], preferred_element_type=F32)
    cos = cos_ref[...]
    sin = sin_ref[...]
    scale = HEAD_DIM ** -0.5 * _LOG2_E
    o_ref[:, :COL_NQ] = y[:, :COL_NQ]
    for c in range(COL_NQ, COL_NQ + NSA_DIM, LANE):
        o_ref[:, c:c + LANE] = _rope(y[:, c:c + LANE], cos, sin) * scale
    o_ref[:, COL_GAB:COL_SEL] = y[:, COL_GAB:COL_SEL]
    for c in (COL_SEL, COL_WIN):
        o_ref[:, c:c + LANE] = _rope(y[:, c:c + LANE], cos, sin)
        o_ref[:, c + LANE:c + 2 * LANE] = y[:, c + LANE:c + 2 * LANE]
    o_ref[:, COL_GATE:PROJ_OUT] = y[:, COL_GATE:PROJ_OUT]
    cmp_ref[...] = y[:, COL_CMP:]


def _inproj(x2d, g, w_all, layer, cos2d, sin2d):
    m, d = x2d.shape
    n = w_all.shape[2]
    tm = min(m, 512)
    row = lambda width: pl.BlockSpec((tm, width), lambda i: (i, 0))
    return pl.pallas_call(
        _inproj_kernel,
        grid=(m // tm,),
        in_specs=[row(d),
                  pl.BlockSpec((1, d), lambda i: (0, 0)),
                  pl.BlockSpec((None, d, n), lambda i: (layer, 0, 0), pipeline_mode=pl.Buffered(1)),
                  row(LANE), row(LANE)],
        out_specs=[row(PROJ_OUT), row(n - PROJ_OUT)],
        out_shape=[jax.ShapeDtypeStruct((m, PROJ_OUT), F32), jax.ShapeDtypeStruct((m, n - PROJ_OUT), F32)],
        compiler_params=_cparams(("parallel",)),
    )(x2d, g, w_all, cos2d, sin2d)


_GDN_PAIR = 2 * GDN_CHUNK


def _gdn_kernel(qkv_ref, z_ref, gab_ref, cw_ref, hp_ref, ng_ref, o_ref, xbuf, state):
    c = GDN_CHUNK
    d = HEAD_DIM
    pr = _GDN_PAIR
    t = qkv_ref.shape[1]
    s = pl.program_id(1)

    @pl.when(s == 0)
    def _():
        xbuf[0:SUBLANE, :] = jnp.zeros((SUBLANE, 3 * GDN_DIM), F32)
        state[...] = jnp.zeros_like(state)

    @pl.when(s != 0)
    def _():
        xbuf[0:SUBLANE, :] = xbuf[t:t + SUBLANE, :]

    xbuf[SUBLANE:SUBLANE + t, :] = qkv_ref[0]
    cw = cw_ref[...]
    hp = hp_ref[...]

    lane_h = lax.broadcasted_iota(jnp.int32, (LANE, GDN_DIM), 0)
    col_h = lax.broadcasted_iota(jnp.int32, (LANE, GDN_DIM), 1) // d
    expand_a = (lane_h == col_h).astype(BF16)
    expand_b = (lane_h == col_h + GDN_HEADS).astype(BF16)
    same_head = (lax.broadcasted_iota(jnp.int32, (GDN_DIM, GDN_DIM), 0) // d
                 == lax.broadcasted_iota(jnp.int32, (GDN_DIM, GDN_DIM), 1) // d).astype(BF16)
    in_chunk = lax.broadcasted_iota(jnp.int32, (pr, LANE), 0) % c
    first = (lax.broadcasted_iota(jnp.int32, (pr, GDN_DIM), 1) % LANE) < d
    a = {}

    def prep_conv(p):
        r0 = SUBLANE + p * pr
        y = cw[GDN_CONV - 1:GDN_CONV] * xbuf[pl.ds(r0, pr), :]
        for j in range(GDN_CONV - 2, -1, -1):
            y = y + cw[j:j + 1] * xbuf[pl.ds(r0 - (GDN_CONV - 1) + j, pr), :]
        a[p] = {"y": _silu(y)}

    def prep_gates(p):
        gab = gab_ref[0, p * pr:(p + 1) * pr, :]
        sp_in = gab + hp[1:2]
        softplus = jnp.maximum(sp_in, 0.0) + jnp.log(1.0 + jnp.exp(-jnp.abs(sp_in)))
        gcum = -jnp.exp(hp[0:1]) * softplus
        shift = 1
        while shift < c:
            gcum = gcum + jnp.where(in_chunk >= shift, pltpu.roll(gcum, shift, 0), 0.0)
            shift *= 2
        g_last = jnp.concatenate(
            [jnp.broadcast_to(gcum[(i + 1) * c - 1:(i + 1) * c, :], (c, LANE)) for i in range(pr // c)], axis=0)
        eg = jnp.exp(gcum)
        a[p].update(gcum=gcum, gcum_t=gcum.T, eg=eg,
                    beta_e=_dot_split(_sigmoid(gab), expand_b),
                    eg_e=_dot_split(eg, expand_a),
                    kdec_e=_dot_split(jnp.exp(g_last - gcum), expand_a))

    def prep_qkv(p):
        ap = a[p]
        y = ap.pop("y")
        q = y[:, :GDN_DIM]
        k = y[:, GDN_DIM:2 * GDN_DIM]
        v = y[:, 2 * GDN_DIM:]
        q = q * (lax.rsqrt(_dot_split(q * q, same_head) + EPS) * (d ** -0.5))
        k = k * lax.rsqrt(_dot_split(k * k, same_head) + EPS)
        kbeta = k * ap["beta_e"]
        ap.update(k16=k.astype(BF16),
                  q16=[jnp.where(first, q, 0.0).astype(BF16), jnp.where(first, 0.0, q).astype(BF16)],
                  kb16=[jnp.where(first, kbeta, 0.0).astype(BF16), jnp.where(first, 0.0, kbeta).astype(BF16)],
                  vb=v * ap["beta_e"], kbe=kbeta * ap["eg_e"],
                  qe16=(q * ap["eg_e"]).astype(BF16), kd16=(k * ap["kdec_e"]).astype(BF16))

    row = lax.broadcasted_iota(jnp.int32, (pr, pr), 0)
    col = lax.broadcasted_iota(jnp.int32, (pr, pr), 1)
    same_chunk = (row // c) == (col // c)
    tri = same_chunk & (row >= col)
    tri_strict = same_chunk & (row > col)
    low = lax.broadcasted_iota(jnp.int32, (pr, LANE), 1) < d
    low_c = lax.broadcasted_iota(jnp.int32, (c, LANE), 1) < d
    n_pair = t // pr
    heads = range(GDN_HEADS)
    per = pr // c
    lanes_of = lambda h: slice((h // 2) * LANE, (h // 2 + 1) * LANE)
    sol, pw, qk = {}, {}, {}
    st = [state[h] for h in heads]
    zeros = jnp.zeros((c, LANE), BF16)
    ng = ng_ref[...]

    def setup_steps(p):
        def step(j):
            ap = a[p]
            g = lanes_of(2 * j)
            lhs = jnp.concatenate([t16[:, g] for t16 in ap["kb16"] + ap["q16"]], axis=0)
            prod = _dot_nt(lhs, ap["k16"][:, g])
            ke_sw = pltpu.roll(ap["kbe"][:, g], d, 1)
            vb = ap["vb"][:, g]
            for i, h in enumerate((2 * j, 2 * j + 1)):
                decay = jnp.exp(jnp.where(tri, ap["gcum"][:, h:h + 1] - ap["gcum_t"][h:h + 1, :], NEG))
                pw[h, p] = jnp.where(tri_strict, -(prod[i * pr:(i + 1) * pr] * decay), 0.0).astype(BF16)
                qk[h, p] = (prod[(2 + i) * pr:(3 + i) * pr] * decay).astype(BF16)
                sol[h, p] = jnp.where(low, vb, ke_sw) if i == 0 else jnp.where(low, ke_sw, vb)

        return ([functools.partial(fn, p) for fn in (prep_conv, prep_gates, prep_qkv)]
                + [functools.partial(step, j) for j in range(GDN_HEADS // 2)])

    def solve_steps(p):
        def apply():
            for h in heads:
                sol[h, p] = sol[h, p] + _dot(pw[h, p], sol[h, p])

        def square():
            for h in heads:
                pw[h, p] = _dot(pw[h, p], pw[h, p]).astype(BF16)

        return [apply] + [square, apply] * 5

    def finish_steps(p):
        ktuw, s_before, v_new, q_st = {}, {}, {}, {}
        rows = lambda ic: slice(ic * c, (ic + 1) * c)

        def operators(ic):
            for h in heads:
                full = lax.dot_general(a[p]["kd16"][rows(ic), lanes_of(h)], sol[h, p][rows(ic)].astype(BF16),
                                       (((0,), (0,)), ((), ())), preferred_element_type=F32)
                ktuw[h, ic] = full[:c] if h % 2 == 0 else full[c:]

        def recur(ic):
            last = (ic + 1) * c - 1
            for h in heads:
                even = h % 2 == 0
                s16 = st[h].astype(BF16)
                s_before[h, ic] = s16
                s_w = jnp.concatenate([zeros, s16] if even else [s16, zeros], axis=0)
                el = a[p]["eg"][last:last + 1, h:h + 1]
                st[h] = jnp.where(low_c if even else ~low_c,
                                  st[h] * el + ktuw[h, ic] - _dot(ktuw[h, ic], s_w), 0.0)

        def read_state(ic):
            for h in heads:
                even = h % 2 == 0
                s16 = s_before[h, ic]
                s_w = jnp.concatenate([zeros, s16] if even else [s16, zeros], axis=0)
                s_q = jnp.concatenate([s16, zeros] if even else [zeros, s16], axis=0)
                sol_c = sol[h, p][rows(ic)]
                v_new[h, ic] = sol_c - _dot(sol_c, s_w)
                q_st[h, ic] = _dot(a[p]["qe16"][rows(ic), lanes_of(h)], s_q)

        def outputs():
            r = slice(p * pr, (p + 1) * pr)
            groups = []
            for j in range(GDN_HEADS // 2):
                halves = []
                for h in (2 * j, 2 * j + 1):
                    vn_pair = jnp.concatenate([v_new[h, ic] for ic in range(per)], axis=0)
                    qs_pair = jnp.concatenate([q_st[h, ic] for ic in range(per)], axis=0)
                    halves.append(qs_pair + _dot(qk[h, p], vn_pair))
                groups.append(jnp.where(low, halves[0], halves[1]))
            o = jnp.concatenate(groups, axis=-1)
            ms = _dot_split(o * o, same_head) * (1.0 / d)
            o_ref[0, r, :] = o * lax.rsqrt(ms + EPS) * ng * _silu(z_ref[0, r, :])

        return ([functools.partial(operators, ic) for ic in range(per)]
                + [functools.partial(recur, ic) for ic in range(per)]
                + [functools.partial(read_state, ic) for ic in range(per)] + [outputs])

    for slot in range(n_pair + 2):
        lists = [steps(q) for steps, q in ((solve_steps, slot - 1), (setup_steps, slot), (finish_steps, slot - 2))
                 if 0 <= q < n_pair]
        merged = sorted(((i + 0.5) / len(lst), k, i, fn) for k, lst in enumerate(lists) for i, fn in enumerate(lst))
        for _, _, _, fn in merged:
            fn()
    for h in heads:
        state[h] = st[h]


def _gdn(proj, conv_w, head_params, norm_g):
    bsz, seq, _ = proj.shape
    t = min(seq, 512)
    w_qkv = 3 * GDN_DIM
    return pl.pallas_call(
        _gdn_kernel,
        grid=(bsz, seq // t),
        in_specs=[pl.BlockSpec((1, t, w_qkv), lambda b, s: (b, s, COL_QKV // w_qkv)),
                  pl.BlockSpec((1, t, GDN_DIM), lambda b, s: (b, s, COL_Z // GDN_DIM)),
                  pl.BlockSpec((1, t, LANE), lambda b, s: (b, s, COL_GAB // LANE)),
                  pl.BlockSpec((GDN_CONV, w_qkv), lambda b, s: (0, 0)),
                  pl.BlockSpec((SUBLANE, LANE), lambda b, s: (0, 0)),
                  pl.BlockSpec((1, GDN_DIM), lambda b, s: (0, 0))],
        out_specs=pl.BlockSpec((1, t, GDN_DIM), lambda b, s: (b, s, 0)),
        out_shape=jax.ShapeDtypeStruct((bsz, seq, GDN_DIM), F32),
        scratch_shapes=[pltpu.VMEM((t + SUBLANE, w_qkv), F32),
                        pltpu.VMEM((GDN_HEADS, HEAD_DIM, LANE), F32)],
        compiler_params=_cparams(("parallel", "arbitrary")),
    )(proj, proj, proj, conv_w, head_params, norm_g)


def _cmp_kernel(c_ref, wa_ref, wb_ref, pea_ref, peb_ref, cos_ref, sin_ref, o_ref):
    cb = c_ref[0]
    ya = _dot(cb + pea_ref[...], wa_ref[...])
    yb = _dot(cb + peb_ref[...], wb_ref[...])
    n = ya.shape[0]
    y = ya + pltpu.roll(yb, n - 1, 0)
    kc = _rope(y[:, :LANE], cos_ref[0], sin_ref[0])
    o_ref[0] = jnp.concatenate([kc, y[:, LANE:]], axis=-1)


def _compress(cmp_rows, layer, wa, wb, pea, peb, cos_c, sin_c):
    bsz, n, width = cmp_rows.shape
    of_layer = lambda arr: pl.BlockSpec((None,) + arr.shape[1:], lambda b: (layer, 0, 0))
    return pl.pallas_call(
        _cmp_kernel,
        grid=(bsz,),
        in_specs=[pl.BlockSpec((1, n, width), lambda b: (b, 0, 0)),
                  of_layer(wa), of_layer(wb), of_layer(pea), of_layer(peb),
                  pl.BlockSpec((1, n, LANE), lambda b: (b, 0, 0)),
                  pl.BlockSpec((1, n, LANE), lambda b: (b, 0, 0))],
        out_specs=pl.BlockSpec((1, n, 2 * LANE), lambda b: (b, 0, 0)),
        out_shape=jax.ShapeDtypeStruct((bsz, n, 2 * LANE), F32),
        compiler_params=_cparams(("parallel",)),
    )(cmp_rows, wa, wb, pea, peb, cos_c, sin_c)


_NSA_KEY_BLOCK = 512
_NSA_Q_TILE = 256


def _attend(k16, v16, q2, bias):
    d = HEAD_DIM
    half = q2.shape[1] // NSA_KV_HEADS
    s = jnp.dot(k16, q2, preferred_element_type=F32) + bias
    m = jnp.max(s, axis=0, keepdims=True)
    p = jnp.exp2(s - m)
    l = jnp.sum(p, axis=0, keepdims=True)
    pv = lax.dot_general(v16, p.astype(BF16), (((0,), (0,)), ((), ())), preferred_element_type=F32)
    return m, l, jnp.concatenate([pv[:d, :half], pv[d:, half:]], axis=1), p


def _nsa_kernel(q_ref, gate_ref, kvc_ref, ksel_ref, kwin_ref, o_ref, mask_ref, *, seq):
    qb = q_ref.shape[1]
    d = HEAD_DIM
    grp = NSA_GROUP
    cols = grp * qb
    n_slc = seq // SEL_BLOCK
    topk = min(SEL_TOPK, n_slc)
    ncp = seq // CMP_STRIDE
    qi = pl.program_id(1)
    q_t = q_ref[0].T
    gate_t = _sigmoid(gate_ref[0]).T
    t_q = qi * qb + lax.broadcasted_iota(jnp.int32, (1, qb), 1)

    heads = NSA_KV_HEADS * grp
    tile_all = lambda a: jnp.concatenate([a] * heads, axis=1)
    n_idx = lax.broadcasted_iota(jnp.int32, (ncp, qb), 0)
    cmp_bias = tile_all(jnp.where((n_idx * CMP_STRIDE + (CMP_LEN - 1) <= t_q) & (n_idx < ncp - 1), 0.0, NEG))
    sj = lax.broadcasted_iota(jnp.int32, (n_slc, ncp), 0) * SEL_BLOCK
    ci = lax.broadcasted_iota(jnp.int32, (n_slc, ncp), 1) * CMP_STRIDE
    overlap_t = ((ci < sj + SEL_BLOCK) & (ci + CMP_LEN > sj) & (ci < (ncp - 1) * CMP_STRIDE)).astype(BF16)
    blk = lax.broadcasted_iota(jnp.int32, (n_slc, qb), 0)
    blk_f = blk.astype(F32)
    cur = t_q // SEL_BLOCK
    forced = (blk == 0) | (blk == cur) | (blk == cur - 1)
    kb = min(seq, _NSA_KEY_BLOCK)
    span = min(seq, WINDOW + qb)
    causal_bias = jnp.where(lax.broadcasted_iota(jnp.int32, (qb, qb), 0)
                            <= lax.broadcasted_iota(jnp.int32, (qb, qb), 1), 0.0, NEG)
    w0 = pl.multiple_of(jnp.maximum(qi * qb + qb - span, 0), qb)
    diff = t_q - (w0 + lax.broadcasted_iota(jnp.int32, (span, qb), 0))
    win_bias = tile_all(jnp.where((diff >= 0) & (diff < WINDOW), 0.0, NEG))
    kvc = kvc_ref[0]

    zeros_q = jnp.zeros((d, cols), BF16)
    q_h = [jnp.concatenate([q_t[(h * grp + g) * d:(h * grp + g + 1) * d, :] for g in range(grp)],
                           axis=1).astype(BF16) for h in range(NSA_KV_HEADS)]
    q2 = jnp.concatenate([jnp.concatenate([q_h[0], zeros_q], axis=0),
                          jnp.concatenate([zeros_q, q_h[1]], axis=0)], axis=1)

    m_c, l_c, o_c, p_c = _attend(kvc[:, :LANE].astype(BF16), kvc[:, LANE:].astype(BF16), q2, cmp_bias)
    inv_l = jnp.where(m_c > 0.5 * NEG, 1.0 / l_c, 0.0)
    o_c = o_c * inv_l
    p_c = p_c * inv_l

    kv = kwin_ref[0, pl.ds(w0, span), :]
    _, l_w, o_w, _ = _attend(kv[:, :LANE].astype(BF16), kv[:, LANE:].astype(BF16), q2, win_bias)
    o_w = o_w * (1.0 / l_w)

    for h in range(NSA_KV_HEADS):
        p_sum = p_c[:, h * cols:h * cols + qb]
        for g in range(1, grp):
            p_sum = p_sum + p_c[:, h * cols + g * qb:h * cols + (g + 1) * qb]
        imp = _dot_split3_rhs(overlap_t, p_sum)
        work = jnp.where(forced, FORCE, jnp.where(blk * SEL_BLOCK <= t_q, imp, -1.0))
        sel = jnp.zeros((n_slc, qb), F32)
        for _ in range(topk):
            best = jnp.max(work, axis=0, keepdims=True)
            idx = jnp.min(jnp.where(work == best, blk_f, 1e9), axis=0, keepdims=True)
            pick = blk_f == idx
            sel = jnp.where(pick, 1.0, sel)
            work = jnp.where(pick, -3.0, work)
        sel_bias = jnp.where((sel > 0.5) & (blk * SEL_BLOCK <= t_q), 0.0, NEG)
        for b in range(n_slc):
            mask_ref[b * SEL_BLOCK:(b + 1) * SEL_BLOCK, h * qb:(h + 1) * qb] = jnp.broadcast_to(
                sel_bias[b:b + 1, :], (SEL_BLOCK, qb))
        own_keys = pl.ds(pl.multiple_of(qi * qb, qb), qb)
        mask_ref[own_keys, h * qb:(h + 1) * qb] = mask_ref[own_keys, h * qb:(h + 1) * qb] + causal_bias

    def sel_body(j, carry):
        m, l, acc = carry
        off = pl.multiple_of(j * kb, kb)
        kv = ksel_ref[0, pl.ds(off, kb), :]
        mask = mask_ref[pl.ds(off, kb), :]
        bias = jnp.concatenate([mask[:, :qb]] * grp + [mask[:, qb:]] * grp, axis=1)
        m_b, l_b, pv_b, _ = _attend(kv[:, :LANE].astype(BF16), kv[:, LANE:].astype(BF16), q2, bias)
        m_new = jnp.maximum(m, m_b)
        w_old = jnp.exp2(m - m_new)
        w_blk = jnp.exp2(m_b - m_new)
        return m_new, w_old * l + w_blk * l_b, w_old * acc + w_blk * pv_b

    init = (jnp.full((1, 2 * cols), NEG, F32), jnp.zeros((1, 2 * cols), F32), jnp.zeros((d, 2 * cols), F32))
    _, l_s, acc_s = lax.fori_loop(0, (qi * qb + qb + kb - 1) // kb, sel_body, init)
    o_s = acc_s * (1.0 / l_s)

    out_rows = []
    for hd in range(heads):
        cs = slice(hd * qb, (hd + 1) * qb)
        out_rows.append(gate_t[3 * hd:3 * hd + 1] * o_c[:, cs] + gate_t[3 * hd + 1:3 * hd + 2] * o_s[:, cs]
                        + gate_t[3 * hd + 2:3 * hd + 3] * o_w[:, cs])
    o_ref[0] = jnp.concatenate(out_rows, axis=0).T


def _nsa_attention(proj, kvc):
    bsz, seq, _ = proj.shape
    qb = min(seq, _NSA_Q_TILE)
    ncp = kvc.shape[1]
    kv = 2 * LANE
    return pl.pallas_call(
        functools.partial(_nsa_kernel, seq=seq),
        grid=(bsz, seq // qb),
        in_specs=[pl.BlockSpec((1, qb, NSA_DIM), lambda b, i: (b, i, COL_NQ // NSA_DIM)),
                  pl.BlockSpec((1, qb, LANE), lambda b, i: (b, i, COL_GATE // LANE)),
                  pl.BlockSpec((1, ncp, kv), lambda b, i: (b, 0, 0)),
                  pl.BlockSpec((1, seq, kv), lambda b, i: (b, 0, COL_SEL // kv)),
                  pl.BlockSpec((1, seq, kv), lambda b, i: (b, 0, COL_WIN // kv))],
        out_specs=pl.BlockSpec((1, qb, NSA_DIM), lambda b, i: (b, i, 0)),
        out_shape=jax.ShapeDtypeStruct((bsz, seq, NSA_DIM), F32),
        scratch_shapes=[pltpu.VMEM((seq, NSA_KV_HEADS * qb), F32)],
        compiler_params=_cparams(("parallel", "arbitrary")),
    )(proj, proj, kvc, proj, proj)


_CC_HALO = 32
_CC_ROWS = 64


def _cconv_kernel(u_ref, w_ref, b_ref, lg_ref, lb_ref, o_ref, xbuf, shifted):
    t = u_ref.shape[1]
    s = pl.program_id(1)

    @pl.when(s == 0)
    def _():
        xbuf[0:_CC_HALO, :] = jnp.zeros((_CC_HALO, CONV_CH), F32)

    @pl.when(s != 0)
    def _():
        xbuf[0:_CC_HALO, :] = xbuf[t:t + _CC_HALO, :]

    u = u_ref[0]
    xbuf[_CC_HALO:_CC_HALO + t, :] = u[:, :CONV_CH] * _sigmoid(u[:, CONV_CH:])
    w = w_ref[...]
    first = _CC_HALO - (CONV_WIDTH - 1)
    rows_kept = t + _CC_HALO - SUBLANE
    for res in range(1, SUBLANE):
        shifted[res - 1, 0:rows_kept, :] = xbuf[pl.ds(res, rows_kept), :]
    for r in range(t // _CC_ROWS):
        acc = jnp.broadcast_to(b_ref[...], (_CC_ROWS, CONV_CH))
        for j in range(CONV_WIDTH):
            res = (first + j) % SUBLANE
            base = r * _CC_ROWS + first + j - res
            tap = xbuf[base:base + _CC_ROWS, :] if res == 0 else shifted[res - 1, base:base + _CC_ROWS, :]
            acc = acc + w[j:j + 1] * tap
        mu = jnp.mean(acc, axis=-1, keepdims=True)
        var = jnp.mean(jnp.square(acc - mu), axis=-1, keepdims=True)
        hn = (acc - mu) * lax.rsqrt(var + EPS) * lg_ref[...] + lb_ref[...]
        o_ref[0, r * _CC_ROWS:(r + 1) * _CC_ROWS, :] = _silu(hn)


def _cconv(proj, dw_w, dw_b, ln_g, ln_b):
    bsz, seq, _ = proj.shape
    t = min(seq, 512)
    wu = 2 * CONV_CH
    vec = pl.BlockSpec((1, CONV_CH), lambda b, s: (0, 0))
    return pl.pallas_call(
        _cconv_kernel,
        grid=(bsz, seq // t),
        in_specs=[pl.BlockSpec((1, t, wu), lambda b, s: (b, s, COL_CU // wu)),
                  pl.BlockSpec((CONV_WIDTH, CONV_CH), lambda b, s: (0, 0)),
                  vec, vec, vec],
        out_specs=pl.BlockSpec((1, t, CONV_CH), lambda b, s: (b, s, 0)),
        out_shape=jax.ShapeDtypeStruct((bsz, seq, CONV_CH), F32),
        scratch_shapes=[pltpu.VMEM((t + _CC_HALO, CONV_CH), F32),
                        pltpu.VMEM((SUBLANE - 1, t + _CC_HALO, CONV_CH), F32)],
        compiler_params=_cparams(("parallel", "arbitrary")),
    )(proj, dw_w, dw_b, ln_g, ln_b)


_FFN_TILE = 256


def _ffn_kernel(x_ref, oa_ref, ob_ref, oc_ref, wo_ref, gmix_ref, gpre_ref, wup_ref, cw_ref, wd_ref, gpost_ref,
                o_ref, xn_ref, act_ref, stage_ref, carry_ref):
    tm = x_ref.shape[1]
    d_ff = wd_ref.shape[0]
    tf = _FFN_TILE
    hal = SUBLANE

    @pl.when(pl.program_id(1) == 0)
    def _():
        carry_ref[...] = jnp.zeros_like(carry_ref)

    mix = (jnp.dot(oa_ref[0].astype(BF16), wo_ref[:GDN_DIM], preferred_element_type=F32)
           + jnp.dot(ob_ref[0].astype(BF16), wo_ref[GDN_DIM:GDN_DIM + NSA_DIM], preferred_element_type=F32)
           + jnp.dot(oc_ref[0].astype(BF16), wo_ref[GDN_DIM + NSA_DIM:], preferred_element_type=F32))
    x = x_ref[0] + _rms(mix, gmix_ref[...])
    xn_ref[...] = _rms(x, gpre_ref[...]).astype(BF16)
    for f in range(d_ff // tf):
        ys = []
        for part in range(2):
            cols = slice(part * d_ff + f * tf, part * d_ff + (f + 1) * tf)
            h = jnp.dot(xn_ref[...], wup_ref[:, cols], preferred_element_type=F32)
            stage = stage_ref.at[f % 2, part]
            stage[0:hal, :] = carry_ref[:, cols]
            stage[hal:hal + tm, :] = h
            carry_ref[:, cols] = h[tm - hal:tm, :]
            cw = cw_ref[:, cols]
            ys.append(cw[2:3] * h + cw[1:2] * stage[pl.ds(hal - 1, tm), :]
                      + cw[0:1] * stage[pl.ds(hal - 2, tm), :])
        act_ref[:, f * tf:(f + 1) * tf] = (_silu(ys[0]) * ys[1]).astype(BF16)
    out = jnp.dot(act_ref[...], wd_ref[...], preferred_element_type=F32)
    o_ref[0] = x + _rms(out, gpost_ref[...])


def _outproj_ffn(x, o_a, o_b, o_c, layer, w_out, g_mix, g_pre, w_up, conv_w, w_down, g_post):
    bsz, seq, d = x.shape
    d_ff = w_down.shape[1]
    tm = min(seq, 512)
    kw = conv_w.shape[1]
    resident = lambda shape: pl.BlockSpec((None,) + shape, lambda b, i: (layer, 0, 0),
                                          pipeline_mode=pl.Buffered(1))
    rows = lambda width: pl.BlockSpec((1, tm, width), lambda b, i: (b, i, 0))
    return pl.pallas_call(
        _ffn_kernel,
        grid=(bsz, seq // tm),
        in_specs=[rows(d), rows(GDN_DIM), rows(NSA_DIM), rows(CONV_CH),
                  resident((d, d)),
                  pl.BlockSpec((1, d), lambda b, i: (0, 0)),
                  pl.BlockSpec((1, d), lambda b, i: (0, 0)),
                  resident((d, 2 * d_ff)),
                  resident((kw, 2 * d_ff)),
                  resident((d_ff, d)),
                  pl.BlockSpec((1, d), lambda b, i: (0, 0))],
        out_specs=pl.BlockSpec((1, tm, d), lambda b, i: (b, i, 0)),
        out_shape=jax.ShapeDtypeStruct((bsz, seq, d), F32),
        scratch_shapes=[pltpu.VMEM((tm, d), BF16),
                        pltpu.VMEM((tm, d_ff), BF16),
                        pltpu.VMEM((2, 2, tm + SUBLANE, _FFN_TILE), F32),
                        pltpu.VMEM((SUBLANE, 2 * d_ff), F32)],
        compiler_params=_cparams(("parallel", "arbitrary")),
    )(x, o_a, o_b, o_c, w_out, g_mix, g_pre, w_up, conv_w, w_down, g_post)


def _pack_w_in(w_in):
    depth, d, _ = w_in.shape
    sizes = (GDN_DIM,) * 4 + (GDN_HEADS,) * 2 + (NSA_DIM,) + (LANE,) * 6 + (3 * NSA_HEADS, 2 * CONV_CH)
    offs = np.concatenate([[0], np.cumsum(sizes)])
    w_in = w_in.astype(BF16)
    piece = lambda k: w_in[:, :, offs[k]:offs[k + 1]]
    zeros = lambda n: jnp.zeros((depth, d, n), BF16)
    gq, gk, gv, gz, ga, gb, nq, nkc, nvc, nks, nvs, nkw, nvw, ngate, cu = [piece(k) for k in range(15)]
    cols = [gq, gk, gv, gz, nq,
            ga, gb, zeros(LANE - 2 * GDN_HEADS),
            cu, nks, nvs, nkw, nvw,
            ngate, zeros(LANE - 3 * NSA_HEADS),
            nkc, nvc]
    packed = jnp.concatenate(cols, axis=-1)
    assert packed.shape[-1] == PROJ_DIM
    return packed


def _pack_compress(wk, wv, pe_k, pe_v):
    depth = wk.shape[0]
    half = CMP_LEN // 2
    d = HEAD_DIM
    w4 = jnp.stack([wk, wk, wv, wv], axis=2).astype(BF16)
    same_slot = jnp.eye(4, dtype=BF16)[None, None, :, None, :, None]
    big = (w4[:, :, :, :, None, :] * same_slot).reshape(depth, 2, half * 4 * d, 4 * d)
    pe = jnp.concatenate([pe_k, pe_k, pe_v, pe_v], axis=-1).reshape(depth, 2, 1, half * 4 * d)
    return big[:, 0], big[:, 1], pe[:, 0], pe[:, 1]


def kernel(x, positions, norm_mix_pre, norm_mix_post, norm_ffn_pre, norm_ffn_post, w_in, w_out, gdn_conv_w, gdn_a_log, gdn_dt_bias, gdn_norm_g, nsa_cmp_wk, nsa_cmp_wv, nsa_cmp_pe_k, nsa_cmp_pe_v, cc_dw_w, cc_dw_b, cc_ln_g, cc_ln_b, ffn_w_up, ffn_conv_w, ffn_w_down):
    bsz, seq, d = x.shape
    depth = w_in.shape[0]
    assert seq % Q_BLOCK == 0 and d == GDN_DIM + NSA_DIM + CONV_CH
    m = bsz * seq

    cos, sin = _rope_tables(positions)
    ncp = seq // CMP_STRIDE
    pad_rows = lambda t: jnp.pad(t[:, CMP_LEN - 1::CMP_STRIDE], ((0, 0), (0, 1), (0, 0)))
    cos_c, sin_c = pad_rows(cos), pad_rows(sin)

    w_in_p = _pack_w_in(w_in)
    w_out_b = w_out.astype(BF16)
    w_up_b = ffn_w_up.astype(BF16)
    w_down_b = ffn_w_down.astype(BF16)
    head_params = jnp.zeros((depth, SUBLANE, LANE), F32)
    head_params = head_params.at[:, 0, :GDN_HEADS].set(gdn_a_log).at[:, 1, :GDN_HEADS].set(gdn_dt_bias)

    wa, wb, pea, peb = _pack_compress(nsa_cmp_wk, nsa_cmp_wv, nsa_cmp_pe_k, nsa_cmp_pe_v)
    cos2d = cos.reshape(m, LANE)
    sin2d = sin.reshape(m, LANE)
    for l in range(depth):
        proj, cmp_cols = _inproj(x.reshape(m, d), norm_mix_pre[l][None], w_in_p, l, cos2d, sin2d)
        proj = proj.reshape(bsz, seq, PROJ_OUT)
        o_a = _gdn(proj, gdn_conv_w[l], head_params[l], jnp.tile(gdn_norm_g[l], GDN_HEADS)[None])
        kvc = _compress(cmp_cols.reshape(bsz, ncp, CMP_STRIDE * 2 * LANE), l, wa, wb, pea, peb, cos_c, sin_c)
        o_b = _nsa_attention(proj, kvc)
        o_c = _cconv(proj, cc_dw_w[l], cc_dw_b[l][None], cc_ln_g[l][None], cc_ln_b[l][None])
        x = _outproj_ffn(x, o_a, o_b, o_c, l, w_out_b, norm_mix_post[l][None], norm_ffn_pre[l][None],
                         w_up_b, ffn_conv_w, w_down_b, norm_ffn_post[l][None])
    return x
```

```python
import functools

import jax
import jax.numpy as jnp
import numpy as np
from jax import lax
from jax.experimental import pallas as pl
from jax.experimental.pallas import tpu as pltpu

F32 = jnp.float32
BF16 = jnp.bfloat16

HEAD_DIM = 64
GDN_HEADS = 6
GDN_DIM = GDN_HEADS * HEAD_DIM
GDN_CONV = 4
GDN_CHUNK = 64
NSA_HEADS = 6
NSA_KV_HEADS = 2
NSA_GROUP = NSA_HEADS // NSA_KV_HEADS
NSA_DIM = NSA_HEADS * HEAD_DIM
CMP_STRIDE = 16
CMP_LEN = 32
SEL_BLOCK = 64
SEL_TOPK = 8
WINDOW = 512
Q_BLOCK = 128
CONV_CH = 256
CONV_WIDTH = 31
ROPE_THETA = 10000.0
EPS = 1e-6
NEG = -1e30
FORCE = 1e4
_LOG2_E = 1.4426950408889634

LANE = 128
SUBLANE = 8
VMEM_LIMIT = 56 * 1024 * 1024
ROW_BLOCK = 512

COL_QKV = 0
COL_Z = 1152
COL_NQ = 1536
COL_GAB = 1920
COL_CU = 2048
COL_SEL = 2560
COL_WIN = 2816
COL_GATE = 3072
PROJ_OUT = 3200
COL_CMP = 3200
PROJ_DIM = 3456


def _cparams(sem):
    return pltpu.CompilerParams(dimension_semantics=sem, vmem_limit_bytes=VMEM_LIMIT)


def _sigmoid(x):
    return 1.0 / (1.0 + jnp.exp(-x))


def _silu(x):
    return x * _sigmoid(x)


def _dot(a, b):
    return jnp.dot(a.astype(BF16), b.astype(BF16), preferred_element_type=F32)


def _dot_nt(a, b):
    return lax.dot_general(a.astype(BF16), b.astype(BF16), (((1,), (1,)), ((), ())),
                           preferred_element_type=F32)


def _dot_split3_rhs(sel, x):
    hi = x.astype(BF16)
    r1 = x - hi.astype(F32)
    mid = r1.astype(BF16)
    lo = (r1 - mid.astype(F32)).astype(BF16)
    return (jnp.dot(sel, hi, preferred_element_type=F32) + jnp.dot(sel, mid, preferred_element_type=F32)
            + jnp.dot(sel, lo, preferred_element_type=F32))


def _dot_split(x, sel):
    hi = x.astype(BF16)
    lo = (x - hi.astype(F32)).astype(BF16)
    return (jnp.dot(hi, sel, preferred_element_type=F32) + jnp.dot(lo, sel, preferred_element_type=F32))


def _rms(x, g):
    return x * lax.rsqrt(jnp.mean(x * x, axis=-1, keepdims=True) + EPS) * g


def _rope_table_kernel(pos_ref, inv_ref, sign_ref, cos_ref, sin_ref):
    ang = pos_ref[0].astype(F32) * inv_ref[...]
    cos_ref[0] = jnp.cos(ang)
    sin_ref[0] = jnp.sin(ang) * sign_ref[...]


def _rope_tables(positions):
    bsz, seq = positions.shape
    t = min(seq, ROW_BLOCK)
    inv = 1.0 / (ROPE_THETA ** (jnp.arange(0, HEAD_DIM, 2, dtype=F32) / HEAD_DIM))
    inv = jnp.tile(inv, LANE // (HEAD_DIM // 2))[None, :]
    sign = jnp.tile(jnp.concatenate([-jnp.ones(HEAD_DIM // 2, F32), jnp.ones(HEAD_DIM // 2, F32)]),
                    LANE // HEAD_DIM)[None, :]
    out = jax.ShapeDtypeStruct((bsz, seq, LANE), F32)
    return pl.pallas_call(
        _rope_table_kernel,
        grid=(bsz, seq // t),
        in_specs=[pl.BlockSpec((1, t, 1), lambda b, s: (b, s, 0)),
                  pl.BlockSpec((1, LANE), lambda b, s: (0, 0)),
                  pl.BlockSpec((1, LANE), lambda b, s: (0, 0))],
        out_specs=[pl.BlockSpec((1, t, LANE), lambda b, s: (b, s, 0))] * 2,
        out_shape=[out, out],
        compiler_params=_cparams(("parallel", "parallel")),
    )(positions[:, :, None], inv, sign)


def _rope(x, cos, sin):
    lane = lax.broadcasted_iota(jnp.int32, x.shape, 1)
    first_half = (lane % HEAD_DIM) < (HEAD_DIM // 2)
    partner = jnp.where(first_half, pltpu.roll(x, LANE - HEAD_DIM // 2, 1), pltpu.roll(x, HEAD_DIM // 2, 1))
    return x * cos + partner * sin


def _inproj_kernel(x_ref, g_ref, w_ref, cos_ref, sin_ref, o_ref, cmp_ref):
    xn = _rms(x_ref[...], g_ref[...]).astype(BF16)
    y = jnp.dot(xn, w_ref[...], preferred_element_type=F32)
    cos = cos_ref[...]
    sin = sin_ref[...]
    scale = HEAD_DIM ** -0.5 * _LOG2_E
    o_ref[:, :COL_NQ] = y[:, :COL_NQ]
    for c in range(COL_NQ, COL_NQ + NSA_DIM, LANE):
        o_ref[:, c:c + LANE] = _rope(y[:, c:c + LANE], cos, sin) * scale
    o_ref[:, COL_GAB:COL_SEL] = y[:, COL_GAB:COL_SEL]
    for c in (COL_SEL, COL_WIN):
        o_ref[:, c:c + LANE] = _rope(y[:, c:c + LANE], cos, sin)
        o_ref[:, c + LANE:c + 2 * LANE] = y[:, c + LANE:c + 2 * LANE]
    o_ref[:, COL_GATE:PROJ_OUT] = y[:, COL_GATE:PROJ_OUT]
    cmp_ref[...] = y[:, COL_CMP:]


def _inproj(x2d, g, w_all, layer, cos2d, sin2d):
    m, d = x2d.shape
    n = w_all.shape[2]
    tm = min(m, ROW_BLOCK)
    row = lambda width: pl.BlockSpec((tm, width), lambda i: (i, 0))
    return pl.pallas_call(
        _inproj_kernel,
        grid=(m // tm,),
        in_specs=[row(d),
                  pl.BlockSpec((1, d), lambda i: (0, 0)),
                  pl.BlockSpec((None, d, n), lambda i: (layer, 0, 0), pipeline_mode=pl.Buffered(1)),
                  row(LANE), row(LANE)],
        out_specs=[row(PROJ_OUT), row(n - PROJ_OUT)],
        out_shape=[jax.ShapeDtypeStruct((m, PROJ_OUT), F32), jax.ShapeDtypeStruct((m, n - PROJ_OUT), F32)],
        compiler_params=_cparams(("parallel",)),
    )(x2d, g, w_all, cos2d, sin2d)


_GDN_PAIR = 2 * GDN_CHUNK


def _gdn_kernel(qkv_ref, z_ref, gab_ref, cw_ref, hp_ref, ng_ref, o_ref, xbuf, state):
    c = GDN_CHUNK
    d = HEAD_DIM
    pr = _GDN_PAIR
    t = qkv_ref.shape[1]
    s = pl.program_id(1)

    @pl.when(s == 0)
    def _():
        xbuf[0:SUBLANE, :] = jnp.zeros((SUBLANE, 3 * GDN_DIM), F32)
        state[...] = jnp.zeros_like(state)

    @pl.when(s != 0)
    def _():
        xbuf[0:SUBLANE, :] = xbuf[t:t + SUBLANE, :]

    xbuf[SUBLANE:SUBLANE + t, :] = qkv_ref[0]
    cw = cw_ref[...]
    hp = hp_ref[...]

    lane_h = lax.broadcasted_iota(jnp.int32, (LANE, GDN_DIM), 0)
    col_h = lax.broadcasted_iota(jnp.int32, (LANE, GDN_DIM), 1) // d
    expand_a = (lane_h == col_h).astype(BF16)
    expand_b = (lane_h == col_h + GDN_HEADS).astype(BF16)
    same_head = (lax.broadcasted_iota(jnp.int32, (GDN_DIM, GDN_DIM), 0) // d
                 == lax.broadcasted_iota(jnp.int32, (GDN_DIM, GDN_DIM), 1) // d).astype(BF16)
    in_chunk = lax.broadcasted_iota(jnp.int32, (pr, LANE), 0) % c
    first = (lax.broadcasted_iota(jnp.int32, (pr, GDN_DIM), 1) % LANE) < d
    a = {}

    def prep_conv(p):
        r0 = SUBLANE + p * pr
        y = cw[GDN_CONV - 1:GDN_CONV] * xbuf[pl.ds(r0, pr), :]
        for j in range(GDN_CONV - 2, -1, -1):
            y = y + cw[j:j + 1] * xbuf[pl.ds(r0 - (GDN_CONV - 1) + j, pr), :]
        a[p] = {"y": _silu(y)}

    def prep_gates(p):
        gab = gab_ref[0, p * pr:(p + 1) * pr, :]
        sp_in = gab + hp[1:2]
        softplus = jnp.maximum(sp_in, 0.0) + jnp.log(1.0 + jnp.exp(-jnp.abs(sp_in)))
        gcum = -jnp.exp(hp[0:1]) * softplus
        shift = 1
        while shift < c:
            gcum = gcum + jnp.where(in_chunk >= shift, pltpu.roll(gcum, shift, 0), 0.0)
            shift *= 2
        g_last = jnp.concatenate(
            [jnp.broadcast_to(gcum[(i + 1) * c - 1:(i + 1) * c, :], (c, LANE)) for i in range(pr // c)], axis=0)
        eg = jnp.exp(gcum)
        a[p].update(gcum=gcum, gcum_t=gcum.T, eg=eg,
                    beta_e=_dot_split(_sigmoid(gab), expand_b),
                    eg_e=_dot_split(eg, expand_a),
                    kdec_e=_dot_split(jnp.exp(g_last - gcum), expand_a))

    def prep_qkv(p):
        ap = a[p]
        y = ap.pop("y")
        q = y[:, :GDN_DIM]
        k = y[:, GDN_DIM:2 * GDN_DIM]
        v = y[:, 2 * GDN_DIM:]
        q = q * (lax.rsqrt(_dot_split(q * q, same_head) + EPS) * (d ** -0.5))
        k = k * lax.rsqrt(_dot_split(k * k, same_head) + EPS)
        kbeta = k * ap["beta_e"]
        ap.update(k16=k.astype(BF16),
                  q16=[jnp.where(first, q, 0.0).astype(BF16), jnp.where(first, 0.0, q).astype(BF16)],
                  kb16=[jnp.where(first, kbeta, 0.0).astype(BF16), jnp.where(first, 0.0, kbeta).astype(BF16)],
                  vb=v * ap["beta_e"], kbe=kbeta * ap["eg_e"],
                  qe16=(q * ap["eg_e"]).astype(BF16), kd16=(k * ap["kdec_e"]).astype(BF16))

    row = lax.broadcasted_iota(jnp.int32, (pr, pr), 0)
    col = lax.broadcasted_iota(jnp.int32, (pr, pr), 1)
    same_chunk = (row // c) == (col // c)
    tri = same_chunk & (row >= col)
    tri_strict = same_chunk & (row > col)
    low = lax.broadcasted_iota(jnp.int32, (pr, LANE), 1) < d
    low_c = lax.broadcasted_iota(jnp.int32, (c, LANE), 1) < d
    n_pair = t // pr
    heads = range(GDN_HEADS)
    per = pr // c
    lanes_of = lambda h: slice((h // 2) * LANE, (h // 2 + 1) * LANE)
    sol, pw, qk = {}, {}, {}
    st = [state[h] for h in heads]
    zeros = jnp.zeros((c, LANE), BF16)
    ng = ng_ref[...]

    def setup_steps(p):
        def step(j):
            ap = a[p]
            g = lanes_of(2 * j)
            lhs = jnp.concatenate([t16[:, g] for t16 in ap["kb16"] + ap["q16"]], axis=0)
            prod = _dot_nt(lhs, ap["k16"][:, g])
            ke_sw = pltpu.roll(ap["kbe"][:, g], d, 1)
            vb = ap["vb"][:, g]
            for i, h in enumerate((2 * j, 2 * j + 1)):
                decay = jnp.exp(jnp.where(tri, ap["gcum"][:, h:h + 1] - ap["gcum_t"][h:h + 1, :], NEG))
                pw[h, p] = jnp.where(tri_strict, -(prod[i * pr:(i + 1) * pr] * decay), 0.0).astype(BF16)
                qk[h, p] = (prod[(2 + i) * pr:(3 + i) * pr] * decay).astype(BF16)
                sol[h, p] = jnp.where(low, vb, ke_sw) if i == 0 else jnp.where(low, ke_sw, vb)

        return ([functools.partial(fn, p) for fn in (prep_conv, prep_gates, prep_qkv)]
                + [functools.partial(step, j) for j in range(GDN_HEADS // 2)])

    def solve_steps(p):
        def apply():
            for h in heads:
                sol[h, p] = sol[h, p] + _dot(pw[h, p], sol[h, p])

        def square():
            for h in heads:
                pw[h, p] = _dot(pw[h, p], pw[h, p]).astype(BF16)

        return [apply] + [square, apply] * 5

    def finish_steps(p):
        ktuw, s_before, v_new, q_st = {}, {}, {}, {}
        rows = lambda ic: slice(ic * c, (ic + 1) * c)

        def operators(ic):
            for h in heads:
                full = lax.dot_general(a[p]["kd16"][rows(ic), lanes_of(h)], sol[h, p][rows(ic)].astype(BF16),
                                       (((0,), (0,)), ((), ())), preferred_element_type=F32)
                ktuw[h, ic] = full[:c] if h % 2 == 0 else full[c:]

        def recur(ic):
            last = (ic + 1) * c - 1
            for h in heads:
                even = h % 2 == 0
                s16 = st[h].astype(BF16)
                s_before[h, ic] = s16
                s_w = jnp.concatenate([zeros, s16] if even else [s16, zeros], axis=0)
                el = a[p]["eg"][last:last + 1, h:h + 1]
                st[h] = jnp.where(low_c if even else ~low_c,
                                  st[h] * el + ktuw[h, ic] - _dot(ktuw[h, ic], s_w), 0.0)

        def read_state(ic):
            for h in heads:
                even = h % 2 == 0
                s16 = s_before[h, ic]
                s_w = jnp.concatenate([zeros, s16] if even else [s16, zeros], axis=0)
                s_q = jnp.concatenate([s16, zeros] if even else [zeros, s16], axis=0)
                sol_c = sol[h, p][rows(ic)]
                v_new[h, ic] = sol_c - _dot(sol_c, s_w)
                q_st[h, ic] = _dot(a[p]["qe16"][rows(ic), lanes_of(h)], s_q)

        def outputs():
            r = slice(p * pr, (p + 1) * pr)
            groups = []
            for j in range(GDN_HEADS // 2):
                halves = []
                for h in (2 * j, 2 * j + 1):
                    vn_pair = jnp.concatenate([v_new[h, ic] for ic in range(per)], axis=0)
                    qs_pair = jnp.concatenate([q_st[h, ic] for ic in range(per)], axis=0)
                    halves.append(qs_pair + _dot(qk[h, p], vn_pair))
                groups.append(jnp.where(low, halves[0], halves[1]))
            o = jnp.concatenate(groups, axis=-1)
            ms = _dot_split(o * o, same_head) * (1.0 / d)
            o_ref[0, r, :] = o * lax.rsqrt(ms + EPS) * ng * _silu(z_ref[0, r, :])

        return ([functools.partial(operators, ic) for ic in range(per)]
                + [functools.partial(recur, ic) for ic in range(per)]
                + [functools.partial(read_state, ic) for ic in range(per)] + [outputs])

    for slot in range(n_pair + 2):
        lists = [steps(q) for steps, q in ((solve_steps, slot - 1), (setup_steps, slot), (finish_steps, slot - 2))
                 if 0 <= q < n_pair]
        merged = sorted(((i + 0.5) / len(lst), k, i, fn) for k, lst in enumerate(lists) for i, fn in enumerate(lst))
        for _, _, _, fn in merged:
            fn()
    for h in heads:
        state[h] = st[h]


def _gdn(proj, conv_w, head_params, norm_g):
    bsz, seq, _ = proj.shape
    t = min(seq, ROW_BLOCK)
    w_qkv = 3 * GDN_DIM
    return pl.pallas_call(
        _gdn_kernel,
        grid=(bsz, seq // t),
        in_specs=[pl.BlockSpec((1, t, w_qkv), lambda b, s: (b, s, COL_QKV // w_qkv)),
                  pl.BlockSpec((1, t, GDN_DIM), lambda b, s: (b, s, COL_Z // GDN_DIM)),
                  pl.BlockSpec((1, t, LANE), lambda b, s: (b, s, COL_GAB // LANE)),
                  pl.BlockSpec((GDN_CONV, w_qkv), lambda b, s: (0, 0)),
                  pl.BlockSpec((SUBLANE, LANE), lambda b, s: (0, 0)),
                  pl.BlockSpec((1, GDN_DIM), lambda b, s: (0, 0))],
        out_specs=pl.BlockSpec((1, t, GDN_DIM), lambda b, s: (b, s, 0)),
        out_shape=jax.ShapeDtypeStruct((bsz, seq, GDN_DIM), F32),
        scratch_shapes=[pltpu.VMEM((t + SUBLANE, w_qkv), F32),
                        pltpu.VMEM((GDN_HEADS, HEAD_DIM, LANE), F32)],
        compiler_params=_cparams(("parallel", "arbitrary")),
    )(proj, proj, proj, conv_w, head_params, norm_g)


def _cmp_kernel(c_ref, wa_ref, wb_ref, pea_ref, peb_ref, cos_ref, sin_ref, o_ref):
    cb = c_ref[0]
    ya = _dot(cb + pea_ref[...], wa_ref[...])
    yb = _dot(cb + peb_ref[...], wb_ref[...])
    n = ya.shape[0]
    y = ya + pltpu.roll(yb, n - 1, 0)
    kc = _rope(y[:, :LANE], cos_ref[0], sin_ref[0])
    o_ref[0] = jnp.concatenate([kc, y[:, LANE:]], axis=-1)


def _compress(cmp_rows, layer, wa, wb, pea, peb, cos_c, sin_c):
    bsz, n, width = cmp_rows.shape
    of_layer = lambda arr: pl.BlockSpec((None,) + arr.shape[1:], lambda b: (layer, 0, 0))
    return pl.pallas_call(
        _cmp_kernel,
        grid=(bsz,),
        in_specs=[pl.BlockSpec((1, n, width), lambda b: (b, 0, 0)),
                  of_layer(wa), of_layer(wb), of_layer(pea), of_layer(peb),
                  pl.BlockSpec((1, n, LANE), lambda b: (b, 0, 0)),
                  pl.BlockSpec((1, n, LANE), lambda b: (b, 0, 0))],
        out_specs=pl.BlockSpec((1, n, 2 * LANE), lambda b: (b, 0, 0)),
        out_shape=jax.ShapeDtypeStruct((bsz, n, 2 * LANE), F32),
        compiler_params=_cparams(("parallel",)),
    )(cmp_rows, wa, wb, pea, peb, cos_c, sin_c)


_NSA_KEY_BLOCK = 512
_NSA_Q_TILE = 256


def _attend(k16, v16, q2, bias):
    d = HEAD_DIM
    half = q2.shape[1] // NSA_KV_HEADS
    s = jnp.dot(k16, q2, preferred_element_type=F32) + bias
    m = jnp.max(s, axis=0, keepdims=True)
    p = jnp.exp2(s - m)
    l = jnp.sum(p, axis=0, keepdims=True)
    pv = lax.dot_general(v16, p.astype(BF16), (((0,), (0,)), ((), ())), preferred_element_type=F32)
    return m, l, jnp.concatenate([pv[:d, :half], pv[d:, half:]], axis=1), p


def _nsa_kernel(q_ref, gate_ref, kvc_ref, ksel_ref, kwin_ref, o_ref, mask_ref, *, seq):
    qb = q_ref.shape[1]
    d = HEAD_DIM
    grp = NSA_GROUP
    cols = grp * qb
    n_slc = seq // SEL_BLOCK
    topk = min(SEL_TOPK, n_slc)
    ncp = seq // CMP_STRIDE
    qi = pl.program_id(1)
    q_t = q_ref[0].T
    gate_t = _sigmoid(gate_ref[0]).T
    t_q = qi * qb + lax.broadcasted_iota(jnp.int32, (1, qb), 1)

    heads = NSA_KV_HEADS * grp
    tile_all = lambda a: jnp.concatenate([a] * heads, axis=1)
    n_idx = lax.broadcasted_iota(jnp.int32, (ncp, qb), 0)
    cmp_bias = tile_all(jnp.where((n_idx * CMP_STRIDE + (CMP_LEN - 1) <= t_q) & (n_idx < ncp - 1), 0.0, NEG))
    sj = lax.broadcasted_iota(jnp.int32, (n_slc, ncp), 0) * SEL_BLOCK
    ci = lax.broadcasted_iota(jnp.int32, (n_slc, ncp), 1) * CMP_STRIDE
    overlap_t = ((ci < sj + SEL_BLOCK) & (ci + CMP_LEN > sj) & (ci < (ncp - 1) * CMP_STRIDE)).astype(BF16)
    blk = lax.broadcasted_iota(jnp.int32, (n_slc, qb), 0)
    blk_f = blk.astype(F32)
    cur = t_q // SEL_BLOCK
    forced = (blk == 0) | (blk == cur) | (blk == cur - 1)
    kb = min(seq, _NSA_KEY_BLOCK)
    span = min(seq, WINDOW + qb)
    causal_bias = jnp.where(lax.broadcasted_iota(jnp.int32, (qb, qb), 0)
                            <= lax.broadcasted_iota(jnp.int32, (qb, qb), 1), 0.0, NEG)
    w0 = pl.multiple_of(jnp.maximum(qi * qb + qb - span, 0), qb)
    diff = t_q - (w0 + lax.broadcasted_iota(jnp.int32, (span, qb), 0))
    win_bias = tile_all(jnp.where((diff >= 0) & (diff < WINDOW), 0.0, NEG))
    kvc = kvc_ref[0]

    zeros_q = jnp.zeros((d, cols), BF16)
    q_h = [jnp.concatenate([q_t[(h * grp + g) * d:(h * grp + g + 1) * d, :] for g in range(grp)],
                           axis=1).astype(BF16) for h in range(NSA_KV_HEADS)]
    q2 = jnp.concatenate([jnp.concatenate([q_h[0], zeros_q], axis=0),
                          jnp.concatenate([zeros_q, q_h[1]], axis=0)], axis=1)

    m_c, l_c, o_c, p_c = _attend(kvc[:, :LANE].astype(BF16), kvc[:, LANE:].astype(BF16), q2, cmp_bias)
    inv_l = jnp.where(m_c > 0.5 * NEG, 1.0 / l_c, 0.0)
    o_c = o_c * inv_l
    p_c = p_c * inv_l

    kv = kwin_ref[0, pl.ds(w0, span), :]
    _, l_w, o_w, _ = _attend(kv[:, :LANE].astype(BF16), kv[:, LANE:].astype(BF16), q2, win_bias)
    o_w = o_w * (1.0 / l_w)

    for h in range(NSA_KV_HEADS):
        p_sum = p_c[:, h * cols:h * cols + qb]
        for g in range(1, grp):
            p_sum = p_sum + p_c[:, h * cols + g * qb:h * cols + (g + 1) * qb]
        imp = _dot_split3_rhs(overlap_t, p_sum)
        work = jnp.where(forced, FORCE, jnp.where(blk * SEL_BLOCK <= t_q, imp, -1.0))
        sel = jnp.zeros((n_slc, qb), F32)
        for _ in range(topk):
            best = jnp.max(work, axis=0, keepdims=True)
            idx = jnp.min(jnp.where(work == best, blk_f, 1e9), axis=0, keepdims=True)
            pick = blk_f == idx
            sel = jnp.where(pick, 1.0, sel)
            work = jnp.where(pick, -3.0, work)
        sel_bias = jnp.where((sel > 0.5) & (blk * SEL_BLOCK <= t_q), 0.0, NEG)
        for b in range(n_slc):
            mask_ref[b * SEL_BLOCK:(b + 1) * SEL_BLOCK, h * qb:(h + 1) * qb] = jnp.broadcast_to(
                sel_bias[b:b + 1, :], (SEL_BLOCK, qb))
        own_keys = pl.ds(pl.multiple_of(qi * qb, qb), qb)
        mask_ref[own_keys, h * qb:(h + 1) * qb] = mask_ref[own_keys, h * qb:(h + 1) * qb] + causal_bias

    def sel_body(j, carry):
        m, l, acc = carry
        off = pl.multiple_of(j * kb, kb)
        kv = ksel_ref[0, pl.ds(off, kb), :]
        mask = mask_ref[pl.ds(off, kb), :]
        bias = jnp.concatenate([mask[:, :qb]] * grp + [mask[:, qb:]] * grp, axis=1)
        m_b, l_b, pv_b, _ = _attend(kv[:, :LANE].astype(BF16), kv[:, LANE:].astype(BF16), q2, bias)
        m_new = jnp.maximum(m, m_b)
        w_old = jnp.exp2(m - m_new)
        w_blk = jnp.exp2(m_b - m_new)
        return m_new, w_old * l + w_blk * l_b, w_old * acc + w_blk * pv_b

    init = (jnp.full((1, 2 * cols), NEG, F32), jnp.zeros((1, 2 * cols), F32), jnp.zeros((d, 2 * cols), F32))
    _, l_s, acc_s = lax.fori_loop(0, (qi * qb + qb + kb - 1) // kb, sel_body, init)
    o_s = acc_s * (1.0 / l_s)

    out_rows = []
    for hd in range(heads):
        cs = slice(hd * qb, (hd + 1) * qb)
        out_rows.append(gate_t[3 * hd:3 * hd + 1] * o_c[:, cs] + gate_t[3 * hd + 1:3 * hd + 2] * o_s[:, cs]
                        + gate_t[3 * hd + 2:3 * hd + 3] * o_w[:, cs])
    o_ref[0] = jnp.concatenate(out_rows, axis=0).T


def _nsa_attention(proj, kvc):
    bsz, seq, _ = proj.shape
    qb = min(seq, _NSA_Q_TILE)
    ncp = kvc.shape[1]
    kv = 2 * LANE
    return pl.pallas_call(
        functools.partial(_nsa_kernel, seq=seq),
        grid=(bsz, seq // qb),
        in_specs=[pl.BlockSpec((1, qb, NSA_DIM), lambda b, i: (b, i, COL_NQ // NSA_DIM)),
                  pl.BlockSpec((1, qb, LANE), lambda b, i: (b, i, COL_GATE // LANE)),
                  pl.BlockSpec((1, ncp, kv), lambda b, i: (b, 0, 0)),
                  pl.BlockSpec((1, seq, kv), lambda b, i: (b, 0, COL_SEL // kv)),
                  pl.BlockSpec((1, seq, kv), lambda b, i: (b, 0, COL_WIN // kv))],
        out_specs=pl.BlockSpec((1, qb, NSA_DIM), lambda b, i: (b, i, 0)),
        out_shape=jax.ShapeDtypeStruct((bsz, seq, NSA_DIM), F32),
        scratch_shapes=[pltpu.VMEM((seq, NSA_KV_HEADS * qb), F32)],
        compiler_params=_cparams(("parallel", "arbitrary")),
    )(proj, proj, kvc, proj, proj)


_CC_HALO = 32
_CC_ROWS = 64


def _cconv_kernel(u_ref, w_ref, b_ref, lg_ref, lb_ref, o_ref, xbuf, shifted):
    t = u_ref.shape[1]
    s = pl.program_id(1)

    @pl.when(s == 0)
    def _():
        xbuf[0:_CC_HALO, :] = jnp.zeros((_CC_HALO, CONV_CH), F32)

    @pl.when(s != 0)
    def _():
        xbuf[0:_CC_HALO, :] = xbuf[t:t + _CC_HALO, :]

    u = u_ref[0]
    xbuf[_CC_HALO:_CC_HALO + t, :] = u[:, :CONV_CH] * _sigmoid(u[:, CONV_CH:])
    w = w_ref[...]
    first = _CC_HALO - (CONV_WIDTH - 1)
    rows_kept = t + _CC_HALO - SUBLANE
    for res in range(1, SUBLANE):
        shifted[res - 1, 0:rows_kept, :] = xbuf[pl.ds(res, rows_kept), :]
    for r in range(t // _CC_ROWS):
        acc = jnp.broadcast_to(b_ref[...], (_CC_ROWS, CONV_CH))
        for j in range(CONV_WIDTH):
            res = (first + j) % SUBLANE
            base = r * _CC_ROWS + first + j - res
            tap = xbuf[base:base + _CC_ROWS, :] if res == 0 else shifted[res - 1, base:base + _CC_ROWS, :]
            acc = acc + w[j:j + 1] * tap
        mu = jnp.mean(acc, axis=-1, keepdims=True)
        var = jnp.mean(jnp.square(acc - mu), axis=-1, keepdims=True)
        hn = (acc - mu) * lax.rsqrt(var + EPS) * lg_ref[...] + lb_ref[...]
        o_ref[0, r * _CC_ROWS:(r + 1) * _CC_ROWS, :] = _silu(hn)


def _cconv(proj, dw_w, dw_b, ln_g, ln_b):
    bsz, seq, _ = proj.shape
    t = min(seq, ROW_BLOCK)
    wu = 2 * CONV_CH
    vec = pl.BlockSpec((1, CONV_CH), lambda b, s: (0, 0))
    return pl.pallas_call(
        _cconv_kernel,
        grid=(bsz, seq // t),
        in_specs=[pl.BlockSpec((1, t, wu), lambda b, s: (b, s, COL_CU // wu)),
                  pl.BlockSpec((CONV_WIDTH, CONV_CH), lambda b, s: (0, 0)),
                  vec, vec, vec],
        out_specs=pl.BlockSpec((1, t, CONV_CH), lambda b, s: (b, s, 0)),
        out_shape=jax.ShapeDtypeStruct((bsz, seq, CONV_CH), F32),
        scratch_shapes=[pltpu.VMEM((t + _CC_HALO, CONV_CH), F32),
                        pltpu.VMEM((SUBLANE - 1, t + _CC_HALO, CONV_CH), F32)],
        compiler_params=_cparams(("parallel", "arbitrary")),
    )(proj, dw_w, dw_b, ln_g, ln_b)


_FFN_TILE = 256


def _ffn_kernel(x_ref, oa_ref, ob_ref, oc_ref, wo_ref, gmix_ref, gpre_ref, wup_ref, cw_ref, wd_ref, gpost_ref,
                o_ref, xn_ref, act_ref, stage_ref, carry_ref):
    tm = x_ref.shape[1]
    d_ff = wd_ref.shape[0]
    tf = _FFN_TILE
    hal = SUBLANE

    @pl.when(pl.program_id(1) == 0)
    def _():
        carry_ref[...] = jnp.zeros_like(carry_ref)

    mix = (jnp.dot(oa_ref[0].astype(BF16), wo_ref[:GDN_DIM], preferred_element_type=F32)
           + jnp.dot(ob_ref[0].astype(BF16), wo_ref[GDN_DIM:GDN_DIM + NSA_DIM], preferred_element_type=F32)
           + jnp.dot(oc_ref[0].astype(BF16), wo_ref[GDN_DIM + NSA_DIM:], preferred_element_type=F32))
    x = x_ref[0] + _rms(mix, gmix_ref[...])
    xn_ref[...] = _rms(x, gpre_ref[...]).astype(BF16)
    for f in range(d_ff // tf):
        ys = []
        for part in range(2):
            cols = slice(part * d_ff + f * tf, part * d_ff + (f + 1) * tf)
            h = jnp.dot(xn_ref[...], wup_ref[:, cols], preferred_element_type=F32)
            stage = stage_ref.at[f % 2, part]
            stage[0:hal, :] = carry_ref[:, cols]
            stage[hal:hal + tm, :] = h
            carry_ref[:, cols] = h[tm - hal:tm, :]
            cw = cw_ref[:, cols]
            ys.append(cw[2:3] * h + cw[1:2] * stage[pl.ds(hal - 1, tm), :]
                      + cw[0:1] * stage[pl.ds(hal - 2, tm), :])
        act_ref[:, f * tf:(f + 1) * tf] = (_silu(ys[0]) * ys[1]).astype(BF16)
    out = jnp.dot(act_ref[...], wd_ref[...], preferred_element_type=F32)
    o_ref[0] = x + _rms(out, gpost_ref[...])


def _outproj_ffn(x, o_a, o_b, o_c, layer, w_out, g_mix, g_pre, w_up, conv_w, w_down, g_post):
    bsz, seq, d = x.shape
    d_ff = w_down.shape[1]
    tm = min(seq, ROW_BLOCK)
    kw = conv_w.shape[1]
    resident = lambda shape: pl.BlockSpec((None,) + shape, lambda b, i: (layer, 0, 0),
                                          pipeline_mode=pl.Buffered(1))
    rows = lambda width: pl.BlockSpec((1, tm, width), lambda b, i: (b, i, 0))
    return pl.pallas_call(
        _ffn_kernel,
        grid=(bsz, seq // tm),
        in_specs=[rows(d), rows(GDN_DIM), rows(NSA_DIM), rows(CONV_CH),
                  resident((d, d)),
                  pl.BlockSpec((1, d), lambda b, i: (0, 0)),
                  pl.BlockSpec((1, d), lambda b, i: (0, 0)),
                  resident((d, 2 * d_ff)),
                  resident((kw, 2 * d_ff)),
                  resident((d_ff, d)),
                  pl.BlockSpec((1, d), lambda b, i: (0, 0))],
        out_specs=pl.BlockSpec((1, tm, d), lambda b, i: (b, i, 0)),
        out_shape=jax.ShapeDtypeStruct((bsz, seq, d), F32),
        scratch_shapes=[pltpu.VMEM((tm, d), BF16),
                        pltpu.VMEM((tm, d_ff), BF16),
                        pltpu.VMEM((2, 2, tm + SUBLANE, _FFN_TILE), F32),
                        pltpu.VMEM((SUBLANE, 2 * d_ff), F32)],
        compiler_params=_cparams(("parallel", "arbitrary")),
    )(x, o_a, o_b, o_c, w_out, g_mix, g_pre, w_up, conv_w, w_down, g_post)


def _pack_w_in(w_in):
    depth, d, _ = w_in.shape
    sizes = (GDN_DIM,) * 4 + (GDN_HEADS,) * 2 + (NSA_DIM,) + (LANE,) * 6 + (3 * NSA_HEADS, 2 * CONV_CH)
    offs = np.concatenate([[0], np.cumsum(sizes)])
    w_in = w_in.astype(BF16)
    piece = lambda k: w_in[:, :, offs[k]:offs[k + 1]]
    zeros = lambda n: jnp.zeros((depth, d, n), BF16)
    gq, gk, gv, gz, ga, gb, nq, nkc, nvc, nks, nvs, nkw, nvw, ngate, cu = [piece(k) for k in range(15)]
    cols = [gq, gk, gv, gz, nq,
            ga, gb, zeros(LANE - 2 * GDN_HEADS),
            cu, nks, nvs, nkw, nvw,
            ngate, zeros(LANE - 3 * NSA_HEADS),
            nkc, nvc]
    packed = jnp.concatenate(cols, axis=-1)
    assert packed.shape[-1] == PROJ_DIM
    return packed


def _pack_compress(wk, wv, pe_k, pe_v):
    depth = wk.shape[0]
    half = CMP_LEN // 2
    d = HEAD_DIM
    w4 = jnp.stack([wk, wk, wv, wv], axis=2).astype(BF16)
    same_slot = jnp.eye(4, dtype=BF16)[None, None, :, None, :, None]
    big = (w4[:, :, :, :, None, :] * same_slot).reshape(depth, 2, half * 4 * d, 4 * d)
    pe = jnp.concatenate([pe_k, pe_k, pe_v, pe_v], axis=-1).reshape(depth, 2, 1, half * 4 * d)
    return big[:, 0], big[:, 1], pe[:, 0], pe[:, 1]


def kernel(x, positions, norm_mix_pre, norm_mix_post, norm_ffn_pre, norm_ffn_post, w_in, w_out, gdn_conv_w, gdn_a_log, gdn_dt_bias, gdn_norm_g, nsa_cmp_wk, nsa_cmp_wv, nsa_cmp_pe_k, nsa_cmp_pe_v, cc_dw_w, cc_dw_b, cc_ln_g, cc_ln_b, ffn_w_up, ffn_conv_w, ffn_w_down):
    bsz, seq, d = x.shape
    depth = w_in.shape[0]
    assert seq % Q_BLOCK == 0 and d == GDN_DIM + NSA_DIM + CONV_CH
    m = bsz * seq

    cos, sin = _rope_tables(positions)
    ncp = seq // CMP_STRIDE
    pad_rows = lambda t: jnp.pad(t[:, CMP_LEN - 1::CMP_STRIDE], ((0, 0), (0, 1), (0, 0)))
    cos_c, sin_c = pad_rows(cos), pad_rows(sin)

    w_in_p = _pack_w_in(w_in)
    w_out_b = w_out.astype(BF16)
    w_up_b = ffn_w_up.astype(BF16)
    w_down_b = ffn_w_down.astype(BF16)
    head_params = jnp.zeros((depth, SUBLANE, LANE), F32)
    head_params = head_params.at[:, 0, :GDN_HEADS].set(gdn_a_log).at[:, 1, :GDN_HEADS].set(gdn_dt_bias)

    wa, wb, pea, peb = _pack_compress(nsa_cmp_wk, nsa_cmp_wv, nsa_cmp_pe_k, nsa_cmp_pe_v)
    cos2d = cos.reshape(m, LANE)
    sin2d = sin.reshape(m, LANE)
    for l in range(depth):
        proj, cmp_cols = _inproj(x.reshape(m, d), norm_mix_pre[l][None], w_in_p, l, cos2d, sin2d)
        proj = proj.reshape(bsz, seq, PROJ_OUT)
        o_a = _gdn(proj, gdn_conv_w[l], head_params[l], jnp.tile(gdn_norm_g[l], GDN_HEADS)[None])
        kvc = _compress(cmp_cols.reshape(bsz, ncp, CMP_STRIDE * 2 * LANE), l, wa, wb, pea, peb, cos_c, sin_c)
        o_b = _nsa_attention(proj, kvc)
        o_c = _cconv(proj, cc_dw_w[l], cc_dw_b[l][None], cc_ln_g[l][None], cc_ln_b[l][None])
        x = _outproj_ffn(x, o_a, o_b, o_c, l, w_out_b, norm_mix_post[l][None], norm_ffn_pre[l][None],
                         w_up_b, ffn_conv_w, w_down_b, norm_ffn_post[l][None])
    return x
```

```python
import functools

import jax
import jax.numpy as jnp
import numpy as np
from jax import lax
from jax.experimental import pallas as pl
from jax.experimental.pallas import tpu as pltpu

F32 = jnp.float32
BF16 = jnp.bfloat16

HEAD_DIM = 64
GDN_HEADS = 6
GDN_DIM = GDN_HEADS * HEAD_DIM
GDN_CONV = 4
GDN_CHUNK = 64
NSA_HEADS = 6
NSA_KV_HEADS = 2
NSA_GROUP = NSA_HEADS // NSA_KV_HEADS
NSA_DIM = NSA_HEADS * HEAD_DIM
CMP_STRIDE = 16
CMP_LEN = 32
SEL_BLOCK = 64
SEL_TOPK = 8
WINDOW = 512
Q_BLOCK = 128
CONV_CH = 256
CONV_WIDTH = 31
ROPE_THETA = 10000.0
EPS = 1e-6
NEG = -1e30
FORCE = 1e4
_LOG2_E = 1.4426950408889634

LANE = 128
SUBLANE = 8
VMEM_LIMIT = 56 * 1024 * 1024

COL_QKV = 0
COL_Z = 1152
COL_NQ = 1536
COL_GAB = 1920
COL_CU = 2048
COL_SEL = 2560
COL_WIN = 2816
COL_GATE = 3072
PROJ_OUT = 3200
COL_CMP = 3200
PROJ_DIM = 3456


def _cparams(sem):
    return pltpu.CompilerParams(dimension_semantics=sem, vmem_limit_bytes=VMEM_LIMIT)


def _sigmoid(x):
    return 1.0 / (1.0 + jnp.exp(-x))


def _silu(x):
    return x * _sigmoid(x)


def _dot(a, b):
    return jnp.dot(a.astype(BF16), b.astype(BF16), preferred_element_type=F32)


def _dot_nt(a, b):
    return lax.dot_general(a.astype(BF16), b.astype(BF16), (((1,), (1,)), ((), ())),
                           preferred_element_type=F32)


def _dot_split3_rhs(sel, x):
    hi = x.astype(BF16)
    r1 = x - hi.astype(F32)
    mid = r1.astype(BF16)
    lo = (r1 - mid.astype(F32)).astype(BF16)
    return (jnp.dot(sel, hi, preferred_element_type=F32) + jnp.dot(sel, mid, preferred_element_type=F32)
            + jnp.dot(sel, lo, preferred_element_type=F32))


def _dot_split(x, sel):
    hi = x.astype(BF16)
    lo = (x - hi.astype(F32)).astype(BF16)
    return (jnp.dot(hi, sel, preferred_element_type=F32) + jnp.dot(lo, sel, preferred_element_type=F32))


def _rms(x, g):
    return x * lax.rsqrt(jnp.mean(x * x, axis=-1, keepdims=True) + EPS) * g


def _rope_table_kernel(pos_ref, inv_ref, sign_ref, cos_ref, sin_ref):
    ang = pos_ref[0].astype(F32) * inv_ref[...]
    cos_ref[0] = jnp.cos(ang)
    sin_ref[0] = jnp.sin(ang) * sign_ref[...]


def _rope_tables(positions):
    bsz, seq = positions.shape
    t = min(seq, 512)
    inv = 1.0 / (ROPE_THETA ** (jnp.arange(0, HEAD_DIM, 2, dtype=F32) / HEAD_DIM))
    inv = jnp.tile(inv, LANE // (HEAD_DIM // 2))[None, :]
    sign = jnp.tile(jnp.concatenate([-jnp.ones(HEAD_DIM // 2, F32), jnp.ones(HEAD_DIM // 2, F32)]),
                    LANE // HEAD_DIM)[None, :]
    out = jax.ShapeDtypeStruct((bsz, seq, LANE), F32)
    return pl.pallas_call(
        _rope_table_kernel,
        grid=(bsz, seq // t),
        in_specs=[pl.BlockSpec((1, t, 1), lambda b, s: (b, s, 0)),
                  pl.BlockSpec((1, LANE), lambda b, s: (0, 0)),
                  pl.BlockSpec((1, LANE), lambda b, s: (0, 0))],
        out_specs=[pl.BlockSpec((1, t, LANE), lambda b, s: (b, s, 0))] * 2,
        out_shape=[out, out],
        compiler_params=_cparams(("parallel", "parallel")),
    )(positions[:, :, None], inv, sign)


def _rope(x, cos, sin):
    lane = lax.broadcasted_iota(jnp.int32, x.shape, 1)
    first_half = (lane % HEAD_DIM) < (HEAD_DIM // 2)
    partner = jnp.where(first_half, pltpu.roll(x, LANE - HEAD_DIM // 2, 1), pltpu.roll(x, HEAD_DIM // 2, 1))
    return x * cos + partner * sin


def _inproj_kernel(x_ref, g_ref, w_ref, cos_ref, sin_ref, o_ref, cmp_ref):
    xn = _rms(x_ref[...], g_ref[...]).astype(BF16)
    y = jnp.dot(xn, w_ref[...], preferred_element_type=F32)
    cos = cos_ref[...]
    sin = sin_ref[...]
    scale = HEAD_DIM ** -0.5 * _LOG2_E
    o_ref[:, :COL_NQ] = y[:, :COL_NQ]
    for c in range(COL_NQ, COL_NQ + NSA_DIM, LANE):
        o_ref[:, c:c + LANE] = _rope(y[:, c:c + LANE], cos, sin) * scale
    o_ref[:, COL_GAB:COL_SEL] = y[:, COL_GAB:COL_SEL]
    for c in (COL_SEL, COL_WIN):
        o_ref[:, c:c + LANE] = _rope(y[:, c:c + LANE], cos, sin)
        o_ref[:, c + LANE:c + 2 * LANE] = y[:, c + LANE:c + 2 * LANE]
    o_ref[:, COL_GATE:PROJ_OUT] = y[:, COL_GATE:PROJ_OUT]
    cmp_ref[...] = y[:, COL_CMP:]


def _inproj(x2d, g, w_all, layer, cos2d, sin2d):
    m, d = x2d.shape
    n = w_all.shape[2]
    tm = min(m, 512)
    row = lambda width: pl.BlockSpec((tm, width), lambda i: (i, 0))
    return pl.pallas_call(
        _inproj_kernel,
        grid=(m // tm,),
        in_specs=[row(d),
                  pl.BlockSpec((1, d), lambda i: (0, 0)),
                  pl.BlockSpec((None, d, n), lambda i: (layer, 0, 0), pipeline_mode=pl.Buffered(1)),
                  row(LANE), row(LANE)],
        out_specs=[row(PROJ_OUT), row(n - PROJ_OUT)],
        out_shape=[jax.ShapeDtypeStruct((m, PROJ_OUT), F32), jax.ShapeDtypeStruct((m, n - PROJ_OUT), F32)],
        compiler_params=_cparams(("parallel",)),
    )(x2d, g, w_all, cos2d, sin2d)


_GDN_PAIR = 2 * GDN_CHUNK


def _gdn_kernel(qkv_ref, z_ref, gab_ref, cw_ref, hp_ref, ng_ref, o_ref, xbuf, state):
    c = GDN_CHUNK
    d = HEAD_DIM
    pr = _GDN_PAIR
    t = qkv_ref.shape[1]
    s = pl.program_id(1)

    @pl.when(s == 0)
    def _():
        xbuf[0:SUBLANE, :] = jnp.zeros((SUBLANE, 3 * GDN_DIM), F32)
        state[...] = jnp.zeros_like(state)

    @pl.when(s != 0)
    def _():
        xbuf[0:SUBLANE, :] = xbuf[t:t + SUBLANE, :]

    xbuf[SUBLANE:SUBLANE + t, :] = qkv_ref[0]
    cw = cw_ref[...]
    hp = hp_ref[...]

    lane_h = lax.broadcasted_iota(jnp.int32, (LANE, GDN_DIM), 0)
    col_h = lax.broadcasted_iota(jnp.int32, (LANE, GDN_DIM), 1) // d
    expand_a = (lane_h == col_h).astype(BF16)
    expand_b = (lane_h == col_h + GDN_HEADS).astype(BF16)
    same_head = (lax.broadcasted_iota(jnp.int32, (GDN_DIM, GDN_DIM), 0) // d
                 == lax.broadcasted_iota(jnp.int32, (GDN_DIM, GDN_DIM), 1) // d).astype(BF16)
    in_chunk = lax.broadcasted_iota(jnp.int32, (pr, LANE), 0) % c
    first = (lax.broadcasted_iota(jnp.int32, (pr, GDN_DIM), 1) % LANE) < d
    a = {}

    def prep_conv(p):
        r0 = SUBLANE + p * pr
        y = cw[GDN_CONV - 1:GDN_CONV] * xbuf[pl.ds(r0, pr), :]
        for j in range(GDN_CONV - 2, -1, -1):
            y = y + cw[j:j + 1] * xbuf[pl.ds(r0 - (GDN_CONV - 1) + j, pr), :]
        a[p] = {"y": _silu(y)}

    def prep_gates(p):
        gab = gab_ref[0, p * pr:(p + 1) * pr, :]
        sp_in = gab + hp[1:2]
        softplus = jnp.maximum(sp_in, 0.0) + jnp.log(1.0 + jnp.exp(-jnp.abs(sp_in)))
        gcum = -jnp.exp(hp[0:1]) * softplus
        shift = 1
        while shift < c:
            gcum = gcum + jnp.where(in_chunk >= shift, pltpu.roll(gcum, shift, 0), 0.0)
            shift *= 2
        g_last = jnp.concatenate(
            [jnp.broadcast_to(gcum[(i + 1) * c - 1:(i + 1) * c, :], (c, LANE)) for i in range(pr // c)], axis=0)
        eg = jnp.exp(gcum)
        a[p].update(gcum=gcum, gcum_t=gcum.T, eg=eg,
                    beta_e=_dot_split(_sigmoid(gab), expand_b),
                    eg_e=_dot_split(eg, expand_a),
                    kdec_e=_dot_split(jnp.exp(g_last - gcum), expand_a))

    def prep_qkv(p):
        ap = a[p]
        y = ap.pop("y")
        q = y[:, :GDN_DIM]
        k = y[:, GDN_DIM:2 * GDN_DIM]
        v = y[:, 2 * GDN_DIM:]
        q = q * (lax.rsqrt(_dot_split(q * q, same_head) + EPS) * (d ** -0.5))
        k = k * lax.rsqrt(_dot_split(k * k, same_head) + EPS)
        kbeta = k * ap["beta_e"]
        ap.update(k16=k.astype(BF16),
                  q16=[jnp.where(first, q, 0.0).astype(BF16), jnp.where(first, 0.0, q).astype(BF16)],
                  kb16=[jnp.where(first, kbeta, 0.0).astype(BF16), jnp.where(first, 0.0, kbeta).astype(BF16)],
                  vb=v * ap["beta_e"], kbe=kbeta * ap["eg_e"],
                  qe16=(q * ap["eg_e"]).astype(BF16), kd16=(k * ap["kdec_e"]).astype(BF16))

    row = lax.broadcasted_iota(jnp.int32, (pr, pr), 0)
    col = lax.broadcasted_iota(jnp.int32, (pr, pr), 1)
    same_chunk = (row // c) == (col // c)
    tri = same_chunk & (row >= col)
    tri_strict = same_chunk & (row > col)
    low = lax.broadcasted_iota(jnp.int32, (pr, LANE), 1) < d
    low_c = lax.broadcasted_iota(jnp.int32, (c, LANE), 1) < d
    n_pair = t // pr
    heads = range(GDN_HEADS)
    per = pr // c
    lanes_of = lambda h: slice((h // 2) * LANE, (h // 2 + 1) * LANE)
    sol, pw, qk = {}, {}, {}
    st = [state[h] for h in heads]
    zeros = jnp.zeros((c, LANE), BF16)
    ng = ng_ref[...]

    def setup_steps(p):
        def step(j):
            ap = a[p]
            g = lanes_of(2 * j)
            lhs = jnp.concatenate([t16[:, g] for t16 in ap["kb16"] + ap["q16"]], axis=0)
            prod = _dot_nt(lhs, ap["k16"][:, g])
            ke_sw = pltpu.roll(ap["kbe"][:, g], d, 1)
            vb = ap["vb"][:, g]
            for i, h in enumerate((2 * j, 2 * j + 1)):
                decay = jnp.exp(jnp.where(tri, ap["gcum"][:, h:h + 1] - ap["gcum_t"][h:h + 1, :], NEG))
                pw[h, p] = jnp.where(tri_strict, -(prod[i * pr:(i + 1) * pr] * decay), 0.0).astype(BF16)
                qk[h, p] = (prod[(2 + i) * pr:(3 + i) * pr] * decay).astype(BF16)
                sol[h, p] = jnp.where(low, vb, ke_sw) if i == 0 else jnp.where(low, ke_sw, vb)

        return ([functools.partial(fn, p) for fn in (prep_conv, prep_gates, prep_qkv)]
                + [functools.partial(step, j) for j in range(GDN_HEADS // 2)])

    def solve_steps(p):
        def apply():
            for h in heads:
                sol[h, p] = sol[h, p] + _dot(pw[h, p], sol[h, p])

        def square():
            for h in heads:
                pw[h, p] = _dot(pw[h, p], pw[h, p]).astype(BF16)

        return [apply] + [square, apply] * 5

    def finish_steps(p):
        ktuw, s_before, v_new, q_st = {}, {}, {}, {}
        rows = lambda ic: slice(ic * c, (ic + 1) * c)

        def operators(ic):
            for h in heads:
                full = lax.dot_general(a[p]["kd16"][rows(ic), lanes_of(h)], sol[h, p][rows(ic)].astype(BF16),
                                       (((0,), (0,)), ((), ())), preferred_element_type=F32)
                ktuw[h, ic] = full[:c] if h % 2 == 0 else full[c:]

        def recur(ic):
            last = (ic + 1) * c - 1
            for h in heads:
                even = h % 2 == 0
                s16 = st[h].astype(BF16)
                s_before[h, ic] = s16
                s_w = jnp.concatenate([zeros, s16] if even else [s16, zeros], axis=0)
                el = a[p]["eg"][last:last + 1, h:h + 1]
                st[h] = jnp.where(low_c if even else ~low_c,
                                  st[h] * el + ktuw[h, ic] - _dot(ktuw[h, ic], s_w), 0.0)

        def read_state(ic):
            for h in heads:
                even = h % 2 == 0
                s16 = s_before[h, ic]
                s_w = jnp.concatenate([zeros, s16] if even else [s16, zeros], axis=0)
                s_q = jnp.concatenate([s16, zeros] if even else [zeros, s16], axis=0)
                sol_c = sol[h, p][rows(ic)]
                v_new[h, ic] = sol_c - _dot(sol_c, s_w)
                q_st[h, ic] = _dot(a[p]["qe16"][rows(ic), lanes_of(h)], s_q)

        def outputs():
            r = slice(p * pr, (p + 1) * pr)
            groups = []
            for j in range(GDN_HEADS // 2):
                halves = []
                for h in (2 * j, 2 * j + 1):
                    vn_pair = jnp.concatenate([v_new[h, ic] for ic in range(per)], axis=0)
                    qs_pair = jnp.concatenate([q_st[h, ic] for ic in range(per)], axis=0)
                    halves.append(qs_pair + _dot(qk[h, p], vn_pair))
                groups.append(jnp.where(low, halves[0], halves[1]))
            o = jnp.concatenate(groups, axis=-1)
            ms = _dot_split(o * o, same_head) * (1.0 / d)
            o_ref[0, r, :] = o * lax.rsqrt(ms + EPS) * ng * _silu(z_ref[0, r, :])

        return ([functools.partial(operators, ic) for ic in range(per)]
                + [functools.partial(recur, ic) for ic in range(per)]
                + [functools.partial(read_state, ic) for ic in range(per)] + [outputs])

    for slot in range(n_pair + 2):
        lists = [steps(q) for steps, q in ((solve_steps, slot - 1), (setup_steps, slot), (finish_steps, slot - 2))
                 if 0 <= q < n_pair]
        merged = sorted(((i + 0.5) / len(lst), k, i, fn) for k, lst in enumerate(lists) for i, fn in enumerate(lst))
        for _, _, _, fn in merged:
            fn()
    for h in heads:
        state[h] = st[h]


def _gdn(proj, conv_w, head_params, norm_g):
    bsz, seq, _ = proj.shape
    t = min(seq, 1024)
    w_qkv = 3 * GDN_DIM
    return pl.pallas_call(
        _gdn_kernel,
        grid=(bsz, seq // t),
        in_specs=[pl.BlockSpec((1, t, w_qkv), lambda b, s: (b, s, COL_QKV // w_qkv)),
                  pl.BlockSpec((1, t, GDN_DIM), lambda b, s: (b, s, COL_Z // GDN_DIM)),
                  pl.BlockSpec((1, t, LANE), lambda b, s: (b, s, COL_GAB // LANE)),
                  pl.BlockSpec((GDN_CONV, w_qkv), lambda b, s: (0, 0)),
                  pl.BlockSpec((SUBLANE, LANE), lambda b, s: (0, 0)),
                  pl.BlockSpec((1, GDN_DIM), lambda b, s: (0, 0))],
        out_specs=pl.BlockSpec((1, t, GDN_DIM), lambda b, s: (b, s, 0)),
        out_shape=jax.ShapeDtypeStruct((bsz, seq, GDN_DIM), F32),
        scratch_shapes=[pltpu.VMEM((t + SUBLANE, w_qkv), F32),
                        pltpu.VMEM((GDN_HEADS, HEAD_DIM, LANE), F32)],
        compiler_params=_cparams(("parallel", "arbitrary")),
    )(proj, proj, proj, conv_w, head_params, norm_g)


def _cmp_kernel(c_ref, wa_ref, wb_ref, pea_ref, peb_ref, cos_ref, sin_ref, o_ref):
    cb = c_ref[0]
    ya = _dot(cb + pea_ref[...], wa_ref[...])
    yb = _dot(cb + peb_ref[...], wb_ref[...])
    n = ya.shape[0]
    y = ya + pltpu.roll(yb, n - 1, 0)
    kc = _rope(y[:, :LANE], cos_ref[0], sin_ref[0])
    o_ref[0] = jnp.concatenate([kc, y[:, LANE:]], axis=-1)


def _compress(cmp_rows, layer, wa, wb, pea, peb, cos_c, sin_c):
    bsz, n, width = cmp_rows.shape
    of_layer = lambda arr: pl.BlockSpec((None,) + arr.shape[1:], lambda b: (layer, 0, 0))
    return pl.pallas_call(
        _cmp_kernel,
        grid=(bsz,),
        in_specs=[pl.BlockSpec((1, n, width), lambda b: (b, 0, 0)),
                  of_layer(wa), of_layer(wb), of_layer(pea), of_layer(peb),
                  pl.BlockSpec((1, n, LANE), lambda b: (b, 0, 0)),
                  pl.BlockSpec((1, n, LANE), lambda b: (b, 0, 0))],
        out_specs=pl.BlockSpec((1, n, 2 * LANE), lambda b: (b, 0, 0)),
        out_shape=jax.ShapeDtypeStruct((bsz, n, 2 * LANE), F32),
        compiler_params=_cparams(("parallel",)),
    )(cmp_rows, wa, wb, pea, peb, cos_c, sin_c)


_NSA_KEY_BLOCK = 512
_NSA_Q_TILE = 256


def _attend(k16, v16, q2, bias):
    d = HEAD_DIM
    half = q2.shape[1] // NSA_KV_HEADS
    s = jnp.dot(k16, q2, preferred_element_type=F32) + bias
    m = jnp.max(s, axis=0, keepdims=True)
    p = jnp.exp2(s - m)
    l = jnp.sum(p, axis=0, keepdims=True)
    pv = lax.dot_general(v16, p.astype(BF16), (((0,), (0,)), ((), ())), preferred_element_type=F32)
    return m, l, jnp.concatenate([pv[:d, :half], pv[d:, half:]], axis=1), p


def _nsa_kernel(q_ref, gate_ref, kvc_ref, ksel_ref, kwin_ref, o_ref, mask_ref, *, seq):
    qb = q_ref.shape[1]
    d = HEAD_DIM
    grp = NSA_GROUP
    cols = grp * qb
    n_slc = seq // SEL_BLOCK
    topk = min(SEL_TOPK, n_slc)
    ncp = seq // CMP_STRIDE
    qi = pl.program_id(1)
    q_t = q_ref[0].T
    gate_t = _sigmoid(gate_ref[0]).T
    t_q = qi * qb + lax.broadcasted_iota(jnp.int32, (1, qb), 1)

    heads = NSA_KV_HEADS * grp
    tile_all = lambda a: jnp.concatenate([a] * heads, axis=1)
    n_idx = lax.broadcasted_iota(jnp.int32, (ncp, qb), 0)
    cmp_bias = tile_all(jnp.where((n_idx * CMP_STRIDE + (CMP_LEN - 1) <= t_q) & (n_idx < ncp - 1), 0.0, NEG))
    sj = lax.broadcasted_iota(jnp.int32, (n_slc, ncp), 0) * SEL_BLOCK
    ci = lax.broadcasted_iota(jnp.int32, (n_slc, ncp), 1) * CMP_STRIDE
    overlap_t = ((ci < sj + SEL_BLOCK) & (ci + CMP_LEN > sj) & (ci < (ncp - 1) * CMP_STRIDE)).astype(BF16)
    blk = lax.broadcasted_iota(jnp.int32, (n_slc, qb), 0)
    blk_f = blk.astype(F32)
    cur = t_q // SEL_BLOCK
    forced = (blk == 0) | (blk == cur) | (blk == cur - 1)
    kb = min(seq, _NSA_KEY_BLOCK)
    span = min(seq, WINDOW + qb)
    causal_bias = jnp.where(lax.broadcasted_iota(jnp.int32, (qb, qb), 0)
                            <= lax.broadcasted_iota(jnp.int32, (qb, qb), 1), 0.0, NEG)
    w0 = pl.multiple_of(jnp.maximum(qi * qb + qb - span, 0), qb)
    diff = t_q - (w0 + lax.broadcasted_iota(jnp.int32, (span, qb), 0))
    win_bias = tile_all(jnp.where((diff >= 0) & (diff < WINDOW), 0.0, NEG))
    kvc = kvc_ref[0]

    zeros_q = jnp.zeros((d, cols), BF16)
    q_h = [jnp.concatenate([q_t[(h * grp + g) * d:(h * grp + g + 1) * d, :] for g in range(grp)],
                           axis=1).astype(BF16) for h in range(NSA_KV_HEADS)]
    q2 = jnp.concatenate([jnp.concatenate([q_h[0], zeros_q], axis=0),
                          jnp.concatenate([zeros_q, q_h[1]], axis=0)], axis=1)

    m_c, l_c, o_c, p_c = _attend(kvc[:, :LANE].astype(BF16), kvc[:, LANE:].astype(BF16), q2, cmp_bias)
    inv_l = jnp.where(m_c > 0.5 * NEG, 1.0 / l_c, 0.0)
    o_c = o_c * inv_l
    p_c = p_c * inv_l

    kv = kwin_ref[0, pl.ds(w0, span), :]
    _, l_w, o_w, _ = _attend(kv[:, :LANE].astype(BF16), kv[:, LANE:].astype(BF16), q2, win_bias)
    o_w = o_w * (1.0 / l_w)

    for h in range(NSA_KV_HEADS):
        p_sum = p_c[:, h * cols:h * cols + qb]
        for g in range(1, grp):
            p_sum = p_sum + p_c[:, h * cols + g * qb:h * cols + (g + 1) * qb]
        imp = _dot_split3_rhs(overlap_t, p_sum)
        work = jnp.where(forced, FORCE, jnp.where(blk * SEL_BLOCK <= t_q, imp, -1.0))
        sel = jnp.zeros((n_slc, qb), F32)
        for _ in range(topk):
            best = jnp.max(work, axis=0, keepdims=True)
            idx = jnp.min(jnp.where(work == best, blk_f, 1e9), axis=0, keepdims=True)
            pick = blk_f == idx
            sel = jnp.where(pick, 1.0, sel)
            work = jnp.where(pick, -3.0, work)
        sel_bias = jnp.where((sel > 0.5) & (blk * SEL_BLOCK <= t_q), 0.0, NEG)
        for b in range(n_slc):
            mask_ref[b * SEL_BLOCK:(b + 1) * SEL_BLOCK, h * qb:(h + 1) * qb] = jnp.broadcast_to(
                sel_bias[b:b + 1, :], (SEL_BLOCK, qb))
        own_keys = pl.ds(pl.multiple_of(qi * qb, qb), qb)
        mask_ref[own_keys, h * qb:(h + 1) * qb] = mask_ref[own_keys, h * qb:(h + 1) * qb] + causal_bias

    def sel_body(j, carry):
        m, l, acc = carry
        off = pl.multiple_of(j * kb, kb)
        kv = ksel_ref[0, pl.ds(off, kb), :]
        mask = mask_ref[pl.ds(off, kb), :]
        bias = jnp.concatenate([mask[:, :qb]] * grp + [mask[:, qb:]] * grp, axis=1)
        m_b, l_b, pv_b, _ = _attend(kv[:, :LANE].astype(BF16), kv[:, LANE:].astype(BF16), q2, bias)
        m_new = jnp.maximum(m, m_b)
        w_old = jnp.exp2(m - m_new)
        w_blk = jnp.exp2(m_b - m_new)
        return m_new, w_old * l + w_blk * l_b, w_old * acc + w_blk * pv_b

    init = (jnp.full((1, 2 * cols), NEG, F32), jnp.zeros((1, 2 * cols), F32), jnp.zeros((d, 2 * cols), F32))
    _, l_s, acc_s = lax.fori_loop(0, (qi * qb + qb + kb - 1) // kb, sel_body, init)
    o_s = acc_s * (1.0 / l_s)

    out_rows = []
    for hd in range(heads):
        cs = slice(hd * qb, (hd + 1) * qb)
        out_rows.append(gate_t[3 * hd:3 * hd + 1] * o_c[:, cs] + gate_t[3 * hd + 1:3 * hd + 2] * o_s[:, cs]
                        + gate_t[3 * hd + 2:3 * hd + 3] * o_w[:, cs])
    o_ref[0] = jnp.concatenate(out_rows, axis=0).T


def _nsa_attention(proj, kvc):
    bsz, seq, _ = proj.shape
    qb = min(seq, _NSA_Q_TILE)
    ncp = kvc.shape[1]
    kv = 2 * LANE
    return pl.pallas_call(
        functools.partial(_nsa_kernel, seq=seq),
        grid=(bsz, seq // qb),
        in_specs=[pl.BlockSpec((1, qb, NSA_DIM), lambda b, i: (b, i, COL_NQ // NSA_DIM)),
                  pl.BlockSpec((1, qb, LANE), lambda b, i: (b, i, COL_GATE // LANE)),
                  pl.BlockSpec((1, ncp, kv), lambda b, i: (b, 0, 0)),
                  pl.BlockSpec((1, seq, kv), lambda b, i: (b, 0, COL_SEL // kv)),
                  pl.BlockSpec((1, seq, kv), lambda b, i: (b, 0, COL_WIN // kv))],
        out_specs=pl.BlockSpec((1, qb, NSA_DIM), lambda b, i: (b, i, 0)),
        out_shape=jax.ShapeDtypeStruct((bsz, seq, NSA_DIM), F32),
        scratch_shapes=[pltpu.VMEM((seq, NSA_KV_HEADS * qb), F32)],
        compiler_params=_cparams(("parallel", "arbitrary")),
    )(proj, proj, kvc, proj, proj)


_CC_HALO = 32
_CC_ROWS = 64


def _cconv_kernel(u_ref, w_ref, b_ref, lg_ref, lb_ref, o_ref, xbuf, shifted):
    t = u_ref.shape[1]
    s = pl.program_id(1)

    @pl.when(s == 0)
    def _():
        xbuf[0:_CC_HALO, :] = jnp.zeros((_CC_HALO, CONV_CH), F32)

    @pl.when(s != 0)
    def _():
        xbuf[0:_CC_HALO, :] = xbuf[t:t + _CC_HALO, :]

    u = u_ref[0]
    xbuf[_CC_HALO:_CC_HALO + t, :] = u[:, :CONV_CH] * _sigmoid(u[:, CONV_CH:])
    w = w_ref[...]
    first = _CC_HALO - (CONV_WIDTH - 1)
    rows_kept = t + _CC_HALO - SUBLANE
    for res in range(1, SUBLANE):
        shifted[res - 1, 0:rows_kept, :] = xbuf[pl.ds(res, rows_kept), :]
    for r in range(t // _CC_ROWS):
        acc = jnp.broadcast_to(b_ref[...], (_CC_ROWS, CONV_CH))
        for j in range(CONV_WIDTH):
            res = (first + j) % SUBLANE
            base = r * _CC_ROWS + first + j - res
            tap = xbuf[base:base + _CC_ROWS, :] if res == 0 else shifted[res - 1, base:base + _CC_ROWS, :]
            acc = acc + w[j:j + 1] * tap
        mu = jnp.mean(acc, axis=-1, keepdims=True)
        var = jnp.mean(jnp.square(acc - mu), axis=-1, keepdims=True)
        hn = (acc - mu) * lax.rsqrt(var + EPS) * lg_ref[...] + lb_ref[...]
        o_ref[0, r * _CC_ROWS:(r + 1) * _CC_ROWS, :] = _silu(hn)


def _cconv(proj, dw_w, dw_b, ln_g, ln_b):
    bsz, seq, _ = proj.shape
    t = min(seq, 512)
    wu = 2 * CONV_CH
    vec = pl.BlockSpec((1, CONV_CH), lambda b, s: (0, 0))
    return pl.pallas_call(
        _cconv_kernel,
        grid=(bsz, seq // t),
        in_specs=[pl.BlockSpec((1, t, wu), lambda b, s: (b, s, COL_CU // wu)),
                  pl.BlockSpec((CONV_WIDTH, CONV_CH), lambda b, s: (0, 0)),
                  vec, vec, vec],
        out_specs=pl.BlockSpec((1, t, CONV_CH), lambda b, s: (b, s, 0)),
        out_shape=jax.ShapeDtypeStruct((bsz, seq, CONV_CH), F32),
        scratch_shapes=[pltpu.VMEM((t + _CC_HALO, CONV_CH), F32),
                        pltpu.VMEM((SUBLANE - 1, t + _CC_HALO, CONV_CH), F32)],
        compiler_params=_cparams(("parallel", "arbitrary")),
    )(proj, dw_w, dw_b, ln_g, ln_b)


_FFN_TILE = 256


def _ffn_kernel(x_ref, oa_ref, ob_ref, oc_ref, wo_ref, gmix_ref, gpre_ref, wup_ref, cw_ref, wd_ref, gpost_ref,
                o_ref, xn_ref, act_ref, stage_ref, carry_ref):
    tm = x_ref.shape[1]
    d_ff = wd_ref.shape[0]
    tf = _FFN_TILE
    hal = SUBLANE

    @pl.when(pl.program_id(1) == 0)
    def _():
        carry_ref[...] = jnp.zeros_like(carry_ref)

    mix = (jnp.dot(oa_ref[0].astype(BF16), wo_ref[:GDN_DIM], preferred_element_type=F32)
           + jnp.dot(ob_ref[0].astype(BF16), wo_ref[GDN_DIM:GDN_DIM + NSA_DIM], preferred_element_type=F32)
           + jnp.dot(oc_ref[0].astype(BF16), wo_ref[GDN_DIM + NSA_DIM:], preferred_element_type=F32))
    x = x_ref[0] + _rms(mix, gmix_ref[...])
    xn_ref[...] = _rms(x, gpre_ref[...]).astype(BF16)
    for f in range(d_ff // tf):
        ys = []
        for part in range(2):
            cols = slice(part * d_ff + f * tf, part * d_ff + (f + 1) * tf)
            h = jnp.dot(xn_ref[...], wup_ref[:, cols], preferred_element_type=F32)
            stage = stage_ref.at[f % 2, part]
            stage[0:hal, :] = carry_ref[:, cols]
            stage[hal:hal + tm, :] = h
            carry_ref[:, cols] = h[tm - hal:tm, :]
            cw = cw_ref[:, cols]
            ys.append(cw[2:3] * h + cw[1:2] * stage[pl.ds(hal - 1, tm), :]
                      + cw[0:1] * stage[pl.ds(hal - 2, tm), :])
        act_ref[:, f * tf:(f + 1) * tf] = (_silu(ys[0]) * ys[1]).astype(BF16)
    out = jnp.dot(act_ref[...], wd_ref[...], preferred_element_type=F32)
    o_ref[0] = x + _rms(out, gpost_ref[...])


def _outproj_ffn(x, o_a, o_b, o_c, layer, w_out, g_mix, g_pre, w_up, conv_w, w_down, g_post):
    bsz, seq, d = x.shape
    d_ff = w_down.shape[1]
    tm = min(seq, 512)
    kw = conv_w.shape[1]
    resident = lambda shape: pl.BlockSpec((None,) + shape, lambda b, i: (layer, 0, 0),
                                          pipeline_mode=pl.Buffered(1))
    rows = lambda width: pl.BlockSpec((1, tm, width), lambda b, i: (b, i, 0))
    return pl.pallas_call(
        _ffn_kernel,
        grid=(bsz, seq // tm),
        in_specs=[rows(d), rows(GDN_DIM), rows(NSA_DIM), rows(CONV_CH),
                  resident((d, d)),
                  pl.BlockSpec((1, d), lambda b, i: (0, 0)),
                  pl.BlockSpec((1, d), lambda b, i: (0, 0)),
                  resident((d, 2 * d_ff)),
                  resident((kw, 2 * d_ff)),
                  resident((d_ff, d)),
                  pl.BlockSpec((1, d), lambda b, i: (0, 0))],
        out_specs=pl.BlockSpec((1, tm, d), lambda b, i: (b, i, 0)),
        out_shape=jax.ShapeDtypeStruct((bsz, seq, d), F32),
        scratch_shapes=[pltpu.VMEM((tm, d), BF16),
                        pltpu.VMEM((tm, d_ff), BF16),
                        pltpu.VMEM((2, 2, tm + SUBLANE, _FFN_TILE), F32),
                        pltpu.VMEM((SUBLANE, 2 * d_ff), F32)],
        compiler_params=_cparams(("parallel", "arbitrary")),
    )(x, o_a, o_b, o_c, w_out, g_mix, g_pre, w_up, conv_w, w_down, g_post)


def _pack_w_in(w_in):
    depth, d, _ = w_in.shape
    sizes = (GDN_DIM,) * 4 + (GDN_HEADS,) * 2 + (NSA_DIM,) + (LANE,) * 6 + (3 * NSA_HEADS, 2 * CONV_CH)
    offs = np.concatenate([[0], np.cumsum(sizes)])
    w_in = w_in.astype(BF16)
    piece = lambda k: w_in[:, :, offs[k]:offs[k + 1]]
    zeros = lambda n: jnp.zeros((depth, d, n), BF16)
    gq, gk, gv, gz, ga, gb, nq, nkc, nvc, nks, nvs, nkw, nvw, ngate, cu = [piece(k) for k in range(15)]
    cols = [gq, gk, gv, gz, nq,
            ga, gb, zeros(LANE - 2 * GDN_HEADS),
            cu, nks, nvs, nkw, nvw,
            ngate, zeros(LANE - 3 * NSA_HEADS),
            nkc, nvc]
    packed = jnp.concatenate(cols, axis=-1)
    assert packed.shape[-1] == PROJ_DIM
    return packed


def _pack_compress(wk, wv, pe_k, pe_v):
    depth = wk.shape[0]
    half = CMP_LEN // 2
    d = HEAD_DIM
    w4 = jnp.stack([wk, wk, wv, wv], axis=2).astype(BF16)
    same_slot = jnp.eye(4, dtype=BF16)[None, None, :, None, :, None]
    big = (w4[:, :, :, :, None, :] * same_slot).reshape(depth, 2, half * 4 * d, 4 * d)
    pe = jnp.concatenate([pe_k, pe_k, pe_v, pe_v], axis=-1).reshape(depth, 2, 1, half * 4 * d)
    return big[:, 0], big[:, 1], pe[:, 0], pe[:, 1]


def kernel(x, positions, norm_mix_pre, norm_mix_post, norm_ffn_pre, norm_ffn_post, w_in, w_out, gdn_conv_w, gdn_a_log, gdn_dt_bias, gdn_norm_g, nsa_cmp_wk, nsa_cmp_wv, nsa_cmp_pe_k, nsa_cmp_pe_v, cc_dw_w, cc_dw_b, cc_ln_g, cc_ln_b, ffn_w_up, ffn_conv_w, ffn_w_down):
    bsz, seq, d = x.shape
    depth = w_in.shape[0]
    assert seq % Q_BLOCK == 0 and d == GDN_DIM + NSA_DIM + CONV_CH
    m = bsz * seq

    cos, sin = _rope_tables(positions)
    ncp = seq // CMP_STRIDE
    pad_rows = lambda t: jnp.pad(t[:, CMP_LEN - 1::CMP_STRIDE], ((0, 0), (0, 1), (0, 0)))
    cos_c, sin_c = pad_rows(cos), pad_rows(sin)

    w_in_p = _pack_w_in(w_in)
    w_out_b = w_out.astype(BF16)
    w_up_b = ffn_w_up.astype(BF16)
    w_down_b = ffn_w_down.astype(BF16)
    head_params = jnp.zeros((depth, SUBLANE, LANE), F32)
    head_params = head_params.at[:, 0, :GDN_HEADS].set(gdn_a_log).at[:, 1, :GDN_HEADS].set(gdn_dt_bias)

    wa, wb, pea, peb = _pack_compress(nsa_cmp_wk, nsa_cmp_wv, nsa_cmp_pe_k, nsa_cmp_pe_v)
    cos2d = cos.reshape(m, LANE)
    sin2d = sin.reshape(m, LANE)
    for l in range(depth):
        proj, cmp_cols = _inproj(x.reshape(m, d), norm_mix_pre[l][None], w_in_p, l, cos2d, sin2d)
        proj = proj.reshape(bsz, seq, PROJ_OUT)
        o_a = _gdn(proj, gdn_conv_w[l], head_params[l], jnp.tile(gdn_norm_g[l], GDN_HEADS)[None])
        kvc = _compress(cmp_cols.reshape(bsz, ncp, CMP_STRIDE * 2 * LANE), l, wa, wb, pea, peb, cos_c, sin_c)
        o_b = _nsa_attention(proj, kvc)
        o_c = _cconv(proj, cc_dw_w[l], cc_dw_b[l][None], cc_ln_g[l][None], cc_ln_b[l][None])
        x = _outproj_ffn(x, o_a, o_b, o_c, l, w_out_b, norm_mix_post[l][None], norm_ffn_pre[l][None],
                         w_up_b, ffn_conv_w, w_down_b, norm_ffn_post[l][None])
    return x
```
